```python
import jax, jax.numpy as jnp
from jax import lax
import numpy as np

D_MODEL = 1024
BATCH = 8
SEQ = 2048
DEPTH = 1
DEC_BATCH = 128
DEC_SEQ = 4
PAST_LEN = 16384
PAGE_SIZE = 128

MIX_WIDTH = D_MODEL
HG_WIDTH = MIX_WIDTH // 2
CV_WIDTH = MIX_WIDTH - HG_WIDTH
HG_HEADS = 4
HG_DK = HG_WIDTH // HG_HEADS
HG_DV = HG_DK
CV_GROUPS = 8
CV_KERNEL = 31
CV_BUF = CV_KERNEL - 1
D_FF = 4 * D_MODEL
CHUNK = 64
N_IN = 4 * HG_WIDTH + 2 * CV_WIDTH
ALPHA = (2.0 * DEPTH) ** 0.25
BETA = (8.0 * DEPTH) ** -0.25
EPS = 1e-5

kernel_name = 'hymba_hgrn2_conformer_deepnorm_adaln_step'


def _norm(x):
    xf = x.astype(jnp.float32)
    mu = jnp.mean(xf, axis=-1, keepdims=True)
    var = jnp.mean(jnp.square(xf - mu), axis=-1, keepdims=True)
    return ((xf - mu) * lax.rsqrt(var + EPS)).astype(x.dtype)


def _layer_norm(x, g, b):
    return _norm(x) * g + b


def _hgrn2_chunked(q, k, v, logf, S0):
    B, T, H, DK = q.shape
    DV = v.shape[-1]
    c = min(CHUNK, T)
    n = -(-T // c)
    pad = n * c - T
    if pad:
        cfg = ((0, 0), (0, pad), (0, 0), (0, 0))
        q, k, v, logf = (jnp.pad(a, cfg) for a in (q, k, v, logf))

    def blk(a):
        return a.reshape(B, n, c, H, a.shape[-1]).transpose(1, 0, 3, 2, 4)

    q, k, v, logf = blk(q), blk(k), blk(v), blk(logf)
    b = jnp.cumsum(logf, axis=3)
    b_last = b[:, :, :, -1, :]
    qe = q * jnp.exp(b)
    ke = k * jnp.exp(-b)
    kd = k * jnp.exp(b_last[:, :, :, None, :] - b)
    causal = jnp.tril(jnp.ones((c, c), dtype=bool))
    att = jnp.where(causal, jnp.einsum('nbhtk,nbhsk->nbhts', qe, ke), 0.0)
    o_intra = jnp.einsum('nbhts,nbhsv->nbhtv', att, v)

    def step(S, inp):
        qe_c, kd_c, v_c, bl_c = inp
        o_c = jnp.einsum('bhtk,bhkv->bhtv', qe_c, S)
        S = jnp.exp(bl_c)[..., None] * S + jnp.einsum('bhtk,bhtv->bhkv', kd_c, v_c)
        return S, o_c

    S, o_inter = lax.scan(step, S0, (qe, kd, v, b_last))
    o = (o_intra + o_inter).transpose(1, 0, 3, 2, 4).reshape(B, n * c, H, DV)[:, :T]
    return o, S


def _causal_dwconv(u, buf, w, bias):
    full = jnp.concatenate([buf, u], axis=1)
    out = lax.conv_general_dilated(
        full, w[:, None, :], window_strides=(1,), padding='VALID',
        dimension_numbers=('NWC', 'WIO', 'NWC'), feature_group_count=full.shape[-1])
    return out + bias, full[:, -CV_BUF:]


def _layer(x, c, S0, buf, lb, w_in, b_in, hg_norm_w, conv_w, conv_b, gn_g, gn_b,
           w_out, b_out, ln1_g, ln1_b, w_up, b_up, w_down, b_down, ln2_g, ln2_b,
           w_ada, b_ada):
    B, T, _ = x.shape
    mod = jax.nn.silu(c) @ w_ada + b_ada
    sh1, sc1, g1, sh2, sc2, g2 = jnp.split(mod[:, None, :], 6, axis=-1)

    h = x * (1.0 + sc1) + sh1
    z = h @ w_in + b_in
    zq, zf, zi, zg, za, zb = jnp.split(
        z, [HG_WIDTH, 2 * HG_WIDTH, 3 * HG_WIDTH, 4 * HG_WIDTH, 4 * HG_WIDTH + CV_WIDTH], axis=-1)

    heads = lambda a: a.reshape(B, T, HG_HEADS, -1).astype(jnp.float32)
    q = jax.nn.silu(heads(zq))
    lbh = lb.reshape(HG_HEADS, HG_DK)
    f = lbh + (1.0 - lbh) * jax.nn.sigmoid(heads(zf))
    o, S_new = _hgrn2_chunked(q, 1.0 - f, heads(zi), jnp.log(f), S0.astype(jnp.float32))
    o = o * lax.rsqrt(jnp.mean(jnp.square(o), axis=-1, keepdims=True) + EPS)
    o = o * hg_norm_w.astype(jnp.float32) * jax.nn.silu(heads(zg))
    o_a = o.reshape(B, T, HG_WIDTH).astype(x.dtype)

    u = za * jax.nn.sigmoid(zb)
    uc, buf_new = _causal_dwconv(u, buf, conv_w, conv_b)
    un = _norm(uc.reshape(B, T, CV_GROUPS, CV_WIDTH // CV_GROUPS)).reshape(B, T, CV_WIDTH)
    o_b = jax.nn.silu(un * gn_g + gn_b)

    mix = jnp.concatenate([o_a, o_b], axis=-1) @ w_out + b_out
    x = _layer_norm(ALPHA * x + (1.0 + g1) * mix, ln1_g, ln1_b)

    h = x * (1.0 + sc2) + sh2
    ff = jnp.square(jax.nn.relu(h @ w_up + b_up)) @ w_down + b_down
    x = _layer_norm(ALPHA * x + (1.0 + g2) * ff, ln2_g, ln2_b)
    return x, S_new.astype(x.dtype), buf_new


def setup_inputs(seed: int = 0) -> dict:
    key = jax.random.key(seed)
    ks = jax.random.split(key, 32)
    nrm = lambda k, shape, s: jax.random.normal(k, shape, jnp.float32) * s
    return {
        'x_prompt': nrm(ks[0], (BATCH, SEQ, D_MODEL), 1.0),
        'x_sample': nrm(ks[1], (DEC_BATCH, DEC_SEQ, D_MODEL), 1.0),
        'c_prompt': nrm(ks[2], (BATCH, D_MODEL), 1.0),
        'c_sample': nrm(ks[3], (DEC_BATCH, D_MODEL), 1.0),
        'state_hgrn': nrm(ks[4], (DEPTH, DEC_BATCH, HG_HEADS, HG_DK, HG_DV), 0.5),
        'state_conv': nrm(ks[5], (DEPTH, DEC_BATCH, CV_BUF, CV_WIDTH), 0.5),
        'lb_logits': nrm(ks[6], (DEPTH + 1, HG_WIDTH), 0.1),
        'w_in': nrm(ks[7], (DEPTH, D_MODEL, N_IN), D_MODEL ** -0.5),
        'b_in': nrm(ks[8], (DEPTH, N_IN), 0.02),
        'hg_norm_w': 1.0 + nrm(ks[9], (DEPTH, HG_DV), 0.02),
        'conv_w': nrm(ks[10], (DEPTH, CV_KERNEL, CV_WIDTH), CV_KERNEL ** -0.5),
        'conv_b': nrm(ks[11], (DEPTH, CV_WIDTH), 0.02),
        'gn_g': 1.0 + nrm(ks[12], (DEPTH, CV_WIDTH), 0.02),
        'gn_b': nrm(ks[13], (DEPTH, CV_WIDTH), 0.02),
        'w_out': nrm(ks[14], (DEPTH, MIX_WIDTH, D_MODEL), MIX_WIDTH ** -0.5 * BETA),
        'b_out': nrm(ks[15], (DEPTH, D_MODEL), 0.02),
        'ln1_g': 1.0 + nrm(ks[16], (DEPTH, D_MODEL), 0.02),
        'ln1_b': nrm(ks[17], (DEPTH, D_MODEL), 0.02),
        'w_up': nrm(ks[18], (DEPTH, D_MODEL, D_FF), D_MODEL ** -0.5 * BETA),
        'b_up': nrm(ks[19], (DEPTH, D_FF), 0.02),
        'w_down': nrm(ks[20], (DEPTH, D_FF, D_MODEL), D_FF ** -0.5 * BETA),
        'b_down': nrm(ks[21], (DEPTH, D_MODEL), 0.02),
        'ln2_g': 1.0 + nrm(ks[22], (DEPTH, D_MODEL), 0.02),
        'ln2_b': nrm(ks[23], (DEPTH, D_MODEL), 0.02),
        'w_ada': nrm(ks[24], (DEPTH, D_MODEL, 6 * D_MODEL), 0.3 * D_MODEL ** -0.5),
        'b_ada': nrm(ks[25], (DEPTH, 6 * D_MODEL), 0.02),
    }


def reference(x_prompt, x_sample, c_prompt, c_sample, state_hgrn, state_conv, lb_logits,
              w_in, b_in, hg_norm_w, conv_w, conv_b, gn_g, gn_b, w_out, b_out,
              ln1_g, ln1_b, w_up, b_up, w_down, b_down, ln2_g, ln2_b, w_ada, b_ada):
    lbs = jnp.cumsum(jax.nn.softmax(lb_logits.astype(jnp.float32), axis=0), axis=0)
    xp, xs = x_prompt, x_sample
    hp_list, cp_list, hs_list, cs_list = [], [], [], []
    for l in range(DEPTH):
        params = (w_in[l], b_in[l], hg_norm_w[l], conv_w[l], conv_b[l], gn_g[l], gn_b[l],
                  w_out[l], b_out[l], ln1_g[l], ln1_b[l], w_up[l], b_up[l], w_down[l],
                  b_down[l], ln2_g[l], ln2_b[l], w_ada[l], b_ada[l])
        S0p = jnp.zeros((xp.shape[0], HG_HEADS, HG_DK, HG_DV), xp.dtype)
        buf0p = jnp.zeros((xp.shape[0], CV_BUF, CV_WIDTH), xp.dtype)
        xp, Sp, bp = _layer(xp, c_prompt, S0p, buf0p, lbs[l], *params)
        xs, Ss, bs = _layer(xs, c_sample, state_hgrn[l], state_conv[l], lbs[l], *params)
        hp_list.append(Sp); cp_list.append(bp); hs_list.append(Ss); cs_list.append(bs)
    new_hgrn_prompt = jnp.stack(hp_list)
    new_conv_prompt = jnp.stack(cp_list)
    new_hgrn_sample = jnp.stack(hs_list)
    new_conv_sample = jnp.stack(cs_list)
    return (xp, xs, new_hgrn_prompt, new_conv_prompt, new_hgrn_sample, new_conv_sample)
```

```python
import functools

import jax
import jax.numpy as jnp
from jax import lax
from jax.experimental import pallas as pl
from jax.experimental.pallas import tpu as pltpu

F32 = jnp.float32
BF16 = jnp.bfloat16

D_MODEL = 1024
HG_WIDTH = 512
CV_WIDTH = 512
HG_HEADS = 4
HG_DK = 128
CV_GROUPS = 8
CV_KERNEL = 31
CV_BUF = CV_KERNEL - 1
D_FF = 4 * D_MODEL
N_IN = 4 * HG_WIDTH + 2 * CV_WIDTH
DEPTH = 1
ALPHA = (2.0 * DEPTH) ** 0.25
EPS = 1e-5

CHUNK = 64
CONV_PAD = 32
DEC_SEQ = 4

VMEM_LIMIT_BYTES = 52 * 1024 * 1024

_NT = (((1,), (1,)), ((), ()))
_TN = (((0,), (0,)), ((), ()))


def _dot(a, b):
    return jnp.dot(a, b, preferred_element_type=F32)


def _dot_nt(a, b):
    return lax.dot_general(a, b, _NT, preferred_element_type=F32)


def _dot_tn(a, b):
    return lax.dot_general(a, b, _TN, preferred_element_type=F32)


def _silu(x):
    return x * jax.nn.sigmoid(x)


def _split3(x):
    hi = x.astype(BF16).astype(F32)
    r = x - hi
    mid = r.astype(BF16).astype(F32)
    lo = (r - mid).astype(BF16).astype(F32)
    return hi, mid, lo


def _layer_norm(y, g, b):
    mu = jnp.mean(y, axis=-1, keepdims=True)
    yc = y - mu
    var = jnp.mean(yc * yc, axis=-1, keepdims=True)
    return yc * lax.rsqrt(var + EPS) * g + b


def _adaln_kernel(c_ref, w_ref, b_ref, o_ref):
    a = _silu(c_ref[...])
    a_hi = a.astype(BF16)
    a_lo = (a - a_hi.astype(F32)).astype(BF16)
    w = w_ref[...]
    w_hi = w.astype(BF16)
    w_lo = (w - w_hi.astype(F32)).astype(BF16)
    o_ref[...] = _dot(a_hi, w_hi) + _dot(a_lo, w_hi) + _dot(a_hi, w_lo) + b_ref[...]


def _adaln(c_all, w_ada, b_ada):
    rows = c_all.shape[0]
    n = w_ada.shape[1]
    tn = 1024
    return pl.pallas_call(
        _adaln_kernel,
        grid=(n // tn,),
        in_specs=[
            pl.BlockSpec((rows, D_MODEL), lambda j: (0, 0)),
            pl.BlockSpec((D_MODEL, tn), lambda j: (0, j)),
            pl.BlockSpec((1, tn), lambda j: (0, j)),
        ],
        out_specs=pl.BlockSpec((rows, tn), lambda j: (0, j)),
        out_shape=jax.ShapeDtypeStruct((rows, n), F32),
        compiler_params=pltpu.CompilerParams(
            dimension_semantics=("arbitrary",), vmem_limit_bytes=VMEM_LIMIT_BYTES),
        name="adaln",
    )(c_all, w_ada, b_ada)


def _lower_bound(lb_ref):
    lg = lb_ref[...]
    e = jnp.exp(lg - jnp.max(lg, axis=0, keepdims=True))
    p = e / jnp.sum(e, axis=0, keepdims=True)
    return jnp.sum(p[0:DEPTH, :], axis=0, keepdims=True)


def _in_projection(x, sh1, sc1, lb, w_in_ref, b_in_ref):
    hb = (x * (1.0 + sc1) + sh1).astype(BF16)

    def sec(i, width):
        return _dot(hb, w_in_ref[:, i:i + width]) + b_in_ref[:, i:i + width]

    q = _silu(sec(0, HG_WIDTH))
    f = lb + (1.0 - lb) * jax.nn.sigmoid(sec(HG_WIDTH, HG_WIDTH))
    v = sec(2 * HG_WIDTH, HG_WIDTH)
    gate = _silu(sec(3 * HG_WIDTH, HG_WIDTH))
    za = sec(4 * HG_WIDTH, CV_WIDTH)
    zb = sec(4 * HG_WIDTH + CV_WIDTH, CV_WIDTH)
    return q, 1.0 - f, jnp.log(f), v, gate, za * jax.nn.sigmoid(zb)


def _head_norm(o, hgw, gate):
    ms = jnp.mean(o * o, axis=-1, keepdims=True)
    return o * lax.rsqrt(ms + EPS) * hgw * gate


def _group_norm_swish(uc, gmat_ref, gn_g, gn_b):
    mu = _dot(uc.astype(BF16), gmat_ref[...])
    d = uc - mu
    var = _dot((d * d).astype(BF16), gmat_ref[...])
    return _silu(d * lax.rsqrt(var + EPS) * gn_g + gn_b)


def _out_projection(x, g1, mix_bf16, w_out_ref, b_out_ref, ln_g, ln_b):
    mix = _dot(mix_bf16, w_out_ref[...]) + b_out_ref[...]
    return _layer_norm(ALPHA * x + (1.0 + g1) * mix, ln_g, ln_b)


def _mixer_prompt_kernel(x_ref, mod_ref, lb_ref, w_in_ref, b_in_ref, hgw_ref, cw_ref, cb_ref,
                         gng_ref, gnb_ref, gmat_ref, w_out_ref, b_out_ref, lng_ref, lnb_ref,
                         x1_ref, s_out_ref, cbuf_out_ref,
                         q_s, k_s, lf_s, v_s, g_s, ubuf, uc_s, mix_s, st_s, *, tile):
    t = pl.program_id(1)

    @pl.when(t == 0)
    def _():
        st_s[...] = jnp.zeros_like(st_s)
        ubuf[0:CONV_PAD, :] = jnp.zeros((CONV_PAD, CV_WIDTH), F32)

    x = x_ref[0]
    sh1 = mod_ref[0, 0:1, :]
    sc1 = mod_ref[0, 1:2, :]
    g1 = mod_ref[0, 2:3, :]
    q, k, lf, v, gate, u = _in_projection(x, sh1, sc1, _lower_bound(lb_ref), w_in_ref, b_in_ref)
    q_s[...] = q
    k_s[...] = k
    lf_s[...] = lf
    v_s[...] = v
    g_s[...] = gate
    ubuf[CONV_PAD:CONV_PAD + tile, :] = u

    row = lax.broadcasted_iota(jnp.int32, (CHUNK, HG_WIDTH), 0)
    causal = (lax.broadcasted_iota(jnp.int32, (CHUNK, CHUNK), 0)
              >= lax.broadcasted_iota(jnp.int32, (CHUNK, CHUNK), 1))
    hgw = hgw_ref[...]
    for c in range(tile // CHUNK):
        rows = slice(c * CHUNK, (c + 1) * CHUNK)
        b = lf_s[rows, :]
        shift = 1
        while shift < CHUNK:
            b = b + jnp.where(row >= shift, pltpu.roll(b, shift, axis=0), 0.0)
            shift *= 2
        b_last = b[CHUNK - 1:CHUNK, :]
        kc = k_s[rows, :]
        qe = (q_s[rows, :] * jnp.exp(b)).astype(BF16)
        ke = (kc * jnp.exp(-b)).astype(BF16)
        kd = (kc * jnp.exp(b_last - b)).astype(BF16)
        decay = jnp.exp(b_last)
        vb = v_s[rows, :].astype(BF16)
        for h in range(HG_HEADS):
            hs = slice(h * HG_DK, (h + 1) * HG_DK)
            att = jnp.where(causal, _dot_nt(qe[:, hs], ke[:, hs]), 0.0).astype(BF16)
            st = st_s[h]
            o = _dot(att, vb[:, hs]) + _dot_nt(qe[:, hs], st.astype(BF16))
            st_s[h] = st * decay[:, hs] + _dot_tn(vb[:, hs], kd[:, hs])
            mix_s[rows, hs] = _head_norm(o, hgw, g_s[rows, hs]).astype(BF16)

    for r0 in range(0, tile, CHUNK):
        for c0 in range(0, CV_WIDTH, 128):
            cs = slice(c0, c0 + 128)
            acc = jnp.broadcast_to(cb_ref[:, cs], (CHUNK, 128))
            for j in range(CV_KERNEL):
                start = CONV_PAD - CV_BUF + j + r0
                acc = acc + cw_ref[j:j + 1, cs] * ubuf[start:start + CHUNK, cs]
            uc_s[r0:r0 + CHUNK, cs] = acc
    mix_s[:, HG_WIDTH:] = _group_norm_swish(
        uc_s[...], gmat_ref, gng_ref[...], gnb_ref[...]).astype(BF16)
    ubuf[0:CONV_PAD, :] = ubuf[tile:tile + CONV_PAD, :]

    x1_ref[0] = _out_projection(x, g1, mix_s[...], w_out_ref, b_out_ref, lng_ref[...], lnb_ref[...])

    @pl.when(t == pl.num_programs(1) - 1)
    def _():
        for h in range(HG_HEADS):
            s_out_ref[0, h] = st_s[h].T
        cbuf_out_ref[0] = ubuf[CONV_PAD - CV_BUF:CONV_PAD, :]


def _const_spec(shape):
    zeros = (0,) * len(shape)
    return pl.BlockSpec(shape, lambda *_: zeros, pipeline_mode=pl.Buffered(1))


def _mixer_prompt(x, mod, lb_logits, w_in, b_in, hgw, conv_w, conv_b, gn_g, gn_b, gmat,
                  w_out, b_out, ln_g, ln_b, *, tile):
    batch, seq, _ = x.shape
    kernel = functools.partial(_mixer_prompt_kernel, tile=tile)
    return pl.pallas_call(
        kernel,
        grid=(batch, seq // tile),
        in_specs=[
            pl.BlockSpec((1, tile, D_MODEL), lambda b, t: (b, t, 0)),
            pl.BlockSpec((1, 6, D_MODEL), lambda b, t: (b, 0, 0)),
            _const_spec(lb_logits.shape),
            _const_spec(w_in.shape),
            _const_spec(b_in.shape),
            _const_spec(hgw.shape),
            _const_spec(conv_w.shape),
            _const_spec(conv_b.shape),
            _const_spec(gn_g.shape),
            _const_spec(gn_b.shape),
            _const_spec(gmat.shape),
            _const_spec(w_out.shape),
            _const_spec(b_out.shape),
            _const_spec(ln_g.shape),
            _const_spec(ln_b.shape),
        ],
        out_specs=[
            pl.BlockSpec((1, tile, D_MODEL), lambda b, t: (b, t, 0)),
            pl.BlockSpec((1, HG_HEADS, HG_DK, HG_DK), lambda b, t: (b, 0, 0, 0)),
            pl.BlockSpec((1, CV_BUF, CV_WIDTH), lambda b, t: (b, 0, 0)),
        ],
        out_shape=[
            jax.ShapeDtypeStruct((batch, seq, D_MODEL), F32),
            jax.ShapeDtypeStruct((batch, HG_HEADS, HG_DK, HG_DK), F32),
            jax.ShapeDtypeStruct((batch, CV_BUF, CV_WIDTH), F32),
        ],
        scratch_shapes=[
            pltpu.VMEM((tile, HG_WIDTH), F32),
            pltpu.VMEM((tile, HG_WIDTH), F32),
            pltpu.VMEM((tile, HG_WIDTH), F32),
            pltpu.VMEM((tile, HG_WIDTH), F32),
            pltpu.VMEM((tile, HG_WIDTH), F32),
            pltpu.VMEM((CONV_PAD + tile, CV_WIDTH), F32),
            pltpu.VMEM((tile, CV_WIDTH), F32),
            pltpu.VMEM((tile, D_MODEL), BF16),
            pltpu.VMEM((HG_HEADS, HG_DK, HG_DK), F32),
        ],
        compiler_params=pltpu.CompilerParams(
            dimension_semantics=("arbitrary", "arbitrary"), vmem_limit_bytes=VMEM_LIMIT_BYTES),
        name="mixer_prompt",
    )(x, mod, lb_logits, w_in, b_in, hgw, conv_w, conv_b, gn_g, gn_b, gmat, w_out, b_out,
      ln_g, ln_b)


def _mixer_sample_kernel(x_ref, mod_ref, s_in_ref, cbuf_in_ref, lb_ref, w_in_ref, b_in_ref, hgw_ref,
                         cw_ref, cb_ref, gng_ref, gnb_ref, gmat_ref, w_out_ref, b_out_ref,
                         lng_ref, lnb_ref,
                         x1_ref, s_out_ref, cbuf_out_ref,
                         full_s, uc_s, mix_s, *, seqs):
    rows_n = seqs * DEC_SEQ
    x = x_ref[...]
    sh1 = mod_ref[:, 0:D_MODEL]
    sc1 = mod_ref[:, D_MODEL:2 * D_MODEL]
    g1 = mod_ref[:, 2 * D_MODEL:3 * D_MODEL]
    q, k, lf, v, gate, u = _in_projection(x, sh1, sc1, _lower_bound(lb_ref), w_in_ref, b_in_ref)

    pos = lax.broadcasted_iota(jnp.int32, (rows_n, HG_WIDTH), 0) % DEC_SEQ
    b = lf
    for s in range(1, DEC_SEQ):
        b = b + jnp.where(pos >= s, pltpu.roll(lf, s, axis=0), 0.0)
    b_last = b
    for s in range(1, DEC_SEQ):
        b_last = jnp.where(pos == DEC_SEQ - 1 - s, pltpu.roll(b, rows_n - s, axis=0), b_last)
    qe = q * jnp.exp(b)
    ke = k * jnp.exp(-b)
    kd = k * jnp.exp(b_last - b)
    d_hi, d_mid, d_lo = _split3(jnp.exp(b_last))
    dsplit = jnp.where(pos == 0, d_hi, jnp.where(pos == 1, d_mid, jnp.where(pos == 2, d_lo, 0.0)))
    hgw = hgw_ref[...]

    r8 = lax.broadcasted_iota(jnp.int32, (8, HG_DK), 0)
    first = r8 < DEC_SEQ
    ones_a = jnp.where(first, 1.0, 0.0).astype(BF16)
    ones_b = jnp.where(first, 0.0, 1.0).astype(BF16)
    i8 = lax.broadcasted_iota(jnp.int32, (8, 8), 0)
    j8 = lax.broadcasted_iota(jnp.int32, (8, 8), 1)
    causal = (i8 >= j8) & ((i8 < DEC_SEQ) == (j8 < DEC_SEQ))
    for p in range(seqs // 2):
        rows = slice(8 * p, 8 * p + 8)
        for h in range(HG_HEADS):
            hs = slice(h * HG_DK, (h + 1) * HG_DK)
            qe8 = qe[rows, hs].astype(BF16)
            ke8 = ke[rows, hs].astype(BF16)
            v8 = v[rows, hs].astype(BF16)
            kd8 = kd[rows, hs]
            kd_a = jnp.where(first, kd8, 0.0).astype(BF16)
            kd_b = jnp.where(first, 0.0, kd8).astype(BF16)
            d8 = dsplit[rows, hs].astype(BF16)
            att = jnp.where(causal, _dot_nt(qe8, ke8), 0.0).astype(BF16)
            s_a = s_in_ref[2 * p, h]
            s_b = s_in_ref[2 * p + 1, h]
            inter = jnp.where(first, _dot(qe8, s_a.astype(BF16)), _dot(qe8, s_b.astype(BF16)))
            o = _dot(att, v8) + inter
            s_out_ref[2 * p, h] = _dot_tn(d8, ones_a) * s_a + _dot_tn(kd_a, v8)
            s_out_ref[2 * p + 1, h] = _dot_tn(d8, ones_b) * s_b + _dot_tn(kd_b, v8)
            mix_s[rows, hs] = _head_norm(o, hgw, gate[rows, hs]).astype(BF16)

    full_s[:, 0:CV_BUF, :] = cbuf_in_ref[...]
    for s in range(seqs):
        full_s[s, CV_BUF:CV_BUF + DEC_SEQ, :] = u[s * DEC_SEQ:(s + 1) * DEC_SEQ, :]
    acc = jnp.broadcast_to(cb_ref[...][None], (seqs, DEC_SEQ, CV_WIDTH))
    for j in range(CV_KERNEL):
        acc = acc + cw_ref[j:j + 1, :][None] * full_s[:, j:j + DEC_SEQ, :]
    for s in range(seqs):
        uc_s[s * DEC_SEQ:(s + 1) * DEC_SEQ, :] = acc[s]
    cbuf_out_ref[...] = full_s[:, DEC_SEQ:DEC_SEQ + CV_BUF, :]
    mix_s[:, HG_WIDTH:] = _group_norm_swish(
        uc_s[...], gmat_ref, gng_ref[...], gnb_ref[...]).astype(BF16)

    x1_ref[...] = _out_projection(x, g1, mix_s[...], w_out_ref, b_out_ref, lng_ref[...], lnb_ref[...])


def _mixer_sample(x, modtok, s_in, cbuf_in, lb_logits, w_in, b_in, hgw, conv_w, conv_b, gn_g, gn_b,
                  gmat, w_out, b_out, ln_g, ln_b, *, seqs):
    n_seq = s_in.shape[0]
    rows = seqs * DEC_SEQ
    kernel = functools.partial(_mixer_sample_kernel, seqs=seqs)
    return pl.pallas_call(
        kernel,
        grid=(n_seq // seqs,),
        in_specs=[
            pl.BlockSpec((rows, D_MODEL), lambda i: (i, 0)),
            pl.BlockSpec((rows, 3 * D_MODEL), lambda i: (i, 0)),
            pl.BlockSpec((seqs, HG_HEADS, HG_DK, HG_DK), lambda i: (i, 0, 0, 0)),
            pl.BlockSpec((seqs, CV_BUF, CV_WIDTH), lambda i: (i, 0, 0)),
            _const_spec(lb_logits.shape),
            _const_spec(w_in.shape),
            _const_spec(b_in.shape),
            _const_spec(hgw.shape),
            _const_spec(conv_w.shape),
            _const_spec(conv_b.shape),
            _const_spec(gn_g.shape),
            _const_spec(gn_b.shape),
            _const_spec(gmat.shape),
            _const_spec(w_out.shape),
            _const_spec(b_out.shape),
            _const_spec(ln_g.shape),
            _const_spec(ln_b.shape),
        ],
        out_specs=[
            pl.BlockSpec((rows, D_MODEL), lambda i: (i, 0)),
            pl.BlockSpec((seqs, HG_HEADS, HG_DK, HG_DK), lambda i: (i, 0, 0, 0)),
            pl.BlockSpec((seqs, CV_BUF, CV_WIDTH), lambda i: (i, 0, 0)),
        ],
        out_shape=[
            jax.ShapeDtypeStruct((n_seq * DEC_SEQ, D_MODEL), F32),
            jax.ShapeDtypeStruct(s_in.shape, F32),
            jax.ShapeDtypeStruct(cbuf_in.shape, F32),
        ],
        scratch_shapes=[
            pltpu.VMEM((seqs, CV_BUF + DEC_SEQ, CV_WIDTH), F32),
            pltpu.VMEM((rows, CV_WIDTH), F32),
            pltpu.VMEM((rows, D_MODEL), BF16),
        ],
        compiler_params=pltpu.CompilerParams(
            dimension_semantics=("arbitrary",), vmem_limit_bytes=VMEM_LIMIT_BYTES),
        name="mixer_sample",
    )(x, modtok, s_in, cbuf_in, lb_logits, w_in, b_in, hgw, conv_w, conv_b, gn_g, gn_b, gmat,
      w_out, b_out, ln_g, ln_b)


FF_CHUNK = 1024


def _mlp_kernel(x_ref, sh_ref, sc_ref, g_ref, w_up_ref, b_up_ref, w_down_ref, b_down_ref,
                lng_ref, lnb_ref, o_ref):
    x = x_ref[...]
    hb = (x * (1.0 + sc_ref[...]) + sh_ref[...]).astype(BF16)
    ff = jnp.broadcast_to(b_down_ref[...], x.shape)
    for c0 in range(0, D_FF, FF_CHUNK):
        cs = slice(c0, c0 + FF_CHUNK)
        a = jnp.maximum(_dot(hb, w_up_ref[:, cs]) + b_up_ref[:, cs], 0.0)
        ff = ff + _dot((a * a).astype(BF16), w_down_ref[cs, :])
    o_ref[...] = _layer_norm(ALPHA * x + (1.0 + g_ref[...]) * ff, lng_ref[...], lnb_ref[...])


def _mlp(x, mod_specs, mods, w_up, b_up, w_down, b_down, ln_g, ln_b, *, tile, name):
    rows = x.shape[0]
    return pl.pallas_call(
        _mlp_kernel,
        grid=(rows // tile,),
        in_specs=[pl.BlockSpec((tile, D_MODEL), lambda i: (i, 0))] + mod_specs + [
            _const_spec(w_up.shape),
            _const_spec(b_up.shape),
            _const_spec(w_down.shape),
            _const_spec(b_down.shape),
            _const_spec(ln_g.shape),
            _const_spec(ln_b.shape),
        ],
        out_specs=pl.BlockSpec((tile, D_MODEL), lambda i: (i, 0)),
        out_shape=jax.ShapeDtypeStruct((rows, D_MODEL), F32),
        compiler_params=pltpu.CompilerParams(
            dimension_semantics=("arbitrary",), vmem_limit_bytes=VMEM_LIMIT_BYTES),
        name=name,
    )(x, *mods, w_up, b_up, w_down, b_down, ln_g, ln_b)


PROMPT_TILE = 256
SAMPLE_SEQS = 16
MLP_TILE = 512


def kernel(x_prompt, x_sample, c_prompt, c_sample, state_hgrn, state_conv, lb_logits, w_in, b_in,
           hg_norm_w, conv_w, conv_b, gn_g, gn_b, w_out, b_out, ln1_g, ln1_b, w_up, b_up, w_down,
           b_down, ln2_g, ln2_b, w_ada, b_ada):
    assert w_in.shape[0] == DEPTH
    batch, seq, _ = x_prompt.shape
    dec_batch, dec_seq, _ = x_sample.shape
    assert dec_seq == DEC_SEQ

    group = jnp.arange(CV_WIDTH, dtype=jnp.int32) // (CV_WIDTH // CV_GROUPS)
    gmat = jnp.where(group[:, None] == group[None, :], CV_GROUPS / CV_WIDTH, 0.0).astype(BF16)

    xp = x_prompt
    xs = x_sample.reshape(dec_batch * DEC_SEQ, D_MODEL)
    hp, cp, hs, cs = [], [], [], []
    for l in range(DEPTH):
        row = lambda a: a[l][None, :]
        w_in_b, w_out_b = w_in[l].astype(BF16), w_out[l].astype(BF16)
        w_up_b, w_down_b = w_up[l].astype(BF16), w_down[l].astype(BF16)

        mod = _adaln(jnp.concatenate([c_prompt, c_sample], axis=0), w_ada[l], row(b_ada))
        mod_p = mod[:batch].reshape(batch, 6, D_MODEL)
        mod_s = jnp.repeat(mod[batch:], DEC_SEQ, axis=0)

        mixer_params = (lb_logits, w_in_b, row(b_in), row(hg_norm_w), conv_w[l], row(conv_b),
                        row(gn_g), row(gn_b), gmat, w_out_b, row(b_out), row(ln1_g), row(ln1_b))
        xp, sp, bp = _mixer_prompt(xp, mod_p, *mixer_params, tile=PROMPT_TILE)
        xs, ss, bs = _mixer_sample(xs, mod_s[:, :3 * D_MODEL], state_hgrn[l], state_conv[l],
                                   *mixer_params, seqs=SAMPLE_SEQS)

        mlp_params = (w_up_b, row(b_up), w_down_b, row(b_down), row(ln2_g), row(ln2_b))
        tiles_per_seq = seq // MLP_TILE
        p_specs = [pl.BlockSpec((None, None, 1, D_MODEL),
                                functools.partial(lambda i, r: (i // tiles_per_seq, r, 0, 0), r=r))
                   for r in (3, 4, 5)]
        mod_p4 = mod_p.reshape(batch, 6, 1, D_MODEL)
        xp = _mlp(xp.reshape(batch * seq, D_MODEL), p_specs, (mod_p4,) * 3, *mlp_params,
                  tile=MLP_TILE, name="mlp_prompt").reshape(batch, seq, D_MODEL)
        s_specs = [pl.BlockSpec((MLP_TILE, D_MODEL), functools.partial(lambda i, r: (i, r), r=r))
                   for r in (3, 4, 5)]
        xs = _mlp(xs, s_specs, (mod_s,) * 3, *mlp_params, tile=MLP_TILE, name="mlp_sample")

        hp.append(sp); cp.append(bp); hs.append(ss); cs.append(bs)

    return (xp, xs.reshape(dec_batch, DEC_SEQ, D_MODEL), jnp.stack(hp), jnp.stack(cp),
            jnp.stack(hs), jnp.stack(cs))
```

```python
import functools

import jax
import jax.numpy as jnp
from jax import lax
from jax.experimental import pallas as pl
from jax.experimental.pallas import tpu as pltpu

F32 = jnp.float32
BF16 = jnp.bfloat16

D_MODEL = 1024
HG_WIDTH = 512
CV_WIDTH = 512
HG_HEADS = 4
HG_DK = 128
CV_GROUPS = 8
CV_KERNEL = 31
CV_BUF = CV_KERNEL - 1
D_FF = 4 * D_MODEL
N_IN = 4 * HG_WIDTH + 2 * CV_WIDTH
DEPTH = 1
ALPHA = (2.0 * DEPTH) ** 0.25
EPS = 1e-5

CHUNK = 64
SUBLANES = 8
CONV_PAD = 32
CONV_ROWS = 128
DEC_SEQ = 4

VMEM_LIMIT_BYTES = 52 * 1024 * 1024

_NT = (((1,), (1,)), ((), ()))
_TN = (((0,), (0,)), ((), ()))


def _dot(a, b):
    return jnp.dot(a, b, preferred_element_type=F32)


def _dot_nt(a, b):
    return lax.dot_general(a, b, _NT, preferred_element_type=F32)


def _dot_tn(a, b):
    return lax.dot_general(a, b, _TN, preferred_element_type=F32)


def _silu(x):
    return x * jax.nn.sigmoid(x)


def _split3(x):
    hi = x.astype(BF16).astype(F32)
    r = x - hi
    mid = r.astype(BF16).astype(F32)
    lo = (r - mid).astype(BF16).astype(F32)
    return hi, mid, lo


def _layer_norm(y, g, b):
    mu = jnp.mean(y, axis=-1, keepdims=True)
    yc = y - mu
    var = jnp.mean(yc * yc, axis=-1, keepdims=True)
    return yc * lax.rsqrt(var + EPS) * g + b


def _adaln_kernel(c_ref, w_ref, b_ref, o_ref):
    a = _silu(c_ref[...])
    a_hi = a.astype(BF16)
    a_lo = (a - a_hi.astype(F32)).astype(BF16)
    w = w_ref[...]
    w_hi = w.astype(BF16)
    w_lo = (w - w_hi.astype(F32)).astype(BF16)
    o_ref[...] = _dot(a_hi, w_hi) + _dot(a_lo, w_hi) + _dot(a_hi, w_lo) + b_ref[...]


def _adaln(c_all, w_ada, b_ada):
    rows = c_all.shape[0]
    n = w_ada.shape[1]
    tn = 1024
    return pl.pallas_call(
        _adaln_kernel,
        grid=(n // tn,),
        in_specs=[
            pl.BlockSpec((rows, D_MODEL), lambda j: (0, 0)),
            pl.BlockSpec((D_MODEL, tn), lambda j: (0, j)),
            pl.BlockSpec((1, tn), lambda j: (0, j)),
        ],
        out_specs=pl.BlockSpec((rows, tn), lambda j: (0, j)),
        out_shape=jax.ShapeDtypeStruct((rows, n), F32),
        compiler_params=pltpu.CompilerParams(
            dimension_semantics=("arbitrary",), vmem_limit_bytes=VMEM_LIMIT_BYTES),
        name="adaln",
    )(c_all, w_ada, b_ada)


def _lower_bound(lb_ref):
    lg = lb_ref[...]
    e = jnp.exp(lg - jnp.max(lg, axis=0, keepdims=True))
    p = e / jnp.sum(e, axis=0, keepdims=True)
    return jnp.sum(p[0:DEPTH, :], axis=0, keepdims=True)


def _in_projection(hb, lb, w_in_ref, b_in_ref):
    def sec(i, width):
        return _dot(hb, w_in_ref[:, i:i + width]) + b_in_ref[:, i:i + width]

    q = _silu(sec(0, HG_WIDTH))
    f = lb + (1.0 - lb) * jax.nn.sigmoid(sec(HG_WIDTH, HG_WIDTH))
    v = sec(2 * HG_WIDTH, HG_WIDTH)
    gate = _silu(sec(3 * HG_WIDTH, HG_WIDTH))
    za = sec(4 * HG_WIDTH, CV_WIDTH)
    zb = sec(4 * HG_WIDTH + CV_WIDTH, CV_WIDTH)
    return q, 1.0 - f, jnp.log(f), v, gate, za * jax.nn.sigmoid(zb)


def _head_norm(o, hgw, gate):
    ms = jnp.mean(o * o, axis=-1, keepdims=True)
    return o * lax.rsqrt(ms + EPS) * hgw * gate


def _group_norm_swish(uc, gmat_ref, gn_g, gn_b):
    mu = _dot(uc.astype(BF16), gmat_ref[...])
    d = uc - mu
    var = _dot((d * d).astype(BF16), gmat_ref[...])
    return _silu(d * lax.rsqrt(var + EPS) * gn_g + gn_b)


def _out_projection(mix_bf16, w_out_ref, b_out_ref):
    return _dot(mix_bf16, w_out_ref[...]) + b_out_ref[...]


def _mixer_prompt_kernel(x_ref, mod_ref, lb_ref, w_in_ref, b_in_ref, hgw_ref, cw_ref, cb_ref,
                         gng_ref, gnb_ref, gmat_ref, w_out_ref, b_out_ref, lng_ref, lnb_ref,
                         x1_ref, s_out_ref, cbuf_out_ref,
                         q_s, k_s, lf_s, v_s, g_s, ubuf, ush, uc_s, mix_s, st_s, *, tile):
    t = pl.program_id(1)

    @pl.when(t == 0)
    def _():
        st_s[...] = jnp.zeros_like(st_s)
        ubuf[0:CONV_PAD, :] = jnp.zeros((CONV_PAD, CV_WIDTH), F32)

    x = x_ref[0]
    sh1 = mod_ref[0, 0:1, :]
    sc1 = mod_ref[0, 1:2, :]
    g1 = mod_ref[0, 2:3, :]
    hb = (x * (1.0 + sc1) + sh1).astype(BF16)
    q, k, lf, v, gate, u = _in_projection(hb, _lower_bound(lb_ref), w_in_ref, b_in_ref)
    q_s[...] = q
    k_s[...] = k
    lf_s[...] = lf
    v_s[...] = v
    g_s[...] = gate
    ubuf[CONV_PAD:CONV_PAD + tile, :] = u

    row = lax.broadcasted_iota(jnp.int32, (CHUNK, HG_WIDTH), 0)
    causal = (lax.broadcasted_iota(jnp.int32, (CHUNK, CHUNK), 0)
              >= lax.broadcasted_iota(jnp.int32, (CHUNK, CHUNK), 1))
    hgw = hgw_ref[...]
    for c in range(tile // CHUNK):
        rows = slice(c * CHUNK, (c + 1) * CHUNK)
        b = lf_s[rows, :]
        shift = 1
        while shift < CHUNK:
            b = b + jnp.where(row >= shift, pltpu.roll(b, shift, axis=0), 0.0)
            shift *= 2
        b_last = b[CHUNK - 1:CHUNK, :]
        kc = k_s[rows, :]
        qe = (q_s[rows, :] * jnp.exp(b)).astype(BF16)
        ke = (kc * jnp.exp(-b)).astype(BF16)
        kd = (kc * jnp.exp(b_last - b)).astype(BF16)
        decay = jnp.exp(b_last)
        vb = v_s[rows, :].astype(BF16)
        for h in range(HG_HEADS):
            hs = slice(h * HG_DK, (h + 1) * HG_DK)
            att = jnp.where(causal, _dot_nt(qe[:, hs], ke[:, hs]), 0.0).astype(BF16)
            st = st_s[h]
            o = _dot(att, vb[:, hs]) + _dot_nt(qe[:, hs], st.astype(BF16))
            st_s[h] = st * decay[:, hs] + _dot_tn(vb[:, hs], kd[:, hs])
            mix_s[rows, hs] = _head_norm(o, hgw, g_s[rows, hs]).astype(BF16)

    first_off = CONV_PAD - CV_BUF
    shifted_rows = ush.shape[1]
    for phase in range(1, SUBLANES):
        ush[phase - 1] = ubuf[phase:phase + shifted_rows, :]
    for r0 in range(0, tile, CONV_ROWS):
        for c0 in range(0, CV_WIDTH, 128):
            cs = slice(c0, c0 + 128)
            acc = jnp.broadcast_to(cb_ref[:, cs], (CONV_ROWS, 128))
            for phase in range(SUBLANES):
                taps = [j for j in range(CV_KERNEL) if (first_off + j) % SUBLANES == phase]
                span = (first_off + taps[-1]) // SUBLANES * SUBLANES
                rows = slice(r0, r0 + span + CONV_ROWS)
                win = ubuf[rows, cs] if phase == 0 else ush[phase - 1, rows, cs]
                for j in taps:
                    a0 = first_off + j - phase
                    acc = acc + cw_ref[j:j + 1, cs] * win[a0:a0 + CONV_ROWS, :]
            uc_s[r0:r0 + CONV_ROWS, cs] = acc
    mix_s[:, HG_WIDTH:] = _group_norm_swish(
        uc_s[...], gmat_ref, gng_ref[...], gnb_ref[...]).astype(BF16)
    ubuf[0:CONV_PAD, :] = ubuf[tile:tile + CONV_PAD, :]

    mix = _out_projection(mix_s[...], w_out_ref, b_out_ref)
    x1_ref[0] = _layer_norm(ALPHA * x + (1.0 + g1) * mix, lng_ref[...], lnb_ref[...])

    @pl.when(t == pl.num_programs(1) - 1)
    def _():
        for h in range(HG_HEADS):
            s_out_ref[0, h] = st_s[h].T
        cbuf_out_ref[0] = ubuf[CONV_PAD - CV_BUF:CONV_PAD, :]


def _const_spec(shape):
    zeros = (0,) * len(shape)
    return pl.BlockSpec(shape, lambda *_: zeros, pipeline_mode=pl.Buffered(1))


def _mixer_prompt(x, mod, lb_logits, w_in, b_in, hgw, conv_w, conv_b, gn_g, gn_b, gmat,
                  w_out, b_out, ln_g, ln_b, *, tile):
    batch, seq, _ = x.shape
    kernel = functools.partial(_mixer_prompt_kernel, tile=tile)
    return pl.pallas_call(
        kernel,
        grid=(batch, seq // tile),
        in_specs=[
            pl.BlockSpec((1, tile, D_MODEL), lambda b, t: (b, t, 0)),
            pl.BlockSpec((1, 6, D_MODEL), lambda b, t: (b, 0, 0)),
            _const_spec(lb_logits.shape),
            _const_spec(w_in.shape),
            _const_spec(b_in.shape),
            _const_spec(hgw.shape),
            _const_spec(conv_w.shape),
            _const_spec(conv_b.shape),
            _const_spec(gn_g.shape),
            _const_spec(gn_b.shape),
            _const_spec(gmat.shape),
            _const_spec(w_out.shape),
            _const_spec(b_out.shape),
            _const_spec(ln_g.shape),
            _const_spec(ln_b.shape),
        ],
        out_specs=[
            pl.BlockSpec((1, tile, D_MODEL), lambda b, t: (b, t, 0)),
            pl.BlockSpec((1, HG_HEADS, HG_DK, HG_DK), lambda b, t: (b, 0, 0, 0)),
            pl.BlockSpec((1, CV_BUF, CV_WIDTH), lambda b, t: (b, 0, 0)),
        ],
        out_shape=[
            jax.ShapeDtypeStruct((batch, seq, D_MODEL), F32),
            jax.ShapeDtypeStruct((batch, HG_HEADS, HG_DK, HG_DK), F32),
            jax.ShapeDtypeStruct((batch, CV_BUF, CV_WIDTH), F32),
        ],
        scratch_shapes=[
            pltpu.VMEM((tile, HG_WIDTH), F32),
            pltpu.VMEM((tile, HG_WIDTH), F32),
            pltpu.VMEM((tile, HG_WIDTH), F32),
            pltpu.VMEM((tile, HG_WIDTH), F32),
            pltpu.VMEM((tile, HG_WIDTH), F32),
            pltpu.VMEM((CONV_PAD + tile, CV_WIDTH), F32),
            pltpu.VMEM((SUBLANES - 1, CONV_PAD + tile - SUBLANES, CV_WIDTH), F32),
            pltpu.VMEM((tile, CV_WIDTH), F32),
            pltpu.VMEM((tile, D_MODEL), BF16),
            pltpu.VMEM((HG_HEADS, HG_DK, HG_DK), F32),
        ],
        compiler_params=pltpu.CompilerParams(
            dimension_semantics=("arbitrary", "arbitrary"), vmem_limit_bytes=VMEM_LIMIT_BYTES),
        name="mixer_prompt",
    )(x, mod, lb_logits, w_in, b_in, hgw, conv_w, conv_b, gn_g, gn_b, gmat, w_out, b_out,
      ln_g, ln_b)


def _mixer_sample_kernel(x_ref, mod_ref, s_in_ref, cbuf_in_ref, lb_ref, w_in_ref, b_in_ref, hgw_ref,
                         cw_ref, cb_ref, gng_ref, gnb_ref, gmat_ref, w_out_ref, b_out_ref,
                         lng_ref, lnb_ref,
                         x1_ref, s_out_ref, cbuf_out_ref,
                         qe_s, ke_s, kd_s, v_s, d_s, o_s, *, seqs):
    rows_n = seqs * DEC_SEQ
    half = seqs // 2
    x = x_ref[...]
    sh1 = mod_ref[:, 0:D_MODEL]
    sc1 = mod_ref[:, D_MODEL:2 * D_MODEL]
    g1 = mod_ref[:, 2 * D_MODEL:3 * D_MODEL]
    hb = (x * (1.0 + sc1) + sh1).astype(BF16).reshape(rows_n, D_MODEL)
    q, k, lf, v, gate, u = _in_projection(hb, _lower_bound(lb_ref), w_in_ref, b_in_ref)

    slab = lambda a, t: a[t * seqs:(t + 1) * seqs, :]
    b = [slab(lf, 0)]
    for t in range(1, DEC_SEQ):
        b.append(b[-1] + slab(lf, t))
    b_last = b[-1]
    parts = _split3(jnp.exp(b_last)) + (jnp.zeros_like(b_last),)
    def put(ref, t, val):
        for h in range(HG_HEADS):
            ref[h, t * seqs:(t + 1) * seqs, :] = val[:, h * HG_DK:(h + 1) * HG_DK]

    for t in range(DEC_SEQ):
        put(qe_s, t, slab(q, t) * jnp.exp(b[t]))
        put(ke_s, t, slab(k, t) * jnp.exp(-b[t]))
        put(kd_s, t, slab(k, t) * jnp.exp(b_last - b[t]))
        put(d_s, t, parts[t])
        put(v_s, t, slab(v, t))

    r8 = lax.broadcasted_iota(jnp.int32, (2 * DEC_SEQ, HG_DK), 0)
    even = r8 % 2 == 0
    ones_a = jnp.where(even, 1.0, 0.0).astype(BF16)
    ones_b = jnp.where(even, 0.0, 1.0).astype(BF16)
    i8 = lax.broadcasted_iota(jnp.int32, (2 * DEC_SEQ, 2 * DEC_SEQ), 0)
    j8 = lax.broadcasted_iota(jnp.int32, (2 * DEC_SEQ, 2 * DEC_SEQ), 1)
    causal = (i8 // 2 >= j8 // 2) & (i8 % 2 == j8 % 2)
    for s in range(half):
        rows = pl.ds(s, 2 * DEC_SEQ, stride=half)
        for h in range(HG_HEADS):
            qe8 = qe_s[h, rows, :].astype(BF16)
            ke8 = ke_s[h, rows, :].astype(BF16)
            v8 = v_s[h, rows, :].astype(BF16)
            kd8 = kd_s[h, rows, :]
            kd_a = jnp.where(even, kd8, 0.0).astype(BF16)
            kd_b = jnp.where(even, 0.0, kd8).astype(BF16)
            d8 = d_s[h, rows, :].astype(BF16)
            att = jnp.where(causal, _dot_nt(qe8, ke8), 0.0).astype(BF16)
            s_a = s_in_ref[s, h]
            s_b = s_in_ref[s + half, h]
            inter = jnp.where(even, _dot(qe8, s_a.astype(BF16)), _dot(qe8, s_b.astype(BF16)))
            o_s[h, rows, :] = _dot(att, v8) + inter
            s_out_ref[s, h] = _dot_tn(d8, ones_a) * s_a + _dot_tn(kd_a, v8)
            s_out_ref[s + half, h] = _dot_tn(d8, ones_b) * s_b + _dot_tn(kd_b, v8)
    hgw = hgw_ref[...]
    o_a = jnp.concatenate(
        [_head_norm(o_s[h], hgw, gate[:, h * HG_DK:(h + 1) * HG_DK]) for h in range(HG_HEADS)],
        axis=-1)

    full = lambda i: cbuf_in_ref[i] if i < CV_BUF else slab(u, i - CV_BUF)
    acc = [jnp.broadcast_to(cb_ref[...], (seqs, CV_WIDTH)) for _ in range(DEC_SEQ)]
    for i in range(CV_BUF + DEC_SEQ):
        f_i = full(i)
        for t in range(DEC_SEQ):
            if 0 <= i - t < CV_KERNEL:
                acc[t] = acc[t] + cw_ref[i - t:i - t + 1, :] * f_i
    for i in range(CV_BUF):
        cbuf_out_ref[i] = full(i + DEC_SEQ)
    o_b = _group_norm_swish(jnp.concatenate(acc, axis=0), gmat_ref, gng_ref[...], gnb_ref[...])

    mix = _out_projection(jnp.concatenate([o_a, o_b], axis=-1).astype(BF16), w_out_ref, b_out_ref)
    y = ALPHA * x + (1.0 + g1) * mix.reshape(DEC_SEQ, seqs, D_MODEL)
    x1_ref[...] = _layer_norm(y, lng_ref[...], lnb_ref[...])


def _mixer_sample(x, mod, s_in, cbuf_in, lb_logits, w_in, b_in, hgw, conv_w, conv_b, gn_g, gn_b,
                  gmat, w_out, b_out, ln_g, ln_b, *, seqs):
    n_seq = s_in.shape[0]
    rows = seqs * DEC_SEQ
    kernel = functools.partial(_mixer_sample_kernel, seqs=seqs)
    return pl.pallas_call(
        kernel,
        grid=(n_seq // seqs,),
        in_specs=[
            pl.BlockSpec((DEC_SEQ, seqs, D_MODEL), lambda i: (0, i, 0)),
            pl.BlockSpec((seqs, 3 * D_MODEL), lambda i: (i, 0)),
            pl.BlockSpec((seqs, HG_HEADS, HG_DK, HG_DK), lambda i: (i, 0, 0, 0)),
            pl.BlockSpec((CV_BUF, seqs, CV_WIDTH), lambda i: (0, i, 0)),
            _const_spec(lb_logits.shape),
            _const_spec(w_in.shape),
            _const_spec(b_in.shape),
            _const_spec(hgw.shape),
            _const_spec(conv_w.shape),
            _const_spec(conv_b.shape),
            _const_spec(gn_g.shape),
            _const_spec(gn_b.shape),
            _const_spec(gmat.shape),
            _const_spec(w_out.shape),
            _const_spec(b_out.shape),
            _const_spec(ln_g.shape),
            _const_spec(ln_b.shape),
        ],
        out_specs=[
            pl.BlockSpec((DEC_SEQ, seqs, D_MODEL), lambda i: (0, i, 0)),
            pl.BlockSpec((seqs, HG_HEADS, HG_DK, HG_DK), lambda i: (i, 0, 0, 0)),
            pl.BlockSpec((CV_BUF, seqs, CV_WIDTH), lambda i: (0, i, 0)),
        ],
        out_shape=[
            jax.ShapeDtypeStruct((DEC_SEQ, n_seq, D_MODEL), F32),
            jax.ShapeDtypeStruct(s_in.shape, F32),
            jax.ShapeDtypeStruct(cbuf_in.shape, F32),
        ],
        scratch_shapes=[pltpu.VMEM((HG_HEADS, rows, HG_DK), F32)] * 6,
        compiler_params=pltpu.CompilerParams(
            dimension_semantics=("arbitrary",), vmem_limit_bytes=VMEM_LIMIT_BYTES),
        name="mixer_sample",
    )(x, mod, s_in, cbuf_in, lb_logits, w_in, b_in, hgw, conv_w, conv_b, gn_g, gn_b, gmat,
      w_out, b_out, ln_g, ln_b)


FF_CHUNK = 1024


def _mlp_kernel(x_ref, sh_ref, sc_ref, g_ref, w_up_ref, b_up_ref, w_down_ref, b_down_ref,
                lng_ref, lnb_ref, o_ref):
    x = x_ref[...]
    hb = (x * (1.0 + sc_ref[...]) + sh_ref[...]).astype(BF16).reshape(-1, D_MODEL)
    ff = jnp.broadcast_to(b_down_ref[...], hb.shape)
    for c0 in range(0, D_FF, FF_CHUNK):
        cs = slice(c0, c0 + FF_CHUNK)
        a = jnp.maximum(_dot(hb, w_up_ref[:, cs]) + b_up_ref[:, cs], 0.0)
        ff = ff + _dot((a * a).astype(BF16), w_down_ref[cs, :])
    y = ALPHA * x + (1.0 + g_ref[...]) * ff.reshape(x.shape)
    o_ref[...] = _layer_norm(y, lng_ref[...], lnb_ref[...])


def _mlp(x, x_spec, grid, mod_specs, mods, w_up, b_up, w_down, b_down, ln_g, ln_b, *, name):
    return pl.pallas_call(
        _mlp_kernel,
        grid=grid,
        in_specs=[x_spec] + mod_specs + [
            _const_spec(w_up.shape),
            _const_spec(b_up.shape),
            _const_spec(w_down.shape),
            _const_spec(b_down.shape),
            _const_spec(ln_g.shape),
            _const_spec(ln_b.shape),
        ],
        out_specs=x_spec,
        out_shape=jax.ShapeDtypeStruct(x.shape, F32),
        compiler_params=pltpu.CompilerParams(
            dimension_semantics=("arbitrary",), vmem_limit_bytes=VMEM_LIMIT_BYTES),
        name=name,
    )(x, *mods, w_up, b_up, w_down, b_down, ln_g, ln_b)


PROMPT_TILE = 256
SAMPLE_SEQS = 16
MLP_TILE = 512


def kernel(x_prompt, x_sample, c_prompt, c_sample, state_hgrn, state_conv, lb_logits, w_in, b_in,
           hg_norm_w, conv_w, conv_b, gn_g, gn_b, w_out, b_out, ln1_g, ln1_b, w_up, b_up, w_down,
           b_down, ln2_g, ln2_b, w_ada, b_ada):
    assert w_in.shape[0] == DEPTH
    batch, seq, _ = x_prompt.shape
    dec_batch, dec_seq, _ = x_sample.shape
    assert dec_seq == DEC_SEQ

    group = jnp.arange(CV_WIDTH, dtype=jnp.int32) // (CV_WIDTH // CV_GROUPS)
    gmat = jnp.where(group[:, None] == group[None, :], CV_GROUPS / CV_WIDTH, 0.0).astype(BF16)

    xp = x_prompt
    xs = jnp.transpose(x_sample, (1, 0, 2))
    hp, cp, hs, cs = [], [], [], []
    for l in range(DEPTH):
        row = lambda a: a[l][None, :]
        w_in_b, w_out_b = w_in[l].astype(BF16), w_out[l].astype(BF16)
        w_up_b, w_down_b = w_up[l].astype(BF16), w_down[l].astype(BF16)

        mod = _adaln(jnp.concatenate([c_prompt, c_sample], axis=0), w_ada[l], row(b_ada))
        mod_p = mod[:batch].reshape(batch, 6, D_MODEL)
        mod_s = mod[batch:]

        mixer_params = (lb_logits, w_in_b, row(b_in), row(hg_norm_w), conv_w[l], row(conv_b),
                        row(gn_g), row(gn_b), gmat, w_out_b, row(b_out), row(ln1_g), row(ln1_b))
        xp, sp, bp = _mixer_prompt(xp, mod_p, *mixer_params, tile=PROMPT_TILE)
        xs, ss, bs = _mixer_sample(xs, mod_s, state_hgrn[l], jnp.transpose(state_conv[l], (1, 0, 2)),
                                   *mixer_params, seqs=SAMPLE_SEQS)

        mlp_params = (w_up_b, row(b_up), w_down_b, row(b_down), row(ln2_g), row(ln2_b))
        tiles_per_seq = seq // MLP_TILE
        p_specs = [pl.BlockSpec((None, None, 1, D_MODEL),
                                functools.partial(lambda i, r: (i // tiles_per_seq, r, 0, 0), r=r))
                   for r in (3, 4, 5)]
        mod_p4 = mod_p.reshape(batch, 6, 1, D_MODEL)
        xp = _mlp(xp.reshape(batch * seq, D_MODEL), pl.BlockSpec((MLP_TILE, D_MODEL), lambda i: (i, 0)),
                  (batch * seq // MLP_TILE,), p_specs, (mod_p4,) * 3, *mlp_params,
                  name="mlp_prompt").reshape(batch, seq, D_MODEL)
        s_specs = [pl.BlockSpec((dec_batch, D_MODEL), functools.partial(lambda i, r: (0, r), r=r))
                   for r in (3, 4, 5)]
        xs = _mlp(xs, pl.BlockSpec((DEC_SEQ, dec_batch, D_MODEL), lambda i: (0, 0, 0)), (1,),
                  s_specs, (mod_s,) * 3, *mlp_params, name="mlp_sample")

        hp.append(sp); cp.append(bp); hs.append(ss); cs.append(jnp.transpose(bs, (1, 0, 2)))

    return (xp, jnp.transpose(xs, (1, 0, 2)), jnp.stack(hp), jnp.stack(cp), jnp.stack(hs),
            jnp.stack(cs))
```

```python
import functools

import jax
import jax.numpy as jnp
from jax import lax
from jax.experimental import pallas as pl
from jax.experimental.pallas import tpu as pltpu

F32 = jnp.float32
BF16 = jnp.bfloat16

D_MODEL = 1024
HG_WIDTH = 512
CV_WIDTH = 512
HG_HEADS = 4
HG_DK = 128
CV_GROUPS = 8
CV_KERNEL = 31
CV_BUF = CV_KERNEL - 1
D_FF = 4 * D_MODEL
N_IN = 4 * HG_WIDTH + 2 * CV_WIDTH
DEPTH = 1
ALPHA = (2.0 * DEPTH) ** 0.25
EPS = 1e-5

CHUNK = 64
SUBLANES = 8
MXU_WIDTH = 256
CONV_PAD = 32
CONV_ROWS = 128
UP_CHUNK = 512
DOWN_CHUNK = 256
FF_CHUNK = 1024
DEC_SEQ = 4

VMEM_LIMIT_BYTES = 52 * 1024 * 1024

_NT = (((1,), (1,)), ((), ()))
_TN = (((0,), (0,)), ((), ()))


def _dot(a, b):
    return jnp.dot(a, b, preferred_element_type=F32)


def _dot_nt(a, b):
    return lax.dot_general(a, b, _NT, preferred_element_type=F32)


def _dot_tn(a, b):
    return lax.dot_general(a, b, _TN, preferred_element_type=F32)


def _silu(x):
    return x * jax.nn.sigmoid(x)


def _split3(x):
    hi = x.astype(BF16).astype(F32)
    r = x - hi
    mid = r.astype(BF16).astype(F32)
    lo = (r - mid).astype(BF16).astype(F32)
    return hi, mid, lo


def _layer_norm(y, g, b):
    mu = jnp.mean(y, axis=-1, keepdims=True)
    yc = y - mu
    var = jnp.mean(yc * yc, axis=-1, keepdims=True)
    return yc * lax.rsqrt(var + EPS) * g + b


def _adaln_kernel(c_ref, w_ref, b_ref, o_ref):
    a = _silu(c_ref[...])
    a_hi = a.astype(BF16)
    a_lo = (a - a_hi.astype(F32)).astype(BF16)
    w = w_ref[...]
    w_hi = w.astype(BF16)
    w_lo = (w - w_hi.astype(F32)).astype(BF16)
    o_ref[...] = _dot(a_hi, w_hi) + _dot(a_lo, w_hi) + _dot(a_hi, w_lo) + b_ref[...]


def _adaln(c_all, w_ada, b_ada):
    rows = c_all.shape[0]
    n = w_ada.shape[1]
    tn = 1024
    return pl.pallas_call(
        _adaln_kernel,
        grid=(n // tn,),
        in_specs=[
            pl.BlockSpec((rows, D_MODEL), lambda j: (0, 0)),
            pl.BlockSpec((D_MODEL, tn), lambda j: (0, j)),
            pl.BlockSpec((1, tn), lambda j: (0, j)),
        ],
        out_specs=pl.BlockSpec((rows, tn), lambda j: (0, j)),
        out_shape=jax.ShapeDtypeStruct((rows, n), F32),
        compiler_params=pltpu.CompilerParams(
            dimension_semantics=("arbitrary",), vmem_limit_bytes=VMEM_LIMIT_BYTES),
        name="adaln",
    )(c_all, w_ada, b_ada)


def _lower_bound(lb_ref):
    lg = lb_ref[...]
    e = jnp.exp(lg - jnp.max(lg, axis=0, keepdims=True))
    p = e / jnp.sum(e, axis=0, keepdims=True)
    return jnp.sum(p[0:DEPTH, :], axis=0, keepdims=True)


def _section(hb, w_in_ref, b_in_ref, i, width):
    return _dot(hb, w_in_ref[:, i:i + width]) + b_in_ref[:, i:i + width]


def _in_projection(hb, lb, w_in_ref, b_in_ref):
    sec = functools.partial(_section, hb, w_in_ref, b_in_ref)

    q = _silu(sec(0, HG_WIDTH))
    f = lb + (1.0 - lb) * jax.nn.sigmoid(sec(HG_WIDTH, HG_WIDTH))
    v = sec(2 * HG_WIDTH, HG_WIDTH)
    gate = _silu(sec(3 * HG_WIDTH, HG_WIDTH))
    za = sec(4 * HG_WIDTH, CV_WIDTH)
    zb = sec(4 * HG_WIDTH + CV_WIDTH, CV_WIDTH)
    return q, 1.0 - f, jnp.log(f), v, gate, za * jax.nn.sigmoid(zb)


def _head_norm(o, hgw, gate):
    ms = jnp.mean(o * o, axis=-1, keepdims=True)
    return o * lax.rsqrt(ms + EPS) * hgw * gate


def _group_norm_swish(uc, gmat_ref, gn_g, gn_b):
    width = gmat_ref.shape[0]

    def group_mean(a):
        ab = a.astype(BF16)
        return jnp.concatenate([_dot(ab[:, c0:c0 + width], gmat_ref[...])
                                for c0 in range(0, a.shape[1], width)], axis=-1)

    d = uc - group_mean(uc)
    var = group_mean(d * d)
    return _silu(d * lax.rsqrt(var + EPS) * gn_g + gn_b)


def _out_projection(mix_bf16, w_out_ref, b_out_ref):
    return _dot(mix_bf16, w_out_ref[...]) + b_out_ref[...]


def _prompt_kernel(x_ref, mod_ref, lb_ref, w_in_ref, b_in_ref, hgw_ref, cw_ref, cb_ref,
                   gng_ref, gnb_ref, gmat_ref, w_out_ref, b_out_ref, ln1g_ref, ln1b_ref,
                   w_up_ref, b_up_ref, w_down_ref, b_down_ref, ln2g_ref, ln2b_ref,
                   y_ref, s_out_ref, cbuf_out_ref,
                   q_s, k_s, lf_s, v_s, g_s, ubuf, ush, uc_s, mix_s, st_s,
                   x1_s, hb2_s, res2_s, act_s, ff_s, *, tile, tiles_per_seq, n_tiles):
    i = pl.program_id(0)
    mixer_tile = jnp.minimum(i, n_tiles - 1)
    b_mix = mixer_tile // tiles_per_seq
    t = mixer_tile % tiles_per_seq
    b_mlp = jnp.maximum(i - 1, 0) // tiles_per_seq

    @pl.when(i == 0)
    def _():
        x1_s[...] = jnp.zeros_like(x1_s)

    @pl.when(t == 0)
    def _():
        st_s[...] = jnp.zeros_like(st_s)
        ubuf[0:CONV_PAD, :] = jnp.zeros((CONV_PAD, CV_WIDTH), F32)

    x1_prev = x1_s[...]
    sh2 = mod_ref[b_mlp, 3:4, :]
    sc2 = mod_ref[b_mlp, 4:5, :]
    g2 = mod_ref[b_mlp, 5:6, :]
    hb2_s[...] = (x1_prev * (1.0 + sc2) + sh2).astype(BF16)
    res2_s[...] = ALPHA * x1_prev

    def mlp_up(c0):
        cs = slice(c0, c0 + UP_CHUNK)
        a = jnp.maximum(_dot(hb2_s[...], w_up_ref[:, cs]) + b_up_ref[:, cs], 0.0)
        act_s[:, cs] = (a * a).astype(BF16)

    def mlp_down(c0):
        cs = slice(c0, c0 + DOWN_CHUNK)
        ff_s[:, cs] = _dot(act_s[...], w_down_ref[:, cs]) + b_down_ref[:, cs]

    up_chunks = list(range(0, D_FF, UP_CHUNK))
    down_chunks = list(range(0, D_MODEL, DOWN_CHUNK))

    x = x_ref[0]
    sh1 = mod_ref[b_mix, 0:1, :]
    sc1 = mod_ref[b_mix, 1:2, :]
    g1 = mod_ref[b_mix, 2:3, :]
    hb = (x * (1.0 + sc1) + sh1).astype(BF16)
    sec = functools.partial(_section, hb, w_in_ref, b_in_ref)
    lb = _lower_bound(lb_ref)

    ubuf[CONV_PAD:CONV_PAD + tile, :] = (sec(4 * HG_WIDTH, CV_WIDTH)
                                         * jax.nn.sigmoid(sec(4 * HG_WIDTH + CV_WIDTH, CV_WIDTH)))
    first_off = CONV_PAD - CV_BUF
    shifted_rows = ush.shape[1]
    for phase in range(1, SUBLANES):
        ush[phase - 1] = ubuf[phase:phase + shifted_rows, :]

    def conv_piece(r0, c0):
        cs = slice(c0, c0 + 128)
        acc = jnp.broadcast_to(cb_ref[:, cs], (CONV_ROWS, 128))
        for phase in range(SUBLANES):
            taps = [j for j in range(CV_KERNEL) if (first_off + j) % SUBLANES == phase]
            span = (first_off + taps[-1]) // SUBLANES * SUBLANES
            rows = slice(r0, r0 + span + CONV_ROWS)
            win = ubuf[rows, cs] if phase == 0 else ush[phase - 1, rows, cs]
            for j in taps:
                a0 = first_off + j - phase
                acc = acc + cw_ref[j:j + 1, cs] * win[a0:a0 + CONV_ROWS, :]
        uc_s[r0:r0 + CONV_ROWS, cs] = acc

    pieces = [(r0, c0) for r0 in range(0, tile, CONV_ROWS) for c0 in range(0, CV_WIDTH, 128)]
    per_sec = -(-len(pieces) // 4)

    def conv_pieces(n):
        for r0, c0 in pieces[n * per_sec:(n + 1) * per_sec]:
            conv_piece(r0, c0)

    mlp_up(up_chunks.pop(0))
    q_s[...] = _silu(sec(0, HG_WIDTH))
    conv_pieces(0)
    mlp_up(up_chunks.pop(0))
    f = lb + (1.0 - lb) * jax.nn.sigmoid(sec(HG_WIDTH, HG_WIDTH))
    k_s[...] = 1.0 - f
    lf_s[...] = jnp.log(f)
    conv_pieces(1)
    mlp_up(up_chunks.pop(0))
    v_s[...] = sec(2 * HG_WIDTH, HG_WIDTH)
    conv_pieces(2)
    mlp_up(up_chunks.pop(0))
    g_s[...] = _silu(sec(3 * HG_WIDTH, HG_WIDTH))
    conv_pieces(3)

    row = lax.broadcasted_iota(jnp.int32, (CHUNK, HG_WIDTH), 0)
    causal = (lax.broadcasted_iota(jnp.int32, (CHUNK, CHUNK), 0)
              >= lax.broadcasted_iota(jnp.int32, (CHUNK, CHUNK), 1))
    hgw = hgw_ref[...]
    for c in range(tile // CHUNK):
        if up_chunks:
            mlp_up(up_chunks.pop(0))
        rows = slice(c * CHUNK, (c + 1) * CHUNK)
        b = lf_s[rows, :]
        shift = 1
        while shift < CHUNK:
            b = b + jnp.where(row >= shift, pltpu.roll(b, shift, axis=0), 0.0)
            shift *= 2
        b_last = b[CHUNK - 1:CHUNK, :]
        kc = k_s[rows, :]
        qe = (q_s[rows, :] * jnp.exp(b)).astype(BF16)
        ke = (kc * jnp.exp(-b)).astype(BF16)
        kd = (kc * jnp.exp(b_last - b)).astype(BF16)
        decay = jnp.exp(b_last)
        vb = v_s[rows, :].astype(BF16)
        for h in range(HG_HEADS):
            hs = slice(h * HG_DK, (h + 1) * HG_DK)
            att = jnp.where(causal, _dot_nt(qe[:, hs], ke[:, hs]), 0.0).astype(BF16)
            st = st_s[h]
            o = _dot(att, vb[:, hs]) + _dot_nt(qe[:, hs], st.astype(BF16))
            st_s[h] = st * decay[:, hs] + _dot_tn(vb[:, hs], kd[:, hs])
            mix_s[rows, hs] = _head_norm(o, hgw, g_s[rows, hs]).astype(BF16)
    while up_chunks:
        mlp_up(up_chunks.pop(0))

    mlp_down(down_chunks.pop(0))
    mix_s[:, HG_WIDTH:] = _group_norm_swish(
        uc_s[...], gmat_ref, gng_ref[...], gnb_ref[...]).astype(BF16)
    ubuf[0:CONV_PAD, :] = ubuf[tile:tile + CONV_PAD, :]
    mlp_down(down_chunks.pop(0))
    mix = _out_projection(mix_s[...], w_out_ref, b_out_ref)
    x1_s[...] = _layer_norm(ALPHA * x + (1.0 + g1) * mix, ln1g_ref[...], ln1b_ref[...])
    while down_chunks:
        mlp_down(down_chunks.pop(0))
    y_ref[0] = _layer_norm(res2_s[...] + (1.0 + g2) * ff_s[...], ln2g_ref[...], ln2b_ref[...])

    @pl.when((t == tiles_per_seq - 1) & (i < n_tiles))
    def _():
        for h in range(HG_HEADS):
            s_out_ref[0, h] = st_s[h].T
        cbuf_out_ref[0] = ubuf[CONV_PAD - CV_BUF:CONV_PAD, :]


def _const_spec(shape):
    zeros = (0,) * len(shape)
    return pl.BlockSpec(shape, lambda *_: zeros, pipeline_mode=pl.Buffered(1))


def _prompt_layer(x, mod, lb_logits, w_in, b_in, hgw, conv_w, conv_b, gn_g, gn_b, gmat,
                  w_out, b_out, ln1_g, ln1_b, w_up, b_up, w_down, b_down, ln2_g, ln2_b, *, tile):
    batch, seq, _ = x.shape
    tiles_per_seq = seq // tile
    n_tiles = batch * tiles_per_seq
    kernel = functools.partial(_prompt_kernel, tile=tile, tiles_per_seq=tiles_per_seq,
                               n_tiles=n_tiles)

    def mixer_block(i):
        m = jnp.minimum(i, n_tiles - 1)
        return m // tiles_per_seq, m % tiles_per_seq

    def mlp_block(i):
        m = jnp.maximum(i - 1, 0)
        return m // tiles_per_seq, m % tiles_per_seq

    consts = (mod, lb_logits, w_in, b_in, hgw, conv_w, conv_b, gn_g, gn_b, gmat, w_out, b_out,
              ln1_g, ln1_b, w_up, b_up, w_down, b_down, ln2_g, ln2_b)
    return pl.pallas_call(
        kernel,
        grid=(n_tiles + 1,),
        in_specs=[pl.BlockSpec((1, tile, D_MODEL), lambda i: (*mixer_block(i), 0))]
        + [_const_spec(a.shape) for a in consts],
        out_specs=[
            pl.BlockSpec((1, tile, D_MODEL), lambda i: (*mlp_block(i), 0)),
            pl.BlockSpec((1, HG_HEADS, HG_DK, HG_DK), lambda i: (mixer_block(i)[0], 0, 0, 0)),
            pl.BlockSpec((1, CV_BUF, CV_WIDTH), lambda i: (mixer_block(i)[0], 0, 0)),
        ],
        out_shape=[
            jax.ShapeDtypeStruct((batch, seq, D_MODEL), F32),
            jax.ShapeDtypeStruct((batch, HG_HEADS, HG_DK, HG_DK), F32),
            jax.ShapeDtypeStruct((batch, CV_BUF, CV_WIDTH), F32),
        ],
        scratch_shapes=[
            pltpu.VMEM((tile, HG_WIDTH), F32),
            pltpu.VMEM((tile, HG_WIDTH), F32),
            pltpu.VMEM((tile, HG_WIDTH), F32),
            pltpu.VMEM((tile, HG_WIDTH), F32),
            pltpu.VMEM((tile, HG_WIDTH), F32),
            pltpu.VMEM((CONV_PAD + tile, CV_WIDTH), F32),
            pltpu.VMEM((SUBLANES - 1, CONV_PAD + tile - SUBLANES, CV_WIDTH), F32),
            pltpu.VMEM((tile, CV_WIDTH), F32),
            pltpu.VMEM((tile, D_MODEL), BF16),
            pltpu.VMEM((HG_HEADS, HG_DK, HG_DK), F32),
            pltpu.VMEM((tile, D_MODEL), F32),
            pltpu.VMEM((tile, D_MODEL), BF16),
            pltpu.VMEM((tile, D_MODEL), F32),
            pltpu.VMEM((tile, D_FF), BF16),
            pltpu.VMEM((tile, D_MODEL), F32),
        ],
        compiler_params=pltpu.CompilerParams(
            dimension_semantics=("arbitrary",), vmem_limit_bytes=VMEM_LIMIT_BYTES),
        name="prompt_layer",
    )(x, *consts)


def _mixer_sample_kernel(x_ref, mod_ref, s_in_ref, cbuf_in_ref, lb_ref, w_in_ref, b_in_ref, hgw_ref,
                         cw_ref, cb_ref, gng_ref, gnb_ref, gmat_ref, w_out_ref, b_out_ref,
                         lng_ref, lnb_ref,
                         x1_ref, s_out_ref, cbuf_out_ref,
                         qe_s, ke_s, kd_s, v_s, d_s, o_s, *, seqs):
    rows_n = seqs * DEC_SEQ
    half = seqs // 2
    x = x_ref[...]
    sh1 = mod_ref[:, 0:D_MODEL]
    sc1 = mod_ref[:, D_MODEL:2 * D_MODEL]
    g1 = mod_ref[:, 2 * D_MODEL:3 * D_MODEL]
    hb = (x * (1.0 + sc1) + sh1).astype(BF16).reshape(rows_n, D_MODEL)
    q, k, lf, v, gate, u = _in_projection(hb, _lower_bound(lb_ref), w_in_ref, b_in_ref)

    slab = lambda a, t: a[t * seqs:(t + 1) * seqs, :]
    b = [slab(lf, 0)]
    for t in range(1, DEC_SEQ):
        b.append(b[-1] + slab(lf, t))
    b_last = b[-1]
    parts = _split3(jnp.exp(b_last)) + (jnp.zeros_like(b_last),)

    def put(ref, t, val):
        for h in range(HG_HEADS):
            ref[h, t * seqs:(t + 1) * seqs, :] = val[:, h * HG_DK:(h + 1) * HG_DK]

    for t in range(DEC_SEQ):
        put(qe_s, t, slab(q, t) * jnp.exp(b[t]))
        put(ke_s, t, slab(k, t) * jnp.exp(-b[t]))
        put(kd_s, t, slab(k, t) * jnp.exp(b_last - b[t]))
        put(d_s, t, parts[t])
        put(v_s, t, slab(v, t))

    r8 = lax.broadcasted_iota(jnp.int32, (2 * DEC_SEQ, HG_DK), 0)
    even = r8 % 2 == 0
    ones_a = jnp.where(even, 1.0, 0.0).astype(BF16)
    ones_b = jnp.where(even, 0.0, 1.0).astype(BF16)
    i8 = lax.broadcasted_iota(jnp.int32, (2 * DEC_SEQ, 2 * DEC_SEQ), 0)
    j8 = lax.broadcasted_iota(jnp.int32, (2 * DEC_SEQ, 2 * DEC_SEQ), 1)
    causal = (i8 // 2 >= j8 // 2) & (i8 % 2 == j8 % 2)
    for s in range(half):
        rows = pl.ds(s, 2 * DEC_SEQ, stride=half)
        for h in range(HG_HEADS):
            qe8 = qe_s[h, rows, :].astype(BF16)
            ke8 = ke_s[h, rows, :].astype(BF16)
            v8 = v_s[h, rows, :].astype(BF16)
            kd8 = kd_s[h, rows, :]
            kd_a = jnp.where(even, kd8, 0.0).astype(BF16)
            kd_b = jnp.where(even, 0.0, kd8).astype(BF16)
            d8 = d_s[h, rows, :].astype(BF16)
            att = jnp.where(causal, _dot_nt(qe8, ke8), 0.0).astype(BF16)
            s_a = s_in_ref[s, h]
            s_b = s_in_ref[s + half, h]
            inter = jnp.where(even, _dot(qe8, s_a.astype(BF16)), _dot(qe8, s_b.astype(BF16)))
            o_s[h, rows, :] = _dot(att, v8) + inter
            s_out_ref[s, h] = _dot_tn(d8, ones_a) * s_a + _dot_tn(kd_a, v8)
            s_out_ref[s + half, h] = _dot_tn(d8, ones_b) * s_b + _dot_tn(kd_b, v8)
    hgw = hgw_ref[...]
    o_a = jnp.concatenate(
        [_head_norm(o_s[h], hgw, gate[:, h * HG_DK:(h + 1) * HG_DK]) for h in range(HG_HEADS)],
        axis=-1)

    full = lambda i: cbuf_in_ref[i] if i < CV_BUF else slab(u, i - CV_BUF)
    acc = [jnp.broadcast_to(cb_ref[...], (seqs, CV_WIDTH)) for _ in range(DEC_SEQ)]
    for i in range(CV_BUF + DEC_SEQ):
        f_i = full(i)
        for t in range(DEC_SEQ):
            if 0 <= i - t < CV_KERNEL:
                acc[t] = acc[t] + cw_ref[i - t:i - t + 1, :] * f_i
    for i in range(CV_BUF):
        cbuf_out_ref[i] = full(i + DEC_SEQ)
    o_b = _group_norm_swish(jnp.concatenate(acc, axis=0), gmat_ref, gng_ref[...], gnb_ref[...])

    mix = _out_projection(jnp.concatenate([o_a, o_b], axis=-1).astype(BF16), w_out_ref, b_out_ref)
    y = ALPHA * x + (1.0 + g1) * mix.reshape(DEC_SEQ, seqs, D_MODEL)
    x1_ref[...] = _layer_norm(y, lng_ref[...], lnb_ref[...])


def _mixer_sample(x, mod, s_in, cbuf_in, lb_logits, w_in, b_in, hgw, conv_w, conv_b, gn_g, gn_b,
                  gmat, w_out, b_out, ln_g, ln_b, *, seqs):
    n_seq = s_in.shape[0]
    rows = seqs * DEC_SEQ
    kernel = functools.partial(_mixer_sample_kernel, seqs=seqs)
    return pl.pallas_call(
        kernel,
        grid=(n_seq // seqs,),
        in_specs=[
            pl.BlockSpec((DEC_SEQ, seqs, D_MODEL), lambda i: (0, i, 0)),
            pl.BlockSpec((seqs, 3 * D_MODEL), lambda i: (i, 0)),
            pl.BlockSpec((seqs, HG_HEADS, HG_DK, HG_DK), lambda i: (i, 0, 0, 0)),
            pl.BlockSpec((CV_BUF, seqs, CV_WIDTH), lambda i: (0, i, 0)),
            _const_spec(lb_logits.shape),
            _const_spec(w_in.shape),
            _const_spec(b_in.shape),
            _const_spec(hgw.shape),
            _const_spec(conv_w.shape),
            _const_spec(conv_b.shape),
            _const_spec(gn_g.shape),
            _const_spec(gn_b.shape),
            _const_spec(gmat.shape),
            _const_spec(w_out.shape),
            _const_spec(b_out.shape),
            _const_spec(ln_g.shape),
            _const_spec(ln_b.shape),
        ],
        out_specs=[
            pl.BlockSpec((DEC_SEQ, seqs, D_MODEL), lambda i: (0, i, 0)),
            pl.BlockSpec((seqs, HG_HEADS, HG_DK, HG_DK), lambda i: (i, 0, 0, 0)),
            pl.BlockSpec((CV_BUF, seqs, CV_WIDTH), lambda i: (0, i, 0)),
        ],
        out_shape=[
            jax.ShapeDtypeStruct((DEC_SEQ, n_seq, D_MODEL), F32),
            jax.ShapeDtypeStruct(s_in.shape, F32),
            jax.ShapeDtypeStruct(cbuf_in.shape, F32),
        ],
        scratch_shapes=[pltpu.VMEM((HG_HEADS, rows, HG_DK), F32)] * 6,
        compiler_params=pltpu.CompilerParams(
            dimension_semantics=("arbitrary",), vmem_limit_bytes=VMEM_LIMIT_BYTES),
        name="mixer_sample",
    )(x, mod, s_in, cbuf_in, lb_logits, w_in, b_in, hgw, conv_w, conv_b, gn_g, gn_b, gmat,
      w_out, b_out, ln_g, ln_b)


def _mlp_kernel(x_ref, sh_ref, sc_ref, g_ref, w_up_ref, b_up_ref, w_down_ref, b_down_ref,
                lng_ref, lnb_ref, o_ref):
    x = x_ref[...]
    hb = (x * (1.0 + sc_ref[...]) + sh_ref[...]).astype(BF16).reshape(-1, D_MODEL)
    ff = jnp.broadcast_to(b_down_ref[...], hb.shape)
    for c0 in range(0, D_FF, FF_CHUNK):
        cs = slice(c0, c0 + FF_CHUNK)
        a = jnp.maximum(_dot(hb, w_up_ref[:, cs]) + b_up_ref[:, cs], 0.0)
        ff = ff + _dot((a * a).astype(BF16), w_down_ref[cs, :])
    y = ALPHA * x + (1.0 + g_ref[...]) * ff.reshape(x.shape)
    o_ref[...] = _layer_norm(y, lng_ref[...], lnb_ref[...])


def _mlp(x, x_spec, grid, mod_specs, mods, w_up, b_up, w_down, b_down, ln_g, ln_b, *, name):
    return pl.pallas_call(
        _mlp_kernel,
        grid=grid,
        in_specs=[x_spec] + mod_specs + [
            _const_spec(w_up.shape),
            _const_spec(b_up.shape),
            _const_spec(w_down.shape),
            _const_spec(b_down.shape),
            _const_spec(ln_g.shape),
            _const_spec(ln_b.shape),
        ],
        out_specs=x_spec,
        out_shape=jax.ShapeDtypeStruct(x.shape, F32),
        compiler_params=pltpu.CompilerParams(
            dimension_semantics=("arbitrary",), vmem_limit_bytes=VMEM_LIMIT_BYTES),
        name=name,
    )(x, *mods, w_up, b_up, w_down, b_down, ln_g, ln_b)


PROMPT_TILE = 256
SAMPLE_SEQS = 16


def kernel(x_prompt, x_sample, c_prompt, c_sample, state_hgrn, state_conv, lb_logits, w_in, b_in,
           hg_norm_w, conv_w, conv_b, gn_g, gn_b, w_out, b_out, ln1_g, ln1_b, w_up, b_up, w_down,
           b_down, ln2_g, ln2_b, w_ada, b_ada):
    assert w_in.shape[0] == DEPTH
    batch, seq, _ = x_prompt.shape
    dec_batch, dec_seq, _ = x_sample.shape
    assert dec_seq == DEC_SEQ

    group = jnp.arange(MXU_WIDTH, dtype=jnp.int32) // (CV_WIDTH // CV_GROUPS)
    gmat = jnp.where(group[:, None] == group[None, :], CV_GROUPS / CV_WIDTH, 0.0).astype(BF16)

    xp = x_prompt
    xs = jnp.transpose(x_sample, (1, 0, 2))
    hp, cp, hs, cs = [], [], [], []
    for l in range(DEPTH):
        row = lambda a: a[l][None, :]
        w_in_b, w_out_b = w_in[l].astype(BF16), w_out[l].astype(BF16)
        w_up_b, w_down_b = w_up[l].astype(BF16), w_down[l].astype(BF16)

        mod = _adaln(jnp.concatenate([c_prompt, c_sample], axis=0), w_ada[l], row(b_ada))
        mod_p = mod[:batch].reshape(batch, 6, D_MODEL)
        mod_s = mod[batch:]

        mixer_params = (lb_logits, w_in_b, row(b_in), row(hg_norm_w), conv_w[l], row(conv_b),
                        row(gn_g), row(gn_b), gmat, w_out_b, row(b_out), row(ln1_g), row(ln1_b))
        mlp_params = (w_up_b, row(b_up), w_down_b, row(b_down), row(ln2_g), row(ln2_b))
        xp, sp, bp = _prompt_layer(xp, mod_p, *mixer_params, *mlp_params, tile=PROMPT_TILE)
        xs, ss, bs = _mixer_sample(xs, mod_s, state_hgrn[l], jnp.transpose(state_conv[l], (1, 0, 2)),
                                   *mixer_params, seqs=SAMPLE_SEQS)
        s_specs = [pl.BlockSpec((dec_batch, D_MODEL), functools.partial(lambda i, r: (0, r), r=r))
                   for r in (3, 4, 5)]
        xs = _mlp(xs, pl.BlockSpec((DEC_SEQ, dec_batch, D_MODEL), lambda i: (0, 0, 0)), (1,),
                  s_specs, (mod_s,) * 3, *mlp_params, name="mlp_sample")

        hp.append(sp); cp.append(bp); hs.append(ss); cs.append(jnp.transpose(bs, (1, 0, 2)))

    return (xp, jnp.transpose(xs, (1, 0, 2)), jnp.stack(hp), jnp.stack(cp), jnp.stack(hs),
            jnp.stack(cs))
```

```python
import functools

import jax
import jax.numpy as jnp
from jax import lax
from jax.experimental import pallas as pl
from jax.experimental.pallas import tpu as pltpu

F32 = jnp.float32
BF16 = jnp.bfloat16

D_MODEL = 1024
HG_WIDTH = 512
CV_WIDTH = 512
HG_HEADS = 4
HG_DK = 128
CV_GROUPS = 8
CV_KERNEL = 31
CV_BUF = CV_KERNEL - 1
D_FF = 4 * D_MODEL
N_IN = 4 * HG_WIDTH + 2 * CV_WIDTH
DEPTH = 1
ALPHA = (2.0 * DEPTH) ** 0.25
EPS = 1e-5

CHUNK = 64
SUBLANES = 8
MXU_WIDTH = 256
CONV_PAD = 32
CONV_ROWS = 32
UP_CHUNK = 512
DOWN_CHUNK = 512
FF_CHUNK = 1024
DEC_SEQ = 4

VMEM_LIMIT_BYTES = 52 * 1024 * 1024

_NT = (((1,), (1,)), ((), ()))
_TN = (((0,), (0,)), ((), ()))


def _dot(a, b):
    return jnp.dot(a, b, preferred_element_type=F32)


def _dot_nt(a, b):
    return lax.dot_general(a, b, _NT, preferred_element_type=F32)


def _dot_tn(a, b):
    return lax.dot_general(a, b, _TN, preferred_element_type=F32)


def _silu(x):
    return x * jax.nn.sigmoid(x)


def _split3(x):
    hi = x.astype(BF16).astype(F32)
    r = x - hi
    mid = r.astype(BF16).astype(F32)
    lo = (r - mid).astype(BF16).astype(F32)
    return hi, mid, lo


def _layer_norm(y, g, b):
    mu = jnp.mean(y, axis=-1, keepdims=True)
    yc = y - mu
    var = jnp.mean(yc * yc, axis=-1, keepdims=True)
    return yc * lax.rsqrt(var + EPS) * g + b


def _adaln_kernel(c_ref, w_ref, b_ref, o_ref):
    a = _silu(c_ref[...])
    a_hi = a.astype(BF16)
    a_lo = (a - a_hi.astype(F32)).astype(BF16)
    w = w_ref[...]
    w_hi = w.astype(BF16)
    w_lo = (w - w_hi.astype(F32)).astype(BF16)
    o_ref[...] = _dot(a_hi, w_hi) + _dot(a_lo, w_hi) + _dot(a_hi, w_lo) + b_ref[...]


def _adaln(c_all, w_ada, b_ada):
    rows = c_all.shape[0]
    n = w_ada.shape[1]
    tn = 1024
    return pl.pallas_call(
        _adaln_kernel,
        grid=(n // tn,),
        in_specs=[
            pl.BlockSpec((rows, D_MODEL), lambda j: (0, 0)),
            pl.BlockSpec((D_MODEL, tn), lambda j: (0, j)),
            pl.BlockSpec((1, tn), lambda j: (0, j)),
        ],
        out_specs=pl.BlockSpec((rows, tn), lambda j: (0, j)),
        out_shape=jax.ShapeDtypeStruct((rows, n), F32),
        compiler_params=pltpu.CompilerParams(
            dimension_semantics=("arbitrary",), vmem_limit_bytes=VMEM_LIMIT_BYTES),
        name="adaln",
    )(c_all, w_ada, b_ada)


def _lower_bound(lb_ref):
    lg = lb_ref[...]
    e = jnp.exp(lg - jnp.max(lg, axis=0, keepdims=True))
    p = e / jnp.sum(e, axis=0, keepdims=True)
    return jnp.sum(p[0:DEPTH, :], axis=0, keepdims=True)


def _section(hb, w_in_ref, b_in_ref, i, width):
    return _dot(hb, w_in_ref[:, i:i + width]) + b_in_ref[:, i:i + width]


def _in_projection(hb, lb, w_in_ref, b_in_ref):
    sec = functools.partial(_section, hb, w_in_ref, b_in_ref)

    q = _silu(sec(0, HG_WIDTH))
    f = lb + (1.0 - lb) * jax.nn.sigmoid(sec(HG_WIDTH, HG_WIDTH))
    v = sec(2 * HG_WIDTH, HG_WIDTH)
    gate = _silu(sec(3 * HG_WIDTH, HG_WIDTH))
    za = sec(4 * HG_WIDTH, CV_WIDTH)
    zb = sec(4 * HG_WIDTH + CV_WIDTH, CV_WIDTH)
    return q, 1.0 - f, jnp.log(f), v, gate, za * jax.nn.sigmoid(zb)


def _head_norm(o, hgw, gate):
    ms = jnp.mean(o * o, axis=-1, keepdims=True)
    return o * lax.rsqrt(ms + EPS) * hgw * gate


def _group_norm_swish(uc, gmat_ref, gn_g, gn_b):
    width = gmat_ref.shape[0]

    def group_mean(a):
        ab = a.astype(BF16)
        return jnp.concatenate([_dot(ab[:, c0:c0 + width], gmat_ref[...])
                                for c0 in range(0, a.shape[1], width)], axis=-1)

    d = uc - group_mean(uc)
    var = group_mean(d * d)
    return _silu(d * lax.rsqrt(var + EPS) * gn_g + gn_b)


def _out_projection(mix_bf16, w_out_ref, b_out_ref):
    return _dot(mix_bf16, w_out_ref[...]) + b_out_ref[...]


def _merge_by_progress(*lists):
    keyed = []
    for n, pieces in enumerate(lists):
        total = sum(cost for cost, _ in pieces)
        done = 0.0
        for k, (cost, item) in enumerate(pieces):
            keyed.append((done / total, n, k, item))
            done += cost
    return [item for _, _, _, item in sorted(keyed, key=lambda e: e[:3])]


def _prompt_kernel(x_ref, mod_ref, lb_ref, w_in_ref, b_in_ref, hgw_ref, cw_ref, cb_ref,
                   gng_ref, gnb_ref, gmat_ref, w_out_ref, b_out_ref, ln1g_ref, ln1b_ref,
                   w_up_ref, b_up_ref, w_down_ref, b_down_ref, ln2g_ref, ln2b_ref,
                   y_ref, s_out_ref, cbuf_out_ref,
                   q_s, k_s, lf_s, v_s, g_s, ubuf, ush, uc_s, mix_s, st_s,
                   qe_s, ke_s, kd_s, dec_s, oi_s, kv_s,
                   x1_s, hb2_s, res2_s, act_s, ff_s, *, tile, tiles_per_seq, n_tiles):
    i = pl.program_id(0)
    mixer_tile = jnp.minimum(i, n_tiles - 1)
    b_mix = mixer_tile // tiles_per_seq
    t = mixer_tile % tiles_per_seq
    b_mlp = jnp.maximum(i - 1, 0) // tiles_per_seq

    @pl.when(i == 0)
    def _():
        x1_s[...] = jnp.zeros_like(x1_s)

    @pl.when(t == 0)
    def _():
        st_s[...] = jnp.zeros_like(st_s)
        ubuf[0:CONV_PAD, :] = jnp.zeros((CONV_PAD, CV_WIDTH), F32)

    x1_prev = _layer_norm(x1_s[...], ln1g_ref[...], ln1b_ref[...])
    sh2 = mod_ref[b_mlp, 3:4, :]
    sc2 = mod_ref[b_mlp, 4:5, :]
    g2 = mod_ref[b_mlp, 5:6, :]
    hb2_s[...] = (x1_prev * (1.0 + sc2) + sh2).astype(BF16)
    res2_s[...] = ALPHA * x1_prev

    def mlp_up(c0):
        cs = slice(c0, c0 + UP_CHUNK)
        a = jnp.maximum(_dot(hb2_s[...], w_up_ref[:, cs]) + b_up_ref[:, cs], 0.0)
        act_s[:, cs] = (a * a).astype(BF16)

    def mlp_down(c0):
        cs = slice(c0, c0 + DOWN_CHUNK)
        ff_s[:, cs] = _dot(act_s[...], w_down_ref[:, cs]) + b_down_ref[:, cs]

    x = x_ref[0]
    sh1 = mod_ref[b_mix, 0:1, :]
    sc1 = mod_ref[b_mix, 1:2, :]
    g1 = mod_ref[b_mix, 2:3, :]
    hb = (x * (1.0 + sc1) + sh1).astype(BF16)
    sec = functools.partial(_section, hb, w_in_ref, b_in_ref)
    lb = _lower_bound(lb_ref)

    ubuf[CONV_PAD:CONV_PAD + tile, :] = (sec(4 * HG_WIDTH, CV_WIDTH)
                                         * jax.nn.sigmoid(sec(4 * HG_WIDTH + CV_WIDTH, CV_WIDTH)))
    q_s[...] = _silu(sec(0, HG_WIDTH))
    f = lb + (1.0 - lb) * jax.nn.sigmoid(sec(HG_WIDTH, HG_WIDTH))
    k_s[...] = 1.0 - f
    lf_s[...] = jnp.log(f)
    v_s[...] = sec(2 * HG_WIDTH, HG_WIDTH).astype(BF16)
    g_s[...] = _silu(sec(3 * HG_WIDTH, HG_WIDTH))

    first_off = CONV_PAD - CV_BUF
    shifted_rows = ush.shape[1]

    def shifted_copies():
        for phase in range(1, SUBLANES):
            ush[phase - 1] = ubuf[phase:phase + shifted_rows, :]

    def conv_piece(r0, c0):
        cs = slice(c0, c0 + 128)
        parts = [jnp.broadcast_to(cb_ref[:, cs], (CONV_ROWS, 128))]
        for phase in range(SUBLANES):
            taps = [j for j in range(CV_KERNEL) if (first_off + j) % SUBLANES == phase]
            span = (first_off + taps[-1]) // SUBLANES * SUBLANES
            rows = slice(r0, r0 + span + CONV_ROWS)
            win = ubuf[rows, cs] if phase == 0 else ush[phase - 1, rows, cs]
            acc = None
            for j in taps:
                a0 = first_off + j - phase
                term = cw_ref[j:j + 1, cs] * win[a0:a0 + CONV_ROWS, :]
                acc = term if acc is None else acc + term
            parts.append(acc)
        while len(parts) > 1:
            parts = [a + b for a, b in zip(parts[0::2], parts[1::2])] + parts[len(parts) & ~1:]
        uc_s[r0:r0 + CONV_ROWS, cs] = parts[0]

    row = lax.broadcasted_iota(jnp.int32, (CHUNK, HG_WIDTH), 0)
    causal = (lax.broadcasted_iota(jnp.int32, (CHUNK, CHUNK), 0)
              >= lax.broadcasted_iota(jnp.int32, (CHUNK, CHUNK), 1))
    hgw = hgw_ref[...]

    heads = [slice(h * HG_DK, (h + 1) * HG_DK) for h in range(HG_HEADS)]

    def hgrn_prep(c):
        rows = slice(c * CHUNK, (c + 1) * CHUNK)
        b = lf_s[rows, :]
        shift = 1
        while shift < CHUNK:
            b = b + jnp.where(row >= shift, pltpu.roll(b, shift, axis=0), 0.0)
            shift *= 2
        b_last = b[CHUNK - 1:CHUNK, :]
        kc = k_s[rows, :]
        qe_s[rows, :] = (q_s[rows, :] * jnp.exp(b)).astype(BF16)
        ke_s[rows, :] = (kc * jnp.exp(-b)).astype(BF16)
        kd_s[rows, :] = (kc * jnp.exp(b_last - b)).astype(BF16)
        dec_s[SUBLANES * c:SUBLANES * c + 1, :] = jnp.exp(b_last)

    def hgrn_intra(c):
        rows = slice(c * CHUNK, (c + 1) * CHUNK)
        att = [jnp.where(causal, _dot_nt(qe_s[rows, hs], ke_s[rows, hs]), 0.0).astype(BF16)
               for hs in heads]
        for h, hs in enumerate(heads):
            oi_s[rows, hs] = _dot(att[h], v_s[rows, hs])
            kv_s[c, h] = _dot_tn(v_s[rows, hs], kd_s[rows, hs])

    def hgrn_inter(c):
        rows = slice(c * CHUNK, (c + 1) * CHUNK)
        decay = dec_s[SUBLANES * c:SUBLANES * c + 1, :]
        for h, hs in enumerate(heads):
            st = st_s[h]
            o = oi_s[rows, hs] + _dot_nt(qe_s[rows, hs], st.astype(BF16))
            st_s[h] = st * decay[:, hs] + kv_s[c, h]
            mix_s[rows, hs] = _head_norm(o, hgw, g_s[rows, hs]).astype(BF16)

    vpu_pieces = [(2.0, shifted_copies)]
    for cost, stage in ((1.6, hgrn_prep), (0.8, hgrn_intra), (1.0, hgrn_inter)):
        vpu_pieces += [(cost, functools.partial(stage, c)) for c in range(tile // CHUNK)]
    vpu_pieces += [(CONV_ROWS / 128, functools.partial(conv_piece, r0, c0))
                   for r0 in range(0, tile, CONV_ROWS) for c0 in range(0, CV_WIDTH, 128)]
    mxu_pieces = [(UP_CHUNK * D_MODEL, functools.partial(mlp_up, c0))
                  for c0 in range(0, D_FF, UP_CHUNK)]
    mxu_pieces += [(DOWN_CHUNK * D_FF, functools.partial(mlp_down, c0))
                   for c0 in range(0, D_MODEL, DOWN_CHUNK)]
    for piece in _merge_by_progress(vpu_pieces, mxu_pieces):
        piece()

    mix_s[:, HG_WIDTH:] = _group_norm_swish(
        uc_s[...], gmat_ref, gng_ref[...], gnb_ref[...]).astype(BF16)
    ubuf[0:CONV_PAD, :] = ubuf[tile:tile + CONV_PAD, :]
    y_ref[0] = _layer_norm(res2_s[...] + (1.0 + g2) * ff_s[...], ln2g_ref[...], ln2b_ref[...])
    mix = _out_projection(mix_s[...], w_out_ref, b_out_ref)
    x1_s[...] = ALPHA * x + (1.0 + g1) * mix

    @pl.when((t == tiles_per_seq - 1) & (i < n_tiles))
    def _():
        for h in range(HG_HEADS):
            s_out_ref[0, h] = st_s[h].T
        cbuf_out_ref[0] = ubuf[CONV_PAD - CV_BUF:CONV_PAD, :]


def _const_spec(shape):
    zeros = (0,) * len(shape)
    return pl.BlockSpec(shape, lambda *_: zeros, pipeline_mode=pl.Buffered(1))


def _prompt_layer(x, mod, lb_logits, w_in, b_in, hgw, conv_w, conv_b, gn_g, gn_b, gmat,
                  w_out, b_out, ln1_g, ln1_b, w_up, b_up, w_down, b_down, ln2_g, ln2_b, *, tile):
    batch, seq, _ = x.shape
    tiles_per_seq = seq // tile
    n_tiles = batch * tiles_per_seq
    kernel = functools.partial(_prompt_kernel, tile=tile, tiles_per_seq=tiles_per_seq,
                               n_tiles=n_tiles)

    def mixer_block(i):
        m = jnp.minimum(i, n_tiles - 1)
        return m // tiles_per_seq, m % tiles_per_seq

    def mlp_block(i):
        m = jnp.maximum(i - 1, 0)
        return m // tiles_per_seq, m % tiles_per_seq

    consts = (mod, lb_logits, w_in, b_in, hgw, conv_w, conv_b, gn_g, gn_b, gmat, w_out, b_out,
              ln1_g, ln1_b, w_up, b_up, w_down, b_down, ln2_g, ln2_b)
    return pl.pallas_call(
        kernel,
        grid=(n_tiles + 1,),
        in_specs=[pl.BlockSpec((1, tile, D_MODEL), lambda i: (*mixer_block(i), 0))]
        + [_const_spec(a.shape) for a in consts],
        out_specs=[
            pl.BlockSpec((1, tile, D_MODEL), lambda i: (*mlp_block(i), 0)),
            pl.BlockSpec((1, HG_HEADS, HG_DK, HG_DK), lambda i: (mixer_block(i)[0], 0, 0, 0)),
            pl.BlockSpec((1, CV_BUF, CV_WIDTH), lambda i: (mixer_block(i)[0], 0, 0)),
        ],
        out_shape=[
            jax.ShapeDtypeStruct((batch, seq, D_MODEL), F32),
            jax.ShapeDtypeStruct((batch, HG_HEADS, HG_DK, HG_DK), F32),
            jax.ShapeDtypeStruct((batch, CV_BUF, CV_WIDTH), F32),
        ],
        scratch_shapes=[
            pltpu.VMEM((tile, HG_WIDTH), F32),
            pltpu.VMEM((tile, HG_WIDTH), F32),
            pltpu.VMEM((tile, HG_WIDTH), F32),
            pltpu.VMEM((tile, HG_WIDTH), BF16),
            pltpu.VMEM((tile, HG_WIDTH), F32),
            pltpu.VMEM((CONV_PAD + tile, CV_WIDTH), F32),
            pltpu.VMEM((SUBLANES - 1, CONV_PAD + tile - SUBLANES, CV_WIDTH), F32),
            pltpu.VMEM((tile, CV_WIDTH), F32),
            pltpu.VMEM((tile, D_MODEL), BF16),
            pltpu.VMEM((HG_HEADS, HG_DK, HG_DK), F32),
            pltpu.VMEM((tile, HG_WIDTH), BF16),
            pltpu.VMEM((tile, HG_WIDTH), BF16),
            pltpu.VMEM((tile, HG_WIDTH), BF16),
            pltpu.VMEM((tile // CHUNK * SUBLANES, HG_WIDTH), F32),
            pltpu.VMEM((tile, HG_WIDTH), F32),
            pltpu.VMEM((tile // CHUNK, HG_HEADS, HG_DK, HG_DK), F32),
            pltpu.VMEM((tile, D_MODEL), F32),
            pltpu.VMEM((tile, D_MODEL), BF16),
            pltpu.VMEM((tile, D_MODEL), F32),
            pltpu.VMEM((tile, D_FF), BF16),
            pltpu.VMEM((tile, D_MODEL), F32),
        ],
        compiler_params=pltpu.CompilerParams(
            dimension_semantics=("arbitrary",), vmem_limit_bytes=VMEM_LIMIT_BYTES),
        name="prompt_layer",
    )(x, *consts)


def _mixer_sample_kernel(x_ref, mod_ref, s_in_ref, cbuf_in_ref, lb_ref, w_in_ref, b_in_ref, hgw_ref,
                         cw_ref, cb_ref, gng_ref, gnb_ref, gmat_ref, w_out_ref, b_out_ref,
                         lng_ref, lnb_ref,
                         x1_ref, s_out_ref, cbuf_out_ref,
                         qe_s, ke_s, kd_s, v_s, d_s, o_s, *, seqs):
    rows_n = seqs * DEC_SEQ
    half = seqs // 2
    x = x_ref[...]
    sh1 = mod_ref[:, 0:D_MODEL]
    sc1 = mod_ref[:, D_MODEL:2 * D_MODEL]
    g1 = mod_ref[:, 2 * D_MODEL:3 * D_MODEL]
    hb = (x * (1.0 + sc1) + sh1).astype(BF16).reshape(rows_n, D_MODEL)
    q, k, lf, v, gate, u = _in_projection(hb, _lower_bound(lb_ref), w_in_ref, b_in_ref)

    slab = lambda a, t: a[t * seqs:(t + 1) * seqs, :]
    b = [slab(lf, 0)]
    for t in range(1, DEC_SEQ):
        b.append(b[-1] + slab(lf, t))
    b_last = b[-1]
    parts = _split3(jnp.exp(b_last)) + (jnp.zeros_like(b_last),)

    def put(ref, t, val):
        for h in range(HG_HEADS):
            ref[h, t * seqs:(t + 1) * seqs, :] = val[:, h * HG_DK:(h + 1) * HG_DK]

    for t in range(DEC_SEQ):
        put(qe_s, t, slab(q, t) * jnp.exp(b[t]))
        put(ke_s, t, slab(k, t) * jnp.exp(-b[t]))
        put(kd_s, t, slab(k, t) * jnp.exp(b_last - b[t]))
        put(d_s, t, parts[t])
        put(v_s, t, slab(v, t))

    r8 = lax.broadcasted_iota(jnp.int32, (2 * DEC_SEQ, HG_DK), 0)
    even = r8 % 2 == 0
    ones_ab = jnp.concatenate([jnp.where(even, 1.0, 0.0), jnp.where(even, 0.0, 1.0)],
                              axis=1).astype(BF16)
    i8 = lax.broadcasted_iota(jnp.int32, (2 * DEC_SEQ, 2 * DEC_SEQ), 0)
    j8 = lax.broadcasted_iota(jnp.int32, (2 * DEC_SEQ, 2 * DEC_SEQ), 1)
    causal = (i8 // 2 >= j8 // 2) & (i8 % 2 == j8 % 2)
    for s in range(half):
        rows = pl.ds(s, 2 * DEC_SEQ, stride=half)
        for h in range(HG_HEADS):
            qe8 = qe_s[h, rows, :].astype(BF16)
            ke8 = ke_s[h, rows, :].astype(BF16)
            kd8 = kd_s[h, rows, :].astype(BF16)
            d8 = d_s[h, rows, :].astype(BF16)
            v8 = v_s[h, rows, :]
            v_ab = jnp.concatenate([jnp.where(even, v8, 0.0), jnp.where(even, 0.0, v8)],
                                   axis=1).astype(BF16)
            s_ab = jnp.concatenate([s_in_ref[s, h], s_in_ref[s + half, h]], axis=1)
            att = jnp.where(causal, _dot_nt(qe8, ke8), 0.0).astype(BF16)
            inter = _dot(qe8, s_ab.astype(BF16))
            o_s[h, rows, :] = (_dot(att, v8.astype(BF16))
                               + jnp.where(even, inter[:, :HG_DK], inter[:, HG_DK:]))
            new = _dot_tn(d8, ones_ab) * s_ab + _dot_tn(kd8, v_ab)
            s_out_ref[s, h] = new[:, :HG_DK]
            s_out_ref[s + half, h] = new[:, HG_DK:]
    hgw = hgw_ref[...]
    o_a = jnp.concatenate(
        [_head_norm(o_s[h], hgw, gate[:, h * HG_DK:(h + 1) * HG_DK]) for h in range(HG_HEADS)],
        axis=-1)

    full = lambda i: cbuf_in_ref[i] if i < CV_BUF else slab(u, i - CV_BUF)
    acc = [jnp.broadcast_to(cb_ref[...], (seqs, CV_WIDTH)) for _ in range(DEC_SEQ)]
    for i in range(CV_BUF + DEC_SEQ):
        f_i = full(i)
        for t in range(DEC_SEQ):
            if 0 <= i - t < CV_KERNEL:
                acc[t] = acc[t] + cw_ref[i - t:i - t + 1, :] * f_i
    for i in range(CV_BUF):
        cbuf_out_ref[i] = full(i + DEC_SEQ)
    o_b = _group_norm_swish(jnp.concatenate(acc, axis=0), gmat_ref, gng_ref[...], gnb_ref[...])

    mix = _out_projection(jnp.concatenate([o_a, o_b], axis=-1).astype(BF16), w_out_ref, b_out_ref)
    y = ALPHA * x + (1.0 + g1) * mix.reshape(DEC_SEQ, seqs, D_MODEL)
    x1_ref[...] = _layer_norm(y, lng_ref[...], lnb_ref[...])


def _mixer_sample(x, mod, s_in, cbuf_in, lb_logits, w_in, b_in, hgw, conv_w, conv_b, gn_g, gn_b,
                  gmat, w_out, b_out, ln_g, ln_b, *, seqs):
    n_seq = s_in.shape[0]
    rows = seqs * DEC_SEQ
    kernel = functools.partial(_mixer_sample_kernel, seqs=seqs)
    return pl.pallas_call(
        kernel,
        grid=(n_seq // seqs,),
        in_specs=[
            pl.BlockSpec((DEC_SEQ, seqs, D_MODEL), lambda i: (0, i, 0)),
            pl.BlockSpec((seqs, 3 * D_MODEL), lambda i: (i, 0)),
            pl.BlockSpec((seqs, HG_HEADS, HG_DK, HG_DK), lambda i: (i, 0, 0, 0)),
            pl.BlockSpec((CV_BUF, seqs, CV_WIDTH), lambda i: (0, i, 0)),
            _const_spec(lb_logits.shape),
            _const_spec(w_in.shape),
            _const_spec(b_in.shape),
            _const_spec(hgw.shape),
            _const_spec(conv_w.shape),
            _const_spec(conv_b.shape),
            _const_spec(gn_g.shape),
            _const_spec(gn_b.shape),
            _const_spec(gmat.shape),
            _const_spec(w_out.shape),
            _const_spec(b_out.shape),
            _const_spec(ln_g.shape),
            _const_spec(ln_b.shape),
        ],
        out_specs=[
            pl.BlockSpec((DEC_SEQ, seqs, D_MODEL), lambda i: (0, i, 0)),
            pl.BlockSpec((seqs, HG_HEADS, HG_DK, HG_DK), lambda i: (i, 0, 0, 0)),
            pl.BlockSpec((CV_BUF, seqs, CV_WIDTH), lambda i: (0, i, 0)),
        ],
        out_shape=[
            jax.ShapeDtypeStruct((DEC_SEQ, n_seq, D_MODEL), F32),
            jax.ShapeDtypeStruct(s_in.shape, F32),
            jax.ShapeDtypeStruct(cbuf_in.shape, F32),
        ],
        scratch_shapes=[pltpu.VMEM((HG_HEADS, rows, HG_DK), F32)] * 6,
        compiler_params=pltpu.CompilerParams(
            dimension_semantics=("arbitrary",), vmem_limit_bytes=VMEM_LIMIT_BYTES),
        name="mixer_sample",
    )(x, mod, s_in, cbuf_in, lb_logits, w_in, b_in, hgw, conv_w, conv_b, gn_g, gn_b, gmat,
      w_out, b_out, ln_g, ln_b)


def _mlp_kernel(x_ref, sh_ref, sc_ref, g_ref, w_up_ref, b_up_ref, w_down_ref, b_down_ref,
                lng_ref, lnb_ref, o_ref):
    x = x_ref[...]
    hb = (x * (1.0 + sc_ref[...]) + sh_ref[...]).astype(BF16).reshape(-1, D_MODEL)
    ff = jnp.broadcast_to(b_down_ref[...], hb.shape)
    for c0 in range(0, D_FF, FF_CHUNK):
        cs = slice(c0, c0 + FF_CHUNK)
        a = jnp.maximum(_dot(hb, w_up_ref[:, cs]) + b_up_ref[:, cs], 0.0)
        ff = ff + _dot((a * a).astype(BF16), w_down_ref[cs, :])
    y = ALPHA * x + (1.0 + g_ref[...]) * ff.reshape(x.shape)
    o_ref[...] = _layer_norm(y, lng_ref[...], lnb_ref[...])


def _mlp(x, x_spec, grid, mod_specs, mods, w_up, b_up, w_down, b_down, ln_g, ln_b, *, name):
    return pl.pallas_call(
        _mlp_kernel,
        grid=grid,
        in_specs=[x_spec] + mod_specs + [
            _const_spec(w_up.shape),
            _const_spec(b_up.shape),
            _const_spec(w_down.shape),
            _const_spec(b_down.shape),
            _const_spec(ln_g.shape),
            _const_spec(ln_b.shape),
        ],
        out_specs=x_spec,
        out_shape=jax.ShapeDtypeStruct(x.shape, F32),
        compiler_params=pltpu.CompilerParams(
            dimension_semantics=("arbitrary",), vmem_limit_bytes=VMEM_LIMIT_BYTES),
        name=name,
    )(x, *mods, w_up, b_up, w_down, b_down, ln_g, ln_b)


PROMPT_TILE = 256
SAMPLE_SEQS = 16


def kernel(x_prompt, x_sample, c_prompt, c_sample, state_hgrn, state_conv, lb_logits, w_in, b_in,
           hg_norm_w, conv_w, conv_b, gn_g, gn_b, w_out, b_out, ln1_g, ln1_b, w_up, b_up, w_down,
           b_down, ln2_g, ln2_b, w_ada, b_ada):
    assert w_in.shape[0] == DEPTH
    batch, seq, _ = x_prompt.shape
    dec_batch, dec_seq, _ = x_sample.shape
    assert dec_seq == DEC_SEQ

    group = jnp.arange(MXU_WIDTH, dtype=jnp.int32) // (CV_WIDTH // CV_GROUPS)
    gmat = jnp.where(group[:, None] == group[None, :], CV_GROUPS / CV_WIDTH, 0.0).astype(BF16)

    xp = x_prompt
    xs = jnp.transpose(x_sample, (1, 0, 2))
    hp, cp, hs, cs = [], [], [], []
    for l in range(DEPTH):
        row = lambda a: a[l][None, :]
        w_in_b, w_out_b = w_in[l].astype(BF16), w_out[l].astype(BF16)
        w_up_b, w_down_b = w_up[l].astype(BF16), w_down[l].astype(BF16)

        mod = _adaln(jnp.concatenate([c_prompt, c_sample], axis=0), w_ada[l], row(b_ada))
        mod_p = mod[:batch].reshape(batch, 6, D_MODEL)
        mod_s = mod[batch:]

        mixer_params = (lb_logits, w_in_b, row(b_in), row(hg_norm_w), conv_w[l], row(conv_b),
                        row(gn_g), row(gn_b), gmat, w_out_b, row(b_out), row(ln1_g), row(ln1_b))
        mlp_params = (w_up_b, row(b_up), w_down_b, row(b_down), row(ln2_g), row(ln2_b))
        xp, sp, bp = _prompt_layer(xp, mod_p, *mixer_params, *mlp_params, tile=PROMPT_TILE)
        xs, ss, bs = _mixer_sample(xs, mod_s, state_hgrn[l], jnp.transpose(state_conv[l], (1, 0, 2)),
                                   *mixer_params, seqs=SAMPLE_SEQS)
        s_specs = [pl.BlockSpec((dec_batch, D_MODEL), functools.partial(lambda i, r: (0, r), r=r))
                   for r in (3, 4, 5)]
        xs = _mlp(xs, pl.BlockSpec((DEC_SEQ, dec_batch, D_MODEL), lambda i: (0, 0, 0)), (1,),
                  s_specs, (mod_s,) * 3, *mlp_params, name="mlp_sample")

        hp.append(sp); cp.append(bp); hs.append(ss); cs.append(jnp.transpose(bs, (1, 0, 2)))

    return (xp, jnp.transpose(xs, (1, 0, 2)), jnp.stack(hp), jnp.stack(cp), jnp.stack(hs),
            jnp.stack(cs))
```

```python
import functools

import jax
import jax.numpy as jnp
from jax import lax
from jax.experimental import pallas as pl
from jax.experimental.pallas import tpu as pltpu

F32 = jnp.float32
BF16 = jnp.bfloat16

D_MODEL = 1024
HG_WIDTH = 512
CV_WIDTH = 512
HG_HEADS = 4
HG_DK = 128
CV_GROUPS = 8
CV_KERNEL = 31
CV_BUF = CV_KERNEL - 1
D_FF = 4 * D_MODEL
N_IN = 4 * HG_WIDTH + 2 * CV_WIDTH
DEPTH = 1
ALPHA = (2.0 * DEPTH) ** 0.25
EPS = 1e-5

CHUNK = 64
SUBLANES = 8
MXU_WIDTH = 256
CONV_PAD = 32
CONV_ROWS = 128
UP_CHUNK = 512
DOWN_CHUNK = 256
FF_CHUNK = 1024
DEC_SEQ = 4

VMEM_LIMIT_BYTES = 52 * 1024 * 1024

_NT = (((1,), (1,)), ((), ()))
_TN = (((0,), (0,)), ((), ()))


def _dot(a, b):
    return jnp.dot(a, b, preferred_element_type=F32)


def _dot_nt(a, b):
    return lax.dot_general(a, b, _NT, preferred_element_type=F32)


def _dot_tn(a, b):
    return lax.dot_general(a, b, _TN, preferred_element_type=F32)


def _silu(x):
    return x * jax.nn.sigmoid(x)


def _split3(x):
    hi = x.astype(BF16).astype(F32)
    r = x - hi
    mid = r.astype(BF16).astype(F32)
    lo = (r - mid).astype(BF16).astype(F32)
    return hi, mid, lo


def _layer_norm(y, g, b):
    mu = jnp.mean(y, axis=-1, keepdims=True)
    yc = y - mu
    var = jnp.mean(yc * yc, axis=-1, keepdims=True)
    return yc * lax.rsqrt(var + EPS) * g + b


def _adaln_kernel(c_ref, w_ref, b_ref, o_ref):
    a = _silu(c_ref[...])
    a_hi = a.astype(BF16)
    a_lo = (a - a_hi.astype(F32)).astype(BF16)
    w = w_ref[...]
    w_hi = w.astype(BF16)
    w_lo = (w - w_hi.astype(F32)).astype(BF16)
    o_ref[...] = _dot(a_hi, w_hi) + _dot(a_lo, w_hi) + _dot(a_hi, w_lo) + b_ref[...]


def _adaln(c_all, w_ada, b_ada):
    rows = c_all.shape[0]
    n = w_ada.shape[1]
    tn = 1024
    return pl.pallas_call(
        _adaln_kernel,
        grid=(n // tn,),
        in_specs=[
            pl.BlockSpec((rows, D_MODEL), lambda j: (0, 0)),
            pl.BlockSpec((D_MODEL, tn), lambda j: (0, j)),
            pl.BlockSpec((1, tn), lambda j: (0, j)),
        ],
        out_specs=pl.BlockSpec((rows, tn), lambda j: (0, j)),
        out_shape=jax.ShapeDtypeStruct((rows, n), F32),
        compiler_params=pltpu.CompilerParams(
            dimension_semantics=("arbitrary",), vmem_limit_bytes=VMEM_LIMIT_BYTES),
        name="adaln",
    )(c_all, w_ada, b_ada)


def _lower_bound(lb_ref):
    lg = lb_ref[...]
    e = jnp.exp(lg - jnp.max(lg, axis=0, keepdims=True))
    p = e / jnp.sum(e, axis=0, keepdims=True)
    return jnp.sum(p[0:DEPTH, :], axis=0, keepdims=True)


def _section(hb, w_in_ref, b_in_ref, i, width):
    return _dot(hb, w_in_ref[:, i:i + width]) + b_in_ref[:, i:i + width]


def _in_projection(hb, lb, w_in_ref, b_in_ref):
    sec = functools.partial(_section, hb, w_in_ref, b_in_ref)

    q = _silu(sec(0, HG_WIDTH))
    f = lb + (1.0 - lb) * jax.nn.sigmoid(sec(HG_WIDTH, HG_WIDTH))
    v = sec(2 * HG_WIDTH, HG_WIDTH)
    gate = _silu(sec(3 * HG_WIDTH, HG_WIDTH))
    za = sec(4 * HG_WIDTH, CV_WIDTH)
    zb = sec(4 * HG_WIDTH + CV_WIDTH, CV_WIDTH)
    return q, 1.0 - f, jnp.log(f), v, gate, za * jax.nn.sigmoid(zb)


def _head_norm(o, hgw, gate):
    ms = jnp.mean(o * o, axis=-1, keepdims=True)
    return o * lax.rsqrt(ms + EPS) * hgw * gate


def _group_norm_swish(uc, gmat_ref, gn_g, gn_b):
    width = gmat_ref.shape[0]

    def group_mean(a):
        ab = a.astype(BF16)
        return jnp.concatenate([_dot(ab[:, c0:c0 + width], gmat_ref[...])
                                for c0 in range(0, a.shape[1], width)], axis=-1)

    d = uc - group_mean(uc)
    var = group_mean(d * d)
    return _silu(d * lax.rsqrt(var + EPS) * gn_g + gn_b)


def _out_projection(mix_bf16, w_out_ref, b_out_ref):
    return _dot(mix_bf16, w_out_ref[...]) + b_out_ref[...]


def _prompt_kernel(x_ref, mod_ref, lb_ref, w_in_ref, b_in_ref, hgw_ref, cw_ref, cb_ref,
                   gng_ref, gnb_ref, gmat_ref, w_out_ref, b_out_ref, ln1g_ref, ln1b_ref,
                   w_up_ref, b_up_ref, w_down_ref, b_down_ref, ln2g_ref, ln2b_ref,
                   y_ref, s_out_ref, cbuf_out_ref,
                   q_s, k_s, lf_s, v_s, g_s, ubuf, ush, uc_s, mix_s, st_s,
                   x1_s, hb2_s, res2_s, act_s, ff_s, *, tile, tiles_per_seq, n_tiles):
    i = pl.program_id(0)
    mixer_tile = jnp.minimum(i, n_tiles - 1)
    b_mix = mixer_tile // tiles_per_seq
    t = mixer_tile % tiles_per_seq
    b_mlp = jnp.maximum(i - 1, 0) // tiles_per_seq

    @pl.when(i == 0)
    def _():
        x1_s[...] = jnp.zeros_like(x1_s)

    @pl.when(t == 0)
    def _():
        st_s[...] = jnp.zeros_like(st_s)
        ubuf[0:CONV_PAD, :] = jnp.zeros((CONV_PAD, CV_WIDTH), F32)

    x1_prev = x1_s[...]
    sh2 = mod_ref[b_mlp, 3:4, :]
    sc2 = mod_ref[b_mlp, 4:5, :]
    g2 = mod_ref[b_mlp, 5:6, :]
    hb2_s[...] = (x1_prev * (1.0 + sc2) + sh2).astype(BF16)
    res2_s[...] = ALPHA * x1_prev

    def mlp_up(c0):
        cs = slice(c0, c0 + UP_CHUNK)
        a = jnp.maximum(_dot(hb2_s[...], w_up_ref[:, cs]) + b_up_ref[:, cs], 0.0)
        act_s[:, cs] = (a * a).astype(BF16)

    def mlp_down(c0):
        cs = slice(c0, c0 + DOWN_CHUNK)
        ff_s[:, cs] = _dot(act_s[...], w_down_ref[:, cs]) + b_down_ref[:, cs]

    x = x_ref[0]
    sh1 = mod_ref[b_mix, 0:1, :]
    sc1 = mod_ref[b_mix, 1:2, :]
    g1 = mod_ref[b_mix, 2:3, :]
    hb = (x * (1.0 + sc1) + sh1).astype(BF16)
    sec = functools.partial(_section, hb, w_in_ref, b_in_ref)
    lb = _lower_bound(lb_ref)

    up_chunks = list(range(0, D_FF, UP_CHUNK))
    down_chunks = list(range(0, D_MODEL, DOWN_CHUNK))

    ubuf[CONV_PAD:CONV_PAD + tile, :] = (sec(4 * HG_WIDTH, CV_WIDTH)
                                         * jax.nn.sigmoid(sec(4 * HG_WIDTH + CV_WIDTH, CV_WIDTH)))
    first_off = CONV_PAD - CV_BUF
    shifted_rows = ush.shape[1]
    for phase in range(1, SUBLANES):
        ush[phase - 1] = ubuf[phase:phase + shifted_rows, :]

    def conv_piece(r0, c0):
        cs = slice(c0, c0 + 128)
        acc = jnp.broadcast_to(cb_ref[:, cs], (CONV_ROWS, 128))
        for phase in range(SUBLANES):
            taps = [j for j in range(CV_KERNEL) if (first_off + j) % SUBLANES == phase]
            span = (first_off + taps[-1]) // SUBLANES * SUBLANES
            rows = slice(r0, r0 + span + CONV_ROWS)
            win = ubuf[rows, cs] if phase == 0 else ush[phase - 1, rows, cs]
            for j in taps:
                a0 = first_off + j - phase
                acc = acc + cw_ref[j:j + 1, cs] * win[a0:a0 + CONV_ROWS, :]
        uc_s[r0:r0 + CONV_ROWS, cs] = acc

    pieces = [(r0, c0) for r0 in range(0, tile, CONV_ROWS) for c0 in range(0, CV_WIDTH, 128)]
    per_sec = -(-len(pieces) // 4)

    def conv_pieces(n):
        for r0, c0 in pieces[n * per_sec:(n + 1) * per_sec]:
            conv_piece(r0, c0)

    mlp_up(up_chunks.pop(0))
    q_s[...] = _silu(sec(0, HG_WIDTH))
    conv_pieces(0)
    mlp_up(up_chunks.pop(0))
    f = lb + (1.0 - lb) * jax.nn.sigmoid(sec(HG_WIDTH, HG_WIDTH))
    k_s[...] = 1.0 - f
    lf_s[...] = jnp.log(f)
    conv_pieces(1)
    mlp_up(up_chunks.pop(0))
    v_s[...] = sec(2 * HG_WIDTH, HG_WIDTH)
    conv_pieces(2)
    mlp_up(up_chunks.pop(0))
    g_s[...] = _silu(sec(3 * HG_WIDTH, HG_WIDTH))
    conv_pieces(3)

    row = lax.broadcasted_iota(jnp.int32, (CHUNK, HG_WIDTH), 0)
    causal = (lax.broadcasted_iota(jnp.int32, (CHUNK, CHUNK), 0)
              >= lax.broadcasted_iota(jnp.int32, (CHUNK, CHUNK), 1))
    hgw = hgw_ref[...]
    for c in range(tile // CHUNK):
        if up_chunks:
            mlp_up(up_chunks.pop(0))
        rows = slice(c * CHUNK, (c + 1) * CHUNK)
        b = lf_s[rows, :]
        shift = 1
        while shift < CHUNK:
            b = b + jnp.where(row >= shift, pltpu.roll(b, shift, axis=0), 0.0)
            shift *= 2
        b_last = b[CHUNK - 1:CHUNK, :]
        kc = k_s[rows, :]
        qe = (q_s[rows, :] * jnp.exp(b)).astype(BF16)
        ke = (kc * jnp.exp(-b)).astype(BF16)
        kd = (kc * jnp.exp(b_last - b)).astype(BF16)
        decay = jnp.exp(b_last)
        vb = v_s[rows, :].astype(BF16)
        for h in range(HG_HEADS):
            hs = slice(h * HG_DK, (h + 1) * HG_DK)
            att = jnp.where(causal, _dot_nt(qe[:, hs], ke[:, hs]), 0.0).astype(BF16)
            st = st_s[h]
            o = _dot(att, vb[:, hs]) + _dot_nt(qe[:, hs], st.astype(BF16))
            st_s[h] = st * decay[:, hs] + _dot_tn(vb[:, hs], kd[:, hs])
            mix_s[rows, hs] = _head_norm(o, hgw, g_s[rows, hs]).astype(BF16)
    while up_chunks:
        mlp_up(up_chunks.pop(0))

    mlp_down(down_chunks.pop(0))
    mix_s[:, HG_WIDTH:] = _group_norm_swish(
        uc_s[...], gmat_ref, gng_ref[...], gnb_ref[...]).astype(BF16)
    ubuf[0:CONV_PAD, :] = ubuf[tile:tile + CONV_PAD, :]
    mlp_down(down_chunks.pop(0))
    mix = _out_projection(mix_s[...], w_out_ref, b_out_ref)
    x1_s[...] = _layer_norm(ALPHA * x + (1.0 + g1) * mix, ln1g_ref[...], ln1b_ref[...])
    while down_chunks:
        mlp_down(down_chunks.pop(0))
    y_ref[0] = _layer_norm(res2_s[...] + (1.0 + g2) * ff_s[...], ln2g_ref[...], ln2b_ref[...])

    @pl.when((t == tiles_per_seq - 1) & (i < n_tiles))
    def _():
        for h in range(HG_HEADS):
            s_out_ref[0, h] = st_s[h].T
        cbuf_out_ref[0] = ubuf[CONV_PAD - CV_BUF:CONV_PAD, :]


def _const_spec(shape):
    zeros = (0,) * len(shape)
    return pl.BlockSpec(shape, lambda *_: zeros, pipeline_mode=pl.Buffered(1))


def _prompt_layer(x, mod, lb_logits, w_in, b_in, hgw, conv_w, conv_b, gn_g, gn_b, gmat,
                  w_out, b_out, ln1_g, ln1_b, w_up, b_up, w_down, b_down, ln2_g, ln2_b, *, tile):
    batch, seq, _ = x.shape
    tiles_per_seq = seq // tile
    n_tiles = batch * tiles_per_seq
    kernel = functools.partial(_prompt_kernel, tile=tile, tiles_per_seq=tiles_per_seq,
                               n_tiles=n_tiles)

    def mixer_block(i):
        m = jnp.minimum(i, n_tiles - 1)
        return m // tiles_per_seq, m % tiles_per_seq

    def mlp_block(i):
        m = jnp.maximum(i - 1, 0)
        return m // tiles_per_seq, m % tiles_per_seq

    consts = (mod, lb_logits, w_in, b_in, hgw, conv_w, conv_b, gn_g, gn_b, gmat, w_out, b_out,
              ln1_g, ln1_b, w_up, b_up, w_down, b_down, ln2_g, ln2_b)
    return pl.pallas_call(
        kernel,
        grid=(n_tiles + 1,),
        in_specs=[pl.BlockSpec((1, tile, D_MODEL), lambda i: (*mixer_block(i), 0))]
        + [_const_spec(a.shape) for a in consts],
        out_specs=[
            pl.BlockSpec((1, tile, D_MODEL), lambda i: (*mlp_block(i), 0)),
            pl.BlockSpec((1, HG_HEADS, HG_DK, HG_DK), lambda i: (mixer_block(i)[0], 0, 0, 0)),
            pl.BlockSpec((1, CV_BUF, CV_WIDTH), lambda i: (mixer_block(i)[0], 0, 0)),
        ],
        out_shape=[
            jax.ShapeDtypeStruct((batch, seq, D_MODEL), F32),
            jax.ShapeDtypeStruct((batch, HG_HEADS, HG_DK, HG_DK), F32),
            jax.ShapeDtypeStruct((batch, CV_BUF, CV_WIDTH), F32),
        ],
        scratch_shapes=[
            pltpu.VMEM((tile, HG_WIDTH), F32),
            pltpu.VMEM((tile, HG_WIDTH), F32),
            pltpu.VMEM((tile, HG_WIDTH), F32),
            pltpu.VMEM((tile, HG_WIDTH), F32),
            pltpu.VMEM((tile, HG_WIDTH), F32),
            pltpu.VMEM((CONV_PAD + tile, CV_WIDTH), F32),
            pltpu.VMEM((SUBLANES - 1, CONV_PAD + tile - SUBLANES, CV_WIDTH), F32),
            pltpu.VMEM((tile, CV_WIDTH), F32),
            pltpu.VMEM((tile, D_MODEL), BF16),
            pltpu.VMEM((HG_HEADS, HG_DK, HG_DK), F32),
            pltpu.VMEM((tile, D_MODEL), F32),
            pltpu.VMEM((tile, D_MODEL), BF16),
            pltpu.VMEM((tile, D_MODEL), F32),
            pltpu.VMEM((tile, D_FF), BF16),
            pltpu.VMEM((tile, D_MODEL), F32),
        ],
        compiler_params=pltpu.CompilerParams(
            dimension_semantics=("arbitrary",), vmem_limit_bytes=VMEM_LIMIT_BYTES),
        name="prompt_layer",
    )(x, *consts)


def _mixer_sample_kernel(x_ref, mod_ref, s_in_ref, cbuf_in_ref, lb_ref, w_in_ref, b_in_ref, hgw_ref,
                         cw_ref, cb_ref, gng_ref, gnb_ref, gmat_ref, w_out_ref, b_out_ref,
                         lng_ref, lnb_ref,
                         x1_ref, s_out_ref, cbuf_out_ref,
                         qe_s, ke_s, kd_s, v_s, d_s, o_s, *, seqs):
    rows_n = seqs * DEC_SEQ
    half = seqs // 2
    x = x_ref[...]
    sh1 = mod_ref[:, 0:D_MODEL]
    sc1 = mod_ref[:, D_MODEL:2 * D_MODEL]
    g1 = mod_ref[:, 2 * D_MODEL:3 * D_MODEL]
    hb = (x * (1.0 + sc1) + sh1).astype(BF16).reshape(rows_n, D_MODEL)
    q, k, lf, v, gate, u = _in_projection(hb, _lower_bound(lb_ref), w_in_ref, b_in_ref)

    slab = lambda a, t: a[t * seqs:(t + 1) * seqs, :]
    b = [slab(lf, 0)]
    for t in range(1, DEC_SEQ):
        b.append(b[-1] + slab(lf, t))
    b_last = b[-1]
    parts = _split3(jnp.exp(b_last)) + (jnp.zeros_like(b_last),)

    def put(ref, t, val):
        for h in range(HG_HEADS):
            ref[h, t * seqs:(t + 1) * seqs, :] = val[:, h * HG_DK:(h + 1) * HG_DK]

    for t in range(DEC_SEQ):
        put(qe_s, t, slab(q, t) * jnp.exp(b[t]))
        put(ke_s, t, slab(k, t) * jnp.exp(-b[t]))
        put(kd_s, t, slab(k, t) * jnp.exp(b_last - b[t]))
        put(d_s, t, parts[t])
        put(v_s, t, slab(v, t))

    r8 = lax.broadcasted_iota(jnp.int32, (2 * DEC_SEQ, HG_DK), 0)
    even = r8 % 2 == 0
    ones_ab = jnp.concatenate([jnp.where(even, 1.0, 0.0), jnp.where(even, 0.0, 1.0)],
                              axis=1).astype(BF16)
    i8 = lax.broadcasted_iota(jnp.int32, (2 * DEC_SEQ, 2 * DEC_SEQ), 0)
    j8 = lax.broadcasted_iota(jnp.int32, (2 * DEC_SEQ, 2 * DEC_SEQ), 1)
    causal = (i8 // 2 >= j8 // 2) & (i8 % 2 == j8 % 2)
    for s in range(half):
        rows = pl.ds(s, 2 * DEC_SEQ, stride=half)
        for h in range(HG_HEADS):
            qe8 = qe_s[h, rows, :].astype(BF16)
            ke8 = ke_s[h, rows, :].astype(BF16)
            kd8 = kd_s[h, rows, :].astype(BF16)
            d8 = d_s[h, rows, :].astype(BF16)
            v8 = v_s[h, rows, :]
            v_ab = jnp.concatenate([jnp.where(even, v8, 0.0), jnp.where(even, 0.0, v8)],
                                   axis=1).astype(BF16)
            s_ab = jnp.concatenate([s_in_ref[s, h], s_in_ref[s + half, h]], axis=1)
            att = jnp.where(causal, _dot_nt(qe8, ke8), 0.0).astype(BF16)
            inter = _dot(qe8, s_ab.astype(BF16))
            o_s[h, rows, :] = (_dot(att, v8.astype(BF16))
                               + jnp.where(even, inter[:, :HG_DK], inter[:, HG_DK:]))
            new = _dot_tn(d8, ones_ab) * s_ab + _dot_tn(kd8, v_ab)
            s_out_ref[s, h] = new[:, :HG_DK]
            s_out_ref[s + half, h] = new[:, HG_DK:]
    hgw = hgw_ref[...]
    o_a = jnp.concatenate(
        [_head_norm(o_s[h], hgw, gate[:, h * HG_DK:(h + 1) * HG_DK]) for h in range(HG_HEADS)],
        axis=-1)

    full = lambda i: cbuf_in_ref[i] if i < CV_BUF else slab(u, i - CV_BUF)
    acc = [jnp.broadcast_to(cb_ref[...], (seqs, CV_WIDTH)) for _ in range(DEC_SEQ)]
    for i in range(CV_BUF + DEC_SEQ):
        f_i = full(i)
        for t in range(DEC_SEQ):
            if 0 <= i - t < CV_KERNEL:
                acc[t] = acc[t] + cw_ref[i - t:i - t + 1, :] * f_i
    for i in range(CV_BUF):
        cbuf_out_ref[i] = full(i + DEC_SEQ)
    o_b = _group_norm_swish(jnp.concatenate(acc, axis=0), gmat_ref, gng_ref[...], gnb_ref[...])

    mix = _out_projection(jnp.concatenate([o_a, o_b], axis=-1).astype(BF16), w_out_ref, b_out_ref)
    y = ALPHA * x + (1.0 + g1) * mix.reshape(DEC_SEQ, seqs, D_MODEL)
    x1_ref[...] = _layer_norm(y, lng_ref[...], lnb_ref[...])


def _mixer_sample(x, mod, s_in, cbuf_in, lb_logits, w_in, b_in, hgw, conv_w, conv_b, gn_g, gn_b,
                  gmat, w_out, b_out, ln_g, ln_b, *, seqs):
    n_seq = s_in.shape[0]
    rows = seqs * DEC_SEQ
    kernel = functools.partial(_mixer_sample_kernel, seqs=seqs)
    return pl.pallas_call(
        kernel,
        grid=(n_seq // seqs,),
        in_specs=[
            pl.BlockSpec((DEC_SEQ, seqs, D_MODEL), lambda i: (0, i, 0)),
            pl.BlockSpec((seqs, 3 * D_MODEL), lambda i: (i, 0)),
            pl.BlockSpec((seqs, HG_HEADS, HG_DK, HG_DK), lambda i: (i, 0, 0, 0)),
            pl.BlockSpec((CV_BUF, seqs, CV_WIDTH), lambda i: (0, i, 0)),
            _const_spec(lb_logits.shape),
            _const_spec(w_in.shape),
            _const_spec(b_in.shape),
            _const_spec(hgw.shape),
            _const_spec(conv_w.shape),
            _const_spec(conv_b.shape),
            _const_spec(gn_g.shape),
            _const_spec(gn_b.shape),
            _const_spec(gmat.shape),
            _const_spec(w_out.shape),
            _const_spec(b_out.shape),
            _const_spec(ln_g.shape),
            _const_spec(ln_b.shape),
        ],
        out_specs=[
            pl.BlockSpec((DEC_SEQ, seqs, D_MODEL), lambda i: (0, i, 0)),
            pl.BlockSpec((seqs, HG_HEADS, HG_DK, HG_DK), lambda i: (i, 0, 0, 0)),
            pl.BlockSpec((CV_BUF, seqs, CV_WIDTH), lambda i: (0, i, 0)),
        ],
        out_shape=[
            jax.ShapeDtypeStruct((DEC_SEQ, n_seq, D_MODEL), F32),
            jax.ShapeDtypeStruct(s_in.shape, F32),
            jax.ShapeDtypeStruct(cbuf_in.shape, F32),
        ],
        scratch_shapes=[pltpu.VMEM((HG_HEADS, rows, HG_DK), F32)] * 6,
        compiler_params=pltpu.CompilerParams(
            dimension_semantics=("arbitrary",), vmem_limit_bytes=VMEM_LIMIT_BYTES),
        name="mixer_sample",
    )(x, mod, s_in, cbuf_in, lb_logits, w_in, b_in, hgw, conv_w, conv_b, gn_g, gn_b, gmat,
      w_out, b_out, ln_g, ln_b)


def _mlp_kernel(x_ref, sh_ref, sc_ref, g_ref, w_up_ref, b_up_ref, w_down_ref, b_down_ref,
                lng_ref, lnb_ref, o_ref):
    x = x_ref[...]
    hb = (x * (1.0 + sc_ref[...]) + sh_ref[...]).astype(BF16).reshape(-1, D_MODEL)
    ff = jnp.broadcast_to(b_down_ref[...], hb.shape)
    for c0 in range(0, D_FF, FF_CHUNK):
        cs = slice(c0, c0 + FF_CHUNK)
        a = jnp.maximum(_dot(hb, w_up_ref[:, cs]) + b_up_ref[:, cs], 0.0)
        ff = ff + _dot((a * a).astype(BF16), w_down_ref[cs, :])
    y = ALPHA * x + (1.0 + g_ref[...]) * ff.reshape(x.shape)
    o_ref[...] = _layer_norm(y, lng_ref[...], lnb_ref[...])


def _mlp(x, x_spec, grid, mod_specs, mods, w_up, b_up, w_down, b_down, ln_g, ln_b, *, name):
    return pl.pallas_call(
        _mlp_kernel,
        grid=grid,
        in_specs=[x_spec] + mod_specs + [
            _const_spec(w_up.shape),
            _const_spec(b_up.shape),
            _const_spec(w_down.shape),
            _const_spec(b_down.shape),
            _const_spec(ln_g.shape),
            _const_spec(ln_b.shape),
        ],
        out_specs=x_spec,
        out_shape=jax.ShapeDtypeStruct(x.shape, F32),
        compiler_params=pltpu.CompilerParams(
            dimension_semantics=("arbitrary",), vmem_limit_bytes=VMEM_LIMIT_BYTES),
        name=name,
    )(x, *mods, w_up, b_up, w_down, b_down, ln_g, ln_b)


PROMPT_TILE = 256
SAMPLE_SEQS = 16


def kernel(x_prompt, x_sample, c_prompt, c_sample, state_hgrn, state_conv, lb_logits, w_in, b_in,
           hg_norm_w, conv_w, conv_b, gn_g, gn_b, w_out, b_out, ln1_g, ln1_b, w_up, b_up, w_down,
           b_down, ln2_g, ln2_b, w_ada, b_ada):
    assert w_in.shape[0] == DEPTH
    batch, seq, _ = x_prompt.shape
    dec_batch, dec_seq, _ = x_sample.shape
    assert dec_seq == DEC_SEQ

    group = jnp.arange(MXU_WIDTH, dtype=jnp.int32) // (CV_WIDTH // CV_GROUPS)
    gmat = jnp.where(group[:, None] == group[None, :], CV_GROUPS / CV_WIDTH, 0.0).astype(BF16)

    xp = x_prompt
    xs = jnp.transpose(x_sample, (1, 0, 2))
    hp, cp, hs, cs = [], [], [], []
    for l in range(DEPTH):
        row = lambda a: a[l][None, :]
        w_in_b, w_out_b = w_in[l].astype(BF16), w_out[l].astype(BF16)
        w_up_b, w_down_b = w_up[l].astype(BF16), w_down[l].astype(BF16)

        mod = _adaln(jnp.concatenate([c_prompt, c_sample], axis=0), w_ada[l], row(b_ada))
        mod_p = mod[:batch].reshape(batch, 6, D_MODEL)
        mod_s = mod[batch:]

        mixer_params = (lb_logits, w_in_b, row(b_in), row(hg_norm_w), conv_w[l], row(conv_b),
                        row(gn_g), row(gn_b), gmat, w_out_b, row(b_out), row(ln1_g), row(ln1_b))
        mlp_params = (w_up_b, row(b_up), w_down_b, row(b_down), row(ln2_g), row(ln2_b))
        xp, sp, bp = _prompt_layer(xp, mod_p, *mixer_params, *mlp_params, tile=PROMPT_TILE)
        xs, ss, bs = _mixer_sample(xs, mod_s, state_hgrn[l], jnp.transpose(state_conv[l], (1, 0, 2)),
                                   *mixer_params, seqs=SAMPLE_SEQS)
        s_specs = [pl.BlockSpec((dec_batch, D_MODEL), functools.partial(lambda i, r: (0, r), r=r))
                   for r in (3, 4, 5)]
        xs = _mlp(xs, pl.BlockSpec((DEC_SEQ, dec_batch, D_MODEL), lambda i: (0, 0, 0)), (1,),
                  s_specs, (mod_s,) * 3, *mlp_params, name="mlp_sample")

        hp.append(sp); cp.append(bp); hs.append(ss); cs.append(jnp.transpose(bs, (1, 0, 2)))

    return (xp, jnp.transpose(xs, (1, 0, 2)), jnp.stack(hp), jnp.stack(cp), jnp.stack(hs),
            jnp.stack(cs))
```

```python
import functools

import jax
import jax.numpy as jnp
from jax import lax
from jax.experimental import pallas as pl
from jax.experimental.pallas import tpu as pltpu

F32 = jnp.float32
BF16 = jnp.bfloat16

D_MODEL = 1024
HG_WIDTH = 512
CV_WIDTH = 512
HG_HEADS = 4
HG_DK = 128
CV_GROUPS = 8
CV_KERNEL = 31
CV_BUF = CV_KERNEL - 1
D_FF = 4 * D_MODEL
N_IN = 4 * HG_WIDTH + 2 * CV_WIDTH
DEPTH = 1
ALPHA = (2.0 * DEPTH) ** 0.25
EPS = 1e-5

CHUNK = 64
SUBLANES = 8
MXU_WIDTH = 256
CONV_PAD = 32
CONV_ROWS = 128
UP_CHUNK = 512
DOWN_CHUNK = 256
FF_CHUNK = 1024
DEC_SEQ = 4
SEQ_GROUP = 4

VMEM_LIMIT_BYTES = 52 * 1024 * 1024
CAST_BLOCK_BYTES = 4 * 1024 * 1024
ADALN_COLS = 512

_NT = (((1,), (1,)), ((), ()))
_TN = (((0,), (0,)), ((), ()))


def _dot(a, b):
    return jnp.dot(a, b, preferred_element_type=F32)


def _dot_nt(a, b):
    return lax.dot_general(a, b, _NT, preferred_element_type=F32)


def _dot_tn(a, b):
    return lax.dot_general(a, b, _TN, preferred_element_type=F32)


def _silu(x):
    return x * jax.nn.sigmoid(x)


def _split3(x):
    hi = x.astype(BF16).astype(F32)
    r = x - hi
    mid = r.astype(BF16).astype(F32)
    lo = (r - mid).astype(BF16).astype(F32)
    return hi, mid, lo


def _layer_norm(y, g, b):
    mu = jnp.mean(y, axis=-1, keepdims=True)
    yc = y - mu
    var = jnp.mean(yc * yc, axis=-1, keepdims=True)
    return yc * lax.rsqrt(var + EPS) * g + b


def _adaln_kernel(c_ref, w_ref, b_ref, o_ref):
    a = _silu(c_ref[...])
    a_hi = a.astype(BF16)
    a_lo = (a - a_hi.astype(F32)).astype(BF16)
    w = w_ref[...]
    w_hi = w.astype(BF16)
    w_lo = (w - w_hi.astype(F32)).astype(BF16)
    o_ref[...] = _dot(a_hi, w_hi) + _dot(a_lo, w_hi) + _dot(a_hi, w_lo) + b_ref[...]


def _adaln(c_all, w_ada, b_ada):
    rows = c_all.shape[0]
    n = w_ada.shape[1]
    tn = ADALN_COLS
    return pl.pallas_call(
        _adaln_kernel,
        grid=(n // tn,),
        in_specs=[
            pl.BlockSpec((rows, D_MODEL), lambda j: (0, 0)),
            pl.BlockSpec((D_MODEL, tn), lambda j: (0, j)),
            pl.BlockSpec((1, tn), lambda j: (0, j)),
        ],
        out_specs=pl.BlockSpec((rows, tn), lambda j: (0, j)),
        out_shape=jax.ShapeDtypeStruct((rows, n), F32),
        compiler_params=pltpu.CompilerParams(
            dimension_semantics=("arbitrary",), vmem_limit_bytes=VMEM_LIMIT_BYTES),
        name="adaln",
    )(c_all, w_ada, b_ada)


def _cast_kernel(w_ref, o_ref):
    o_ref[...] = w_ref[...].astype(o_ref.dtype)


def _to_bf16(w):
    rows, cols = w.shape
    block_rows = rows
    while block_rows * cols * w.dtype.itemsize > CAST_BLOCK_BYTES and block_rows % 16 == 0:
        block_rows //= 2
    spec = pl.BlockSpec((block_rows, cols), lambda i: (i, 0))
    return pl.pallas_call(
        _cast_kernel,
        grid=(rows // block_rows,),
        in_specs=[spec],
        out_specs=spec,
        out_shape=jax.ShapeDtypeStruct(w.shape, BF16),
        compiler_params=pltpu.CompilerParams(
            dimension_semantics=("arbitrary",), vmem_limit_bytes=VMEM_LIMIT_BYTES),
        name="to_bf16",
    )(w)


def _lower_bound(lb_ref):
    lg = lb_ref[...]
    e = jnp.exp(lg - jnp.max(lg, axis=0, keepdims=True))
    p = e / jnp.sum(e, axis=0, keepdims=True)
    return jnp.sum(p[0:DEPTH, :], axis=0, keepdims=True)


def _section(hb, w_in_ref, b_in_ref, i, width):
    return _dot(hb, w_in_ref[:, i:i + width]) + b_in_ref[:, i:i + width]


def _in_projection(hb, lb, w_in_ref, b_in_ref):
    sec = functools.partial(_section, hb, w_in_ref, b_in_ref)

    q = _silu(sec(0, HG_WIDTH))
    f = lb + (1.0 - lb) * jax.nn.sigmoid(sec(HG_WIDTH, HG_WIDTH))
    v = sec(2 * HG_WIDTH, HG_WIDTH)
    gate = _silu(sec(3 * HG_WIDTH, HG_WIDTH))
    za = sec(4 * HG_WIDTH, CV_WIDTH)
    zb = sec(4 * HG_WIDTH + CV_WIDTH, CV_WIDTH)
    return q, 1.0 - f, jnp.log(f), v, gate, za * jax.nn.sigmoid(zb)


def _head_norm(o, hgw, gate):
    ms = jnp.mean(o * o, axis=-1, keepdims=True)
    return o * lax.rsqrt(ms + EPS) * hgw * gate


def _group_norm_swish(uc, gmat_ref, gn_g, gn_b):
    width = gmat_ref.shape[0]

    def group_mean(a):
        ab = a.astype(BF16)
        return jnp.concatenate([_dot(ab[:, c0:c0 + width], gmat_ref[...])
                                for c0 in range(0, a.shape[1], width)], axis=-1)

    d = uc - group_mean(uc)
    var = group_mean(d * d)
    return _silu(d * lax.rsqrt(var + EPS) * gn_g + gn_b)


def _out_projection(mix_bf16, w_out_ref, b_out_ref):
    return _dot(mix_bf16, w_out_ref[...]) + b_out_ref[...]


def _prompt_kernel(x_ref, mod_ref, lb_ref, w_in_ref, b_in_ref, hgw_ref, cw_ref, cb_ref,
                   gng_ref, gnb_ref, gmat_ref, w_out_ref, b_out_ref, ln1g_ref, ln1b_ref,
                   w_up_ref, b_up_ref, w_down_ref, b_down_ref, ln2g_ref, ln2b_ref,
                   y_ref, s_out_ref, cbuf_out_ref,
                   q_s, k_s, lf_s, v_s, g_s, ubuf, ush, uc_s, mix_s, st_s,
                   x1_s, hb2_s, res2_s, act_s, ff_s, *, tile, tiles_per_seq, n_tiles):
    i = pl.program_id(0)
    mixer_tile = jnp.minimum(i, n_tiles - 1)
    b_mix = mixer_tile // tiles_per_seq
    t = mixer_tile % tiles_per_seq
    b_mlp = jnp.maximum(i - 1, 0) // tiles_per_seq

    @pl.when(i == 0)
    def _():
        x1_s[...] = jnp.zeros_like(x1_s)

    @pl.when(t == 0)
    def _():
        st_s[...] = jnp.zeros_like(st_s)
        ubuf[0:CONV_PAD, :] = jnp.zeros((CONV_PAD, CV_WIDTH), F32)

    x1_prev = x1_s[...]
    sh2 = mod_ref[b_mlp, 3:4, :]
    sc2 = mod_ref[b_mlp, 4:5, :]
    g2 = mod_ref[b_mlp, 5:6, :]
    hb2_s[...] = (x1_prev * (1.0 + sc2) + sh2).astype(BF16)
    res2_s[...] = ALPHA * x1_prev

    def mlp_up(c0):
        cs = slice(c0, c0 + UP_CHUNK)
        a = jnp.maximum(_dot(hb2_s[...], w_up_ref[:, cs]) + b_up_ref[:, cs], 0.0)
        act_s[:, cs] = (a * a).astype(BF16)

    def mlp_down(c0):
        cs = slice(c0, c0 + DOWN_CHUNK)
        ff_s[:, cs] = _dot(act_s[...], w_down_ref[:, cs]) + b_down_ref[:, cs]

    x = x_ref[0]
    sh1 = mod_ref[b_mix, 0:1, :]
    sc1 = mod_ref[b_mix, 1:2, :]
    g1 = mod_ref[b_mix, 2:3, :]
    hb = (x * (1.0 + sc1) + sh1).astype(BF16)
    sec = functools.partial(_section, hb, w_in_ref, b_in_ref)
    lb = _lower_bound(lb_ref)

    up_chunks = list(range(0, D_FF, UP_CHUNK))
    down_chunks = list(range(0, D_MODEL, DOWN_CHUNK))

    ubuf[CONV_PAD:CONV_PAD + tile, :] = (sec(4 * HG_WIDTH, CV_WIDTH)
                                         * jax.nn.sigmoid(sec(4 * HG_WIDTH + CV_WIDTH, CV_WIDTH)))
    first_off = CONV_PAD - CV_BUF
    shifted_rows = ush.shape[1]
    for phase in range(1, SUBLANES):
        ush[phase - 1] = ubuf[phase:phase + shifted_rows, :]

    def conv_piece(r0, c0):
        cs = slice(c0, c0 + 128)
        acc = jnp.broadcast_to(cb_ref[:, cs], (CONV_ROWS, 128))
        for phase in range(SUBLANES):
            taps = [j for j in range(CV_KERNEL) if (first_off + j) % SUBLANES == phase]
            span = (first_off + taps[-1]) // SUBLANES * SUBLANES
            rows = slice(r0, r0 + span + CONV_ROWS)
            win = ubuf[rows, cs] if phase == 0 else ush[phase - 1, rows, cs]
            for j in taps:
                a0 = first_off + j - phase
                acc = acc + cw_ref[j:j + 1, cs] * win[a0:a0 + CONV_ROWS, :]
        uc_s[r0:r0 + CONV_ROWS, cs] = acc

    pieces = [(r0, c0) for r0 in range(0, tile, CONV_ROWS) for c0 in range(0, CV_WIDTH, 128)]
    per_sec = -(-len(pieces) // 4)

    def conv_pieces(n):
        for r0, c0 in pieces[n * per_sec:(n + 1) * per_sec]:
            conv_piece(r0, c0)

    mlp_up(up_chunks.pop(0))
    q_s[...] = _silu(sec(0, HG_WIDTH))
    conv_pieces(0)
    mlp_up(up_chunks.pop(0))
    f = lb + (1.0 - lb) * jax.nn.sigmoid(sec(HG_WIDTH, HG_WIDTH))
    k_s[...] = 1.0 - f
    lf_s[...] = jnp.log(f)
    conv_pieces(1)
    mlp_up(up_chunks.pop(0))
    v_s[...] = sec(2 * HG_WIDTH, HG_WIDTH)
    conv_pieces(2)
    mlp_up(up_chunks.pop(0))
    g_s[...] = _silu(sec(3 * HG_WIDTH, HG_WIDTH))
    conv_pieces(3)

    row = lax.broadcasted_iota(jnp.int32, (CHUNK, HG_WIDTH), 0)
    causal = (lax.broadcasted_iota(jnp.int32, (CHUNK, CHUNK), 0)
              >= lax.broadcasted_iota(jnp.int32, (CHUNK, CHUNK), 1))
    hgw = hgw_ref[...]
    for c in range(tile // CHUNK):
        if up_chunks:
            mlp_up(up_chunks.pop(0))
        rows = slice(c * CHUNK, (c + 1) * CHUNK)
        b = lf_s[rows, :]
        shift = 1
        while shift < CHUNK:
            b = b + jnp.where(row >= shift, pltpu.roll(b, shift, axis=0), 0.0)
            shift *= 2
        b_last = b[CHUNK - 1:CHUNK, :]
        kc = k_s[rows, :]
        qe = (q_s[rows, :] * jnp.exp(b)).astype(BF16)
        ke = (kc * jnp.exp(-b)).astype(BF16)
        kd = (kc * jnp.exp(b_last - b)).astype(BF16)
        decay = jnp.exp(b_last)
        vb = v_s[rows, :].astype(BF16)
        for h in range(HG_HEADS):
            hs = slice(h * HG_DK, (h + 1) * HG_DK)
            att = jnp.where(causal, _dot_nt(qe[:, hs], ke[:, hs]), 0.0).astype(BF16)
            st = st_s[h]
            o = _dot(att, vb[:, hs]) + _dot_nt(qe[:, hs], st.astype(BF16))
            st_s[h] = st * decay[:, hs] + _dot_tn(vb[:, hs], kd[:, hs])
            mix_s[rows, hs] = _head_norm(o, hgw, g_s[rows, hs]).astype(BF16)
    while up_chunks:
        mlp_up(up_chunks.pop(0))

    mlp_down(down_chunks.pop(0))
    mix_s[:, HG_WIDTH:] = _group_norm_swish(
        uc_s[...], gmat_ref, gng_ref[...], gnb_ref[...]).astype(BF16)
    ubuf[0:CONV_PAD, :] = ubuf[tile:tile + CONV_PAD, :]
    mlp_down(down_chunks.pop(0))
    mix = _out_projection(mix_s[...], w_out_ref, b_out_ref)
    x1_s[...] = _layer_norm(ALPHA * x + (1.0 + g1) * mix, ln1g_ref[...], ln1b_ref[...])
    while down_chunks:
        mlp_down(down_chunks.pop(0))
    y_ref[0] = _layer_norm(res2_s[...] + (1.0 + g2) * ff_s[...], ln2g_ref[...], ln2b_ref[...])

    @pl.when((t == tiles_per_seq - 1) & (i < n_tiles))
    def _():
        for h in range(HG_HEADS):
            s_out_ref[0, h] = st_s[h].T
        cbuf_out_ref[0] = ubuf[CONV_PAD - CV_BUF:CONV_PAD, :]


def _const_spec(shape):
    zeros = (0,) * len(shape)
    return pl.BlockSpec(shape, lambda *_: zeros, pipeline_mode=pl.Buffered(1))


def _prompt_layer(x, mod, lb_logits, w_in, b_in, hgw, conv_w, conv_b, gn_g, gn_b, gmat,
                  w_out, b_out, ln1_g, ln1_b, w_up, b_up, w_down, b_down, ln2_g, ln2_b, *, tile):
    batch, seq, _ = x.shape
    tiles_per_seq = seq // tile
    n_tiles = batch * tiles_per_seq
    kernel = functools.partial(_prompt_kernel, tile=tile, tiles_per_seq=tiles_per_seq,
                               n_tiles=n_tiles)

    def mixer_block(i):
        m = jnp.minimum(i, n_tiles - 1)
        return m // tiles_per_seq, m % tiles_per_seq

    def mlp_block(i):
        m = jnp.maximum(i - 1, 0)
        return m // tiles_per_seq, m % tiles_per_seq

    consts = (mod, lb_logits, w_in, b_in, hgw, conv_w, conv_b, gn_g, gn_b, gmat, w_out, b_out,
              ln1_g, ln1_b, w_up, b_up, w_down, b_down, ln2_g, ln2_b)
    return pl.pallas_call(
        kernel,
        grid=(n_tiles + 1,),
        in_specs=[pl.BlockSpec((1, tile, D_MODEL), lambda i: (*mixer_block(i), 0))]
        + [_const_spec(a.shape) for a in consts],
        out_specs=[
            pl.BlockSpec((1, tile, D_MODEL), lambda i: (*mlp_block(i), 0)),
            pl.BlockSpec((1, HG_HEADS, HG_DK, HG_DK), lambda i: (mixer_block(i)[0], 0, 0, 0)),
            pl.BlockSpec((1, CV_BUF, CV_WIDTH), lambda i: (mixer_block(i)[0], 0, 0)),
        ],
        out_shape=[
            jax.ShapeDtypeStruct((batch, seq, D_MODEL), F32),
            jax.ShapeDtypeStruct((batch, HG_HEADS, HG_DK, HG_DK), F32),
            jax.ShapeDtypeStruct((batch, CV_BUF, CV_WIDTH), F32),
        ],
        scratch_shapes=[
            pltpu.VMEM((tile, HG_WIDTH), F32),
            pltpu.VMEM((tile, HG_WIDTH), F32),
            pltpu.VMEM((tile, HG_WIDTH), F32),
            pltpu.VMEM((tile, HG_WIDTH), F32),
            pltpu.VMEM((tile, HG_WIDTH), F32),
            pltpu.VMEM((CONV_PAD + tile, CV_WIDTH), F32),
            pltpu.VMEM((SUBLANES - 1, CONV_PAD + tile - SUBLANES, CV_WIDTH), F32),
            pltpu.VMEM((tile, CV_WIDTH), F32),
            pltpu.VMEM((tile, D_MODEL), BF16),
            pltpu.VMEM((HG_HEADS, HG_DK, HG_DK), F32),
            pltpu.VMEM((tile, D_MODEL), F32),
            pltpu.VMEM((tile, D_MODEL), BF16),
            pltpu.VMEM((tile, D_MODEL), F32),
            pltpu.VMEM((tile, D_FF), BF16),
            pltpu.VMEM((tile, D_MODEL), F32),
        ],
        compiler_params=pltpu.CompilerParams(
            dimension_semantics=("arbitrary",), vmem_limit_bytes=VMEM_LIMIT_BYTES),
        name="prompt_layer",
    )(x, *consts)


def _mixer_sample_kernel(x_ref, mod_ref, s_in_ref, cbuf_in_ref, lb_ref, w_in_ref, b_in_ref, hgw_ref,
                         cw_ref, cb_ref, gng_ref, gnb_ref, gmat_ref, w_out_ref, b_out_ref,
                         lng_ref, lnb_ref,
                         x1_ref, s_out_ref, cbuf_out_ref,
                         qe_s, ke_s, kd_s, v_s, d_s, o_s, *, seqs):
    rows_n = seqs * DEC_SEQ
    x = x_ref[...]
    sh1 = mod_ref[:, 0:D_MODEL]
    sc1 = mod_ref[:, D_MODEL:2 * D_MODEL]
    g1 = mod_ref[:, 2 * D_MODEL:3 * D_MODEL]
    hb = (x * (1.0 + sc1) + sh1).astype(BF16).reshape(rows_n, D_MODEL)
    q, k, lf, v, gate, u = _in_projection(hb, _lower_bound(lb_ref), w_in_ref, b_in_ref)

    slab = lambda a, t: a[t * seqs:(t + 1) * seqs, :]
    b = [slab(lf, 0)]
    for t in range(1, DEC_SEQ):
        b.append(b[-1] + slab(lf, t))
    b_last = b[-1]
    parts = _split3(jnp.exp(b_last)) + (jnp.zeros_like(b_last),)

    def put(ref, t, val):
        for h in range(HG_HEADS):
            ref[h, t * seqs:(t + 1) * seqs, :] = val[:, h * HG_DK:(h + 1) * HG_DK]

    for t in range(DEC_SEQ):
        put(qe_s, t, slab(q, t) * jnp.exp(b[t]))
        put(ke_s, t, slab(k, t) * jnp.exp(-b[t]))
        put(kd_s, t, slab(k, t) * jnp.exp(b_last - b[t]))
        put(d_s, t, parts[t])
        put(v_s, t, slab(v, t))

    stride = seqs // SEQ_GROUP
    n_rows = SEQ_GROUP * DEC_SEQ
    member = lax.broadcasted_iota(jnp.int32, (n_rows, HG_DK), 0) % SEQ_GROUP
    lanes = lambda g: slice(g * HG_DK, (g + 1) * HG_DK)
    per_member = lambda a: jnp.concatenate(
        [jnp.where(member == g, a, 0.0) for g in range(SEQ_GROUP)], axis=1)
    ones_g = per_member(jnp.ones((n_rows, HG_DK), F32)).astype(BF16)
    ri = lax.broadcasted_iota(jnp.int32, (n_rows, n_rows), 0)
    ci = lax.broadcasted_iota(jnp.int32, (n_rows, n_rows), 1)
    causal = (ri // SEQ_GROUP >= ci // SEQ_GROUP) & (ri % SEQ_GROUP == ci % SEQ_GROUP)
    for s in range(stride):
        rows = pl.ds(s, n_rows, stride=stride)
        for h in range(HG_HEADS):
            qe_g = qe_s[h, rows, :].astype(BF16)
            ke_g = ke_s[h, rows, :].astype(BF16)
            kd_g = kd_s[h, rows, :].astype(BF16)
            d_g = d_s[h, rows, :].astype(BF16)
            v_g = v_s[h, rows, :]
            s_g = jnp.concatenate([s_in_ref[s + g * stride, h] for g in range(SEQ_GROUP)], axis=1)
            att = jnp.where(causal, _dot_nt(qe_g, ke_g), 0.0).astype(BF16)
            inter_all = _dot(qe_g, s_g.astype(BF16))
            inter = inter_all[:, lanes(0)]
            for g in range(1, SEQ_GROUP):
                inter = jnp.where(member == g, inter_all[:, lanes(g)], inter)
            o_s[h, rows, :] = _dot(att, v_g.astype(BF16)) + inter
            new = _dot_tn(d_g, ones_g) * s_g + _dot_tn(kd_g, per_member(v_g).astype(BF16))
            for g in range(SEQ_GROUP):
                s_out_ref[s + g * stride, h] = new[:, lanes(g)]
    hgw = hgw_ref[...]
    o_a = jnp.concatenate(
        [_head_norm(o_s[h], hgw, gate[:, h * HG_DK:(h + 1) * HG_DK]) for h in range(HG_HEADS)],
        axis=-1)

    full = lambda i: cbuf_in_ref[i] if i < CV_BUF else slab(u, i - CV_BUF)
    acc = [jnp.broadcast_to(cb_ref[...], (seqs, CV_WIDTH)) for _ in range(DEC_SEQ)]
    for i in range(CV_BUF + DEC_SEQ):
        f_i = full(i)
        for t in range(DEC_SEQ):
            if 0 <= i - t < CV_KERNEL:
                acc[t] = acc[t] + cw_ref[i - t:i - t + 1, :] * f_i
    for i in range(CV_BUF):
        cbuf_out_ref[i] = full(i + DEC_SEQ)
    o_b = _group_norm_swish(jnp.concatenate(acc, axis=0), gmat_ref, gng_ref[...], gnb_ref[...])

    mix = _out_projection(jnp.concatenate([o_a, o_b], axis=-1).astype(BF16), w_out_ref, b_out_ref)
    y = ALPHA * x + (1.0 + g1) * mix.reshape(DEC_SEQ, seqs, D_MODEL)
    x1_ref[...] = _layer_norm(y, lng_ref[...], lnb_ref[...])


def _mixer_sample(x, mod, s_in, cbuf_in, lb_logits, w_in, b_in, hgw, conv_w, conv_b, gn_g, gn_b,
                  gmat, w_out, b_out, ln_g, ln_b, *, seqs):
    n_seq = s_in.shape[0]
    rows = seqs * DEC_SEQ
    kernel = functools.partial(_mixer_sample_kernel, seqs=seqs)
    return pl.pallas_call(
        kernel,
        grid=(n_seq // seqs,),
        in_specs=[
            pl.BlockSpec((DEC_SEQ, seqs, D_MODEL), lambda i: (0, i, 0)),
            pl.BlockSpec((seqs, 3 * D_MODEL), lambda i: (i, 0)),
            pl.BlockSpec((seqs, HG_HEADS, HG_DK, HG_DK), lambda i: (i, 0, 0, 0)),
            pl.BlockSpec((CV_BUF, seqs, CV_WIDTH), lambda i: (0, i, 0)),
            _const_spec(lb_logits.shape),
            _const_spec(w_in.shape),
            _const_spec(b_in.shape),
            _const_spec(hgw.shape),
            _const_spec(conv_w.shape),
            _const_spec(conv_b.shape),
            _const_spec(gn_g.shape),
            _const_spec(gn_b.shape),
            _const_spec(gmat.shape),
            _const_spec(w_out.shape),
            _const_spec(b_out.shape),
            _const_spec(ln_g.shape),
            _const_spec(ln_b.shape),
        ],
        out_specs=[
            pl.BlockSpec((DEC_SEQ, seqs, D_MODEL), lambda i: (0, i, 0)),
            pl.BlockSpec((seqs, HG_HEADS, HG_DK, HG_DK), lambda i: (i, 0, 0, 0)),
            pl.BlockSpec((CV_BUF, seqs, CV_WIDTH), lambda i: (0, i, 0)),
        ],
        out_shape=[
            jax.ShapeDtypeStruct((DEC_SEQ, n_seq, D_MODEL), F32),
            jax.ShapeDtypeStruct(s_in.shape, F32),
            jax.ShapeDtypeStruct(cbuf_in.shape, F32),
        ],
        scratch_shapes=[pltpu.VMEM((HG_HEADS, rows, HG_DK), F32)] * 6,
        compiler_params=pltpu.CompilerParams(
            dimension_semantics=("arbitrary",), vmem_limit_bytes=VMEM_LIMIT_BYTES),
        name="mixer_sample",
    )(x, mod, s_in, cbuf_in, lb_logits, w_in, b_in, hgw, conv_w, conv_b, gn_g, gn_b, gmat,
      w_out, b_out, ln_g, ln_b)


def _mlp_kernel(x_ref, sh_ref, sc_ref, g_ref, w_up_ref, b_up_ref, w_down_ref, b_down_ref,
                lng_ref, lnb_ref, o_ref):
    x = x_ref[...]
    hb = (x * (1.0 + sc_ref[...]) + sh_ref[...]).astype(BF16).reshape(-1, D_MODEL)
    ff = jnp.broadcast_to(b_down_ref[...], hb.shape)
    for c0 in range(0, D_FF, FF_CHUNK):
        cs = slice(c0, c0 + FF_CHUNK)
        a = jnp.maximum(_dot(hb, w_up_ref[:, cs]) + b_up_ref[:, cs], 0.0)
        ff = ff + _dot((a * a).astype(BF16), w_down_ref[cs, :])
    y = ALPHA * x + (1.0 + g_ref[...]) * ff.reshape(x.shape)
    o_ref[...] = _layer_norm(y, lng_ref[...], lnb_ref[...])


def _mlp(x, x_spec, grid, mod_specs, mods, w_up, b_up, w_down, b_down, ln_g, ln_b, *, name):
    return pl.pallas_call(
        _mlp_kernel,
        grid=grid,
        in_specs=[x_spec] + mod_specs + [
            _const_spec(w_up.shape),
            _const_spec(b_up.shape),
            _const_spec(w_down.shape),
            _const_spec(b_down.shape),
            _const_spec(ln_g.shape),
            _const_spec(ln_b.shape),
        ],
        out_specs=x_spec,
        out_shape=jax.ShapeDtypeStruct(x.shape, F32),
        compiler_params=pltpu.CompilerParams(
            dimension_semantics=("arbitrary",), vmem_limit_bytes=VMEM_LIMIT_BYTES),
        name=name,
    )(x, *mods, w_up, b_up, w_down, b_down, ln_g, ln_b)


PROMPT_TILE = 256
SAMPLE_SEQS = 16


def kernel(x_prompt, x_sample, c_prompt, c_sample, state_hgrn, state_conv, lb_logits, w_in, b_in,
           hg_norm_w, conv_w, conv_b, gn_g, gn_b, w_out, b_out, ln1_g, ln1_b, w_up, b_up, w_down,
           b_down, ln2_g, ln2_b, w_ada, b_ada):
    assert w_in.shape[0] == DEPTH
    batch, seq, _ = x_prompt.shape
    dec_batch, dec_seq, _ = x_sample.shape
    assert dec_seq == DEC_SEQ

    group = jnp.arange(MXU_WIDTH, dtype=jnp.int32) // (CV_WIDTH // CV_GROUPS)
    gmat = jnp.where(group[:, None] == group[None, :], CV_GROUPS / CV_WIDTH, 0.0).astype(BF16)

    xp = x_prompt
    xs = jnp.transpose(x_sample, (1, 0, 2))
    hp, cp, hs, cs = [], [], [], []
    for l in range(DEPTH):
        row = lambda a: a[l][None, :]
        w_in_b, w_out_b = _to_bf16(w_in[l]), _to_bf16(w_out[l])
        w_up_b, w_down_b = _to_bf16(w_up[l]), _to_bf16(w_down[l])

        mod = _adaln(jnp.concatenate([c_prompt, c_sample], axis=0), w_ada[l], row(b_ada))
        mod_p = mod[:batch].reshape(batch, 6, D_MODEL)
        mod_s = mod[batch:]

        mixer_params = (lb_logits, w_in_b, row(b_in), row(hg_norm_w), conv_w[l], row(conv_b),
                        row(gn_g), row(gn_b), gmat, w_out_b, row(b_out), row(ln1_g), row(ln1_b))
        mlp_params = (w_up_b, row(b_up), w_down_b, row(b_down), row(ln2_g), row(ln2_b))
        xp, sp, bp = _prompt_layer(xp, mod_p, *mixer_params, *mlp_params, tile=PROMPT_TILE)
        xs, ss, bs = _mixer_sample(xs, mod_s, state_hgrn[l], jnp.transpose(state_conv[l], (1, 0, 2)),
                                   *mixer_params, seqs=SAMPLE_SEQS)
        s_specs = [pl.BlockSpec((dec_batch, D_MODEL), functools.partial(lambda i, r: (0, r), r=r))
                   for r in (3, 4, 5)]
        xs = _mlp(xs, pl.BlockSpec((DEC_SEQ, dec_batch, D_MODEL), lambda i: (0, 0, 0)), (1,),
                  s_specs, (mod_s,) * 3, *mlp_params, name="mlp_sample")

        hp.append(sp); cp.append(bp); hs.append(ss); cs.append(jnp.transpose(bs, (1, 0, 2)))

    return (xp, jnp.transpose(xs, (1, 0, 2)), jnp.stack(hp), jnp.stack(cp), jnp.stack(hs),
            jnp.stack(cs))
```

```python
import functools

import jax
import jax.numpy as jnp
from jax import lax
from jax.experimental import pallas as pl
from jax.experimental.pallas import tpu as pltpu

F32 = jnp.float32
BF16 = jnp.bfloat16

D_MODEL = 1024
HG_WIDTH = 512
CV_WIDTH = 512
HG_HEADS = 4
HG_DK = 128
CV_GROUPS = 8
CV_KERNEL = 31
CV_BUF = CV_KERNEL - 1
D_FF = 4 * D_MODEL
N_IN = 4 * HG_WIDTH + 2 * CV_WIDTH
DEPTH = 1
ALPHA = (2.0 * DEPTH) ** 0.25
EPS = 1e-5

CHUNK = 64
SUBLANES = 8
MXU_WIDTH = 256
CONV_PAD = 32
CONV_ROWS = 128
UP_CHUNK = 512
DOWN_CHUNK = 256
FF_CHUNK = 1024
DEC_SEQ = 4
SEQ_GROUP = 4

VMEM_LIMIT_BYTES = 52 * 1024 * 1024
PROMPT_VMEM_LIMIT_BYTES = 60 * 1024 * 1024
CAST_BLOCK_BYTES = 4 * 1024 * 1024
ADALN_COLS = 1024

_NT = (((1,), (1,)), ((), ()))
_TN = (((0,), (0,)), ((), ()))


def _dot(a, b):
    return jnp.dot(a, b, preferred_element_type=F32)


def _dot_nt(a, b):
    return lax.dot_general(a, b, _NT, preferred_element_type=F32)


def _dot_tn(a, b):
    return lax.dot_general(a, b, _TN, preferred_element_type=F32)


def _silu(x):
    return x * jax.nn.sigmoid(x)


def _split3(x):
    hi = x.astype(BF16).astype(F32)
    r = x - hi
    mid = r.astype(BF16).astype(F32)
    lo = (r - mid).astype(BF16).astype(F32)
    return hi, mid, lo


def _layer_norm(y, g, b):
    mu = jnp.mean(y, axis=-1, keepdims=True)
    yc = y - mu
    var = jnp.mean(yc * yc, axis=-1, keepdims=True)
    return yc * lax.rsqrt(var + EPS) * g + b


def _adaln_kernel(c_ref, w_ref, b_ref, o_ref):
    a = _silu(c_ref[...])
    a_hi = a.astype(BF16)
    a_lo = (a - a_hi.astype(F32)).astype(BF16)
    w = w_ref[...].astype(BF16)
    o_ref[...] = _dot(a_hi, w) + _dot(a_lo, w) + b_ref[...]


def _adaln(c_all, w_ada, b_ada):
    rows = c_all.shape[0]
    n = w_ada.shape[1]
    tn = ADALN_COLS
    return pl.pallas_call(
        _adaln_kernel,
        grid=(n // tn,),
        in_specs=[
            pl.BlockSpec((rows, D_MODEL), lambda j: (0, 0)),
            pl.BlockSpec((D_MODEL, tn), lambda j: (0, j)),
            pl.BlockSpec((1, tn), lambda j: (0, j)),
        ],
        out_specs=pl.BlockSpec((rows, tn), lambda j: (0, j)),
        out_shape=jax.ShapeDtypeStruct((rows, n), F32),
        compiler_params=pltpu.CompilerParams(
            dimension_semantics=("arbitrary",), vmem_limit_bytes=VMEM_LIMIT_BYTES),
        name="adaln",
    )(c_all, w_ada, b_ada)


def _cast_kernel(w_ref, o_ref):
    o_ref[...] = w_ref[...].astype(o_ref.dtype)


def _to_bf16(w):
    rows, cols = w.shape
    block_rows = rows
    while block_rows * cols * w.dtype.itemsize > CAST_BLOCK_BYTES and block_rows % 16 == 0:
        block_rows //= 2
    spec = pl.BlockSpec((block_rows, cols), lambda i: (i, 0))
    return pl.pallas_call(
        _cast_kernel,
        grid=(rows // block_rows,),
        in_specs=[spec],
        out_specs=spec,
        out_shape=jax.ShapeDtypeStruct(w.shape, BF16),
        compiler_params=pltpu.CompilerParams(
            dimension_semantics=("arbitrary",), vmem_limit_bytes=VMEM_LIMIT_BYTES),
        name="to_bf16",
    )(w)


def _lower_bound(lb_ref):
    lg = lb_ref[...]
    e = jnp.exp(lg - jnp.max(lg, axis=0, keepdims=True))
    p = e / jnp.sum(e, axis=0, keepdims=True)
    return jnp.sum(p[0:DEPTH, :], axis=0, keepdims=True)


def _section(hb, w_in_ref, b_in_ref, i, width):
    return _dot(hb, w_in_ref[:, i:i + width]) + b_in_ref[:, i:i + width]


def _in_projection(hb, lb, w_in_ref, b_in_ref):
    sec = functools.partial(_section, hb, w_in_ref, b_in_ref)

    q = _silu(sec(0, HG_WIDTH))
    f = lb + (1.0 - lb) * jax.nn.sigmoid(sec(HG_WIDTH, HG_WIDTH))
    v = sec(2 * HG_WIDTH, HG_WIDTH)
    gate = _silu(sec(3 * HG_WIDTH, HG_WIDTH))
    za = sec(4 * HG_WIDTH, CV_WIDTH)
    zb = sec(4 * HG_WIDTH + CV_WIDTH, CV_WIDTH)
    return q, 1.0 - f, jnp.log(f), v, gate, za * jax.nn.sigmoid(zb)


def _head_norm(o, hgw, gate):
    ms = jnp.mean(o * o, axis=-1, keepdims=True)
    return o * lax.rsqrt(ms + EPS) * hgw * gate


def _group_norm_swish(uc, gmat_ref, gn_g, gn_b):
    width = gmat_ref.shape[0]

    def group_mean(a):
        ab = a.astype(BF16)
        return jnp.concatenate([_dot(ab[:, c0:c0 + width], gmat_ref[...])
                                for c0 in range(0, a.shape[1], width)], axis=-1)

    d = uc - group_mean(uc)
    var = group_mean(d * d)
    return _silu(d * lax.rsqrt(var + EPS) * gn_g + gn_b)


def _out_projection(mix_bf16, w_out_ref, b_out_ref):
    return _dot(mix_bf16, w_out_ref[...]) + b_out_ref[...]


def _prompt_kernel(x_ref, mod_ref, lb_ref, w_in_ref, b_in_ref, hgw_ref, cw_ref, cb_ref,
                   gng_ref, gnb_ref, gmat_ref, w_out_ref, b_out_ref, ln1g_ref, ln1b_ref,
                   w_up_ref, b_up_ref, w_down_ref, b_down_ref, ln2g_ref, ln2b_ref,
                   y_ref, s_out_ref, cbuf_out_ref,
                   q_s, k_s, lf_s, v_s, g_s, ubuf, ush, uc_s, mix_s, st_s,
                   x1_s, hb2_s, res2_s, act_s, ff_s, *, tile, tiles_per_seq, n_tiles):
    i = pl.program_id(0)
    mixer_tile = jnp.minimum(i, n_tiles - 1)
    b_mix = mixer_tile // tiles_per_seq
    t = mixer_tile % tiles_per_seq
    b_mlp = jnp.maximum(i - 1, 0) // tiles_per_seq

    @pl.when(i == 0)
    def _():
        x1_s[...] = jnp.zeros_like(x1_s)

    @pl.when(t == 0)
    def _():
        st_s[...] = jnp.zeros_like(st_s)
        ubuf[0:CONV_PAD, :] = jnp.zeros((CONV_PAD, CV_WIDTH), F32)

    x1_prev = x1_s[...]
    sh2 = mod_ref[b_mlp, 3:4, :]
    sc2 = mod_ref[b_mlp, 4:5, :]
    g2 = mod_ref[b_mlp, 5:6, :]
    hb2_s[...] = (x1_prev * (1.0 + sc2) + sh2).astype(BF16)
    res2_s[...] = ALPHA * x1_prev

    def mlp_up(c0):
        cs = slice(c0, c0 + UP_CHUNK)
        a = jnp.maximum(_dot(hb2_s[...], w_up_ref[:, cs]) + b_up_ref[:, cs], 0.0)
        act_s[:, cs] = (a * a).astype(BF16)

    def mlp_down(c0):
        cs = slice(c0, c0 + DOWN_CHUNK)
        ff_s[:, cs] = _dot(act_s[...], w_down_ref[:, cs]) + b_down_ref[:, cs]

    x = x_ref[0]
    sh1 = mod_ref[b_mix, 0:1, :]
    sc1 = mod_ref[b_mix, 1:2, :]
    g1 = mod_ref[b_mix, 2:3, :]
    hb = (x * (1.0 + sc1) + sh1).astype(BF16)
    sec = functools.partial(_section, hb, w_in_ref, b_in_ref)
    lb = _lower_bound(lb_ref)

    up_chunks = list(range(0, D_FF, UP_CHUNK))
    down_chunks = list(range(0, D_MODEL, DOWN_CHUNK))

    ubuf[CONV_PAD:CONV_PAD + tile, :] = (sec(4 * HG_WIDTH, CV_WIDTH)
                                         * jax.nn.sigmoid(sec(4 * HG_WIDTH + CV_WIDTH, CV_WIDTH)))
    first_off = CONV_PAD - CV_BUF
    shifted_rows = ush.shape[1]

    def conv_piece(r0, cs):
        acc = jnp.broadcast_to(cb_ref[:, cs], (CONV_ROWS, 128))
        for phase in range(SUBLANES):
            taps = [j for j in range(CV_KERNEL) if (first_off + j) % SUBLANES == phase]
            span = (first_off + taps[-1]) // SUBLANES * SUBLANES
            rows = slice(r0, r0 + span + CONV_ROWS)
            win = ubuf[rows, cs] if phase == 0 else ush[phase - 1, rows, :]
            for j in taps:
                a0 = first_off + j - phase
                acc = acc + cw_ref[j:j + 1, cs] * win[a0:a0 + CONV_ROWS, :]
        uc_s[r0:r0 + CONV_ROWS, cs] = acc

    def conv_pieces(n):
        cs = slice(n * 128, (n + 1) * 128)
        for phase in range(1, SUBLANES):
            ush[phase - 1] = ubuf[phase:phase + shifted_rows, cs]
        for r0 in range(0, tile, CONV_ROWS):
            conv_piece(r0, cs)

    mlp_up(up_chunks.pop(0))
    q_s[...] = _silu(sec(0, HG_WIDTH))
    conv_pieces(0)
    mlp_up(up_chunks.pop(0))
    f = lb + (1.0 - lb) * jax.nn.sigmoid(sec(HG_WIDTH, HG_WIDTH))
    k_s[...] = 1.0 - f
    lf_s[...] = jnp.log(f)
    conv_pieces(1)
    mlp_up(up_chunks.pop(0))
    v_s[...] = sec(2 * HG_WIDTH, HG_WIDTH).astype(BF16)
    conv_pieces(2)
    mlp_up(up_chunks.pop(0))
    g_s[...] = _silu(sec(3 * HG_WIDTH, HG_WIDTH))
    conv_pieces(3)

    row = lax.broadcasted_iota(jnp.int32, (CHUNK, HG_WIDTH), 0)
    causal = (lax.broadcasted_iota(jnp.int32, (CHUNK, CHUNK), 0)
              >= lax.broadcasted_iota(jnp.int32, (CHUNK, CHUNK), 1))
    hgw = hgw_ref[...]
    for c in range(tile // CHUNK):
        if up_chunks:
            mlp_up(up_chunks.pop(0))
        rows = slice(c * CHUNK, (c + 1) * CHUNK)
        b = lf_s[rows, :]
        shift = 1
        while shift < CHUNK:
            b = b + jnp.where(row >= shift, pltpu.roll(b, shift, axis=0), 0.0)
            shift *= 2
        b_last = b[CHUNK - 1:CHUNK, :]
        kc = k_s[rows, :]
        qe = (q_s[rows, :] * jnp.exp(b)).astype(BF16)
        ke = (kc * jnp.exp(-b)).astype(BF16)
        kd = (kc * jnp.exp(b_last - b)).astype(BF16)
        decay = jnp.exp(b_last)
        vb = v_s[rows, :]
        for h in range(HG_HEADS):
            hs = slice(h * HG_DK, (h + 1) * HG_DK)
            att = jnp.where(causal, _dot_nt(qe[:, hs], ke[:, hs]), 0.0).astype(BF16)
            st = st_s[h]
            o = _dot(att, vb[:, hs]) + _dot_nt(qe[:, hs], st.astype(BF16))
            st_s[h] = st * decay[:, hs] + _dot_tn(vb[:, hs], kd[:, hs])
            mix_s[rows, hs] = _head_norm(o, hgw, g_s[rows, hs]).astype(BF16)
    while up_chunks:
        mlp_up(up_chunks.pop(0))

    mlp_down(down_chunks.pop(0))
    mix_s[:, HG_WIDTH:] = _group_norm_swish(
        uc_s[...], gmat_ref, gng_ref[...], gnb_ref[...]).astype(BF16)
    ubuf[0:CONV_PAD, :] = ubuf[tile:tile + CONV_PAD, :]
    mlp_down(down_chunks.pop(0))
    mix = _out_projection(mix_s[...], w_out_ref, b_out_ref)
    x1_s[...] = _layer_norm(ALPHA * x + (1.0 + g1) * mix, ln1g_ref[...], ln1b_ref[...])
    while down_chunks:
        mlp_down(down_chunks.pop(0))
    y_ref[0] = _layer_norm(res2_s[...] + (1.0 + g2) * ff_s[...], ln2g_ref[...], ln2b_ref[...])

    @pl.when((t == tiles_per_seq - 1) & (i < n_tiles))
    def _():
        for h in range(HG_HEADS):
            s_out_ref[0, h] = st_s[h].T
        cbuf_out_ref[0] = ubuf[CONV_PAD - CV_BUF:CONV_PAD, :]


def _const_spec(shape):
    zeros = (0,) * len(shape)
    return pl.BlockSpec(shape, lambda *_: zeros, pipeline_mode=pl.Buffered(1))


def _prompt_layer(x, mod, lb_logits, w_in, b_in, hgw, conv_w, conv_b, gn_g, gn_b, gmat,
                  w_out, b_out, ln1_g, ln1_b, w_up, b_up, w_down, b_down, ln2_g, ln2_b, *, tile):
    batch, seq, _ = x.shape
    tiles_per_seq = seq // tile
    n_tiles = batch * tiles_per_seq
    kernel = functools.partial(_prompt_kernel, tile=tile, tiles_per_seq=tiles_per_seq,
                               n_tiles=n_tiles)

    def mixer_block(i):
        m = jnp.minimum(i, n_tiles - 1)
        return m // tiles_per_seq, m % tiles_per_seq

    def mlp_block(i):
        m = jnp.maximum(i - 1, 0)
        return m // tiles_per_seq, m % tiles_per_seq

    consts = (mod, lb_logits, w_in, b_in, hgw, conv_w, conv_b, gn_g, gn_b, gmat, w_out, b_out,
              ln1_g, ln1_b, w_up, b_up, w_down, b_down, ln2_g, ln2_b)
    return pl.pallas_call(
        kernel,
        grid=(n_tiles + 1,),
        in_specs=[pl.BlockSpec((1, tile, D_MODEL), lambda i: (*mixer_block(i), 0))]
        + [_const_spec(a.shape) for a in consts],
        out_specs=[
            pl.BlockSpec((1, tile, D_MODEL), lambda i: (*mlp_block(i), 0)),
            pl.BlockSpec((1, HG_HEADS, HG_DK, HG_DK), lambda i: (mixer_block(i)[0], 0, 0, 0)),
            pl.BlockSpec((1, CV_BUF, CV_WIDTH), lambda i: (mixer_block(i)[0], 0, 0)),
        ],
        out_shape=[
            jax.ShapeDtypeStruct((batch, seq, D_MODEL), F32),
            jax.ShapeDtypeStruct((batch, HG_HEADS, HG_DK, HG_DK), F32),
            jax.ShapeDtypeStruct((batch, CV_BUF, CV_WIDTH), F32),
        ],
        scratch_shapes=[
            pltpu.VMEM((tile, HG_WIDTH), F32),
            pltpu.VMEM((tile, HG_WIDTH), F32),
            pltpu.VMEM((tile, HG_WIDTH), F32),
            pltpu.VMEM((tile, HG_WIDTH), BF16),
            pltpu.VMEM((tile, HG_WIDTH), F32),
            pltpu.VMEM((CONV_PAD + tile, CV_WIDTH), F32),
            pltpu.VMEM((SUBLANES - 1, CONV_PAD + tile - SUBLANES, 128), F32),
            pltpu.VMEM((tile, CV_WIDTH), F32),
            pltpu.VMEM((tile, D_MODEL), BF16),
            pltpu.VMEM((HG_HEADS, HG_DK, HG_DK), F32),
            pltpu.VMEM((tile, D_MODEL), F32),
            pltpu.VMEM((tile, D_MODEL), BF16),
            pltpu.VMEM((tile, D_MODEL), F32),
            pltpu.VMEM((tile, D_FF), BF16),
            pltpu.VMEM((tile, D_MODEL), F32),
        ],
        compiler_params=pltpu.CompilerParams(
            dimension_semantics=("arbitrary",), vmem_limit_bytes=PROMPT_VMEM_LIMIT_BYTES),
        name="prompt_layer",
    )(x, *consts)


def _mixer_sample_kernel(x_ref, mod_ref, s_in_ref, cbuf_in_ref, lb_ref, w_in_ref, b_in_ref, hgw_ref,
                         cw_ref, cb_ref, gng_ref, gnb_ref, gmat_ref, w_out_ref, b_out_ref,
                         lng_ref, lnb_ref,
                         x1_ref, s_out_ref, cbuf_out_ref,
                         qe_s, ke_s, kd_s, v_s, d_s, o_s, *, seqs):
    rows_n = seqs * DEC_SEQ
    x = x_ref[...]
    sh1 = mod_ref[:, 0:D_MODEL]
    sc1 = mod_ref[:, D_MODEL:2 * D_MODEL]
    g1 = mod_ref[:, 2 * D_MODEL:3 * D_MODEL]
    hb = (x * (1.0 + sc1) + sh1).astype(BF16).reshape(rows_n, D_MODEL)
    q, k, lf, v, gate, u = _in_projection(hb, _lower_bound(lb_ref), w_in_ref, b_in_ref)

    slab = lambda a, t: a[t * seqs:(t + 1) * seqs, :]
    b = [slab(lf, 0)]
    for t in range(1, DEC_SEQ):
        b.append(b[-1] + slab(lf, t))
    b_last = b[-1]
    parts = _split3(jnp.exp(b_last)) + (jnp.zeros_like(b_last),)

    def put(ref, t, val):
        for h in range(HG_HEADS):
            ref[h, t * seqs:(t + 1) * seqs, :] = val[:, h * HG_DK:(h + 1) * HG_DK]

    for t in range(DEC_SEQ):
        put(qe_s, t, slab(q, t) * jnp.exp(b[t]))
        put(ke_s, t, slab(k, t) * jnp.exp(-b[t]))
        put(kd_s, t, slab(k, t) * jnp.exp(b_last - b[t]))
        put(d_s, t, parts[t])
        put(v_s, t, slab(v, t))

    stride = seqs // SEQ_GROUP
    n_rows = SEQ_GROUP * DEC_SEQ
    member = lax.broadcasted_iota(jnp.int32, (n_rows, HG_DK), 0) % SEQ_GROUP
    lanes = lambda g: slice(g * HG_DK, (g + 1) * HG_DK)
    per_member = lambda a: jnp.concatenate(
        [jnp.where(member == g, a, 0.0) for g in range(SEQ_GROUP)], axis=1)
    ones_g = per_member(jnp.ones((n_rows, HG_DK), F32)).astype(BF16)
    ri = lax.broadcasted_iota(jnp.int32, (n_rows, n_rows), 0)
    ci = lax.broadcasted_iota(jnp.int32, (n_rows, n_rows), 1)
    causal = (ri // SEQ_GROUP >= ci // SEQ_GROUP) & (ri % SEQ_GROUP == ci % SEQ_GROUP)
    for s in range(stride):
        rows = pl.ds(s, n_rows, stride=stride)
        for h in range(HG_HEADS):
            qe_g = qe_s[h, rows, :].astype(BF16)
            ke_g = ke_s[h, rows, :].astype(BF16)
            kd_g = kd_s[h, rows, :].astype(BF16)
            d_g = d_s[h, rows, :].astype(BF16)
            v_g = v_s[h, rows, :]
            s_g = jnp.concatenate([s_in_ref[s + g * stride, h] for g in range(SEQ_GROUP)], axis=1)
            att = jnp.where(causal, _dot_nt(qe_g, ke_g), 0.0).astype(BF16)
            inter_all = _dot(qe_g, s_g.astype(BF16))
            inter = inter_all[:, lanes(0)]
            for g in range(1, SEQ_GROUP):
                inter = jnp.where(member == g, inter_all[:, lanes(g)], inter)
            o_s[h, rows, :] = _dot(att, v_g.astype(BF16)) + inter
            new = _dot_tn(d_g, ones_g) * s_g + _dot_tn(kd_g, per_member(v_g).astype(BF16))
            for g in range(SEQ_GROUP):
                s_out_ref[s + g * stride, h] = new[:, lanes(g)]
    hgw = hgw_ref[...]
    o_a = jnp.concatenate(
        [_head_norm(o_s[h], hgw, gate[:, h * HG_DK:(h + 1) * HG_DK]) for h in range(HG_HEADS)],
        axis=-1)

    full = lambda i: cbuf_in_ref[i] if i < CV_BUF else slab(u, i - CV_BUF)
    acc = [jnp.broadcast_to(cb_ref[...], (seqs, CV_WIDTH)) for _ in range(DEC_SEQ)]
    for i in range(CV_BUF + DEC_SEQ):
        f_i = full(i)
        for t in range(DEC_SEQ):
            if 0 <= i - t < CV_KERNEL:
                acc[t] = acc[t] + cw_ref[i - t:i - t + 1, :] * f_i
    for i in range(CV_BUF):
        cbuf_out_ref[i] = full(i + DEC_SEQ)
    o_b = _group_norm_swish(jnp.concatenate(acc, axis=0), gmat_ref, gng_ref[...], gnb_ref[...])

    mix = _out_projection(jnp.concatenate([o_a, o_b], axis=-1).astype(BF16), w_out_ref, b_out_ref)
    y = ALPHA * x + (1.0 + g1) * mix.reshape(DEC_SEQ, seqs, D_MODEL)
    x1_ref[...] = _layer_norm(y, lng_ref[...], lnb_ref[...])


def _mixer_sample(x, mod, s_in, cbuf_in, lb_logits, w_in, b_in, hgw, conv_w, conv_b, gn_g, gn_b,
                  gmat, w_out, b_out, ln_g, ln_b, *, seqs):
    n_seq = s_in.shape[0]
    rows = seqs * DEC_SEQ
    kernel = functools.partial(_mixer_sample_kernel, seqs=seqs)
    return pl.pallas_call(
        kernel,
        grid=(n_seq // seqs,),
        in_specs=[
            pl.BlockSpec((DEC_SEQ, seqs, D_MODEL), lambda i: (0, i, 0)),
            pl.BlockSpec((seqs, 3 * D_MODEL), lambda i: (i, 0)),
            pl.BlockSpec((seqs, HG_HEADS, HG_DK, HG_DK), lambda i: (i, 0, 0, 0)),
            pl.BlockSpec((CV_BUF, seqs, CV_WIDTH), lambda i: (0, i, 0)),
            _const_spec(lb_logits.shape),
            _const_spec(w_in.shape),
            _const_spec(b_in.shape),
            _const_spec(hgw.shape),
            _const_spec(conv_w.shape),
            _const_spec(conv_b.shape),
            _const_spec(gn_g.shape),
            _const_spec(gn_b.shape),
            _const_spec(gmat.shape),
            _const_spec(w_out.shape),
            _const_spec(b_out.shape),
            _const_spec(ln_g.shape),
            _const_spec(ln_b.shape),
        ],
        out_specs=[
            pl.BlockSpec((DEC_SEQ, seqs, D_MODEL), lambda i: (0, i, 0)),
            pl.BlockSpec((seqs, HG_HEADS, HG_DK, HG_DK), lambda i: (i, 0, 0, 0)),
            pl.BlockSpec((CV_BUF, seqs, CV_WIDTH), lambda i: (0, i, 0)),
        ],
        out_shape=[
            jax.ShapeDtypeStruct((DEC_SEQ, n_seq, D_MODEL), F32),
            jax.ShapeDtypeStruct(s_in.shape, F32),
            jax.ShapeDtypeStruct(cbuf_in.shape, F32),
        ],
        scratch_shapes=[pltpu.VMEM((HG_HEADS, rows, HG_DK), F32)] * 6,
        compiler_params=pltpu.CompilerParams(
            dimension_semantics=("arbitrary",), vmem_limit_bytes=VMEM_LIMIT_BYTES),
        name="mixer_sample",
    )(x, mod, s_in, cbuf_in, lb_logits, w_in, b_in, hgw, conv_w, conv_b, gn_g, gn_b, gmat,
      w_out, b_out, ln_g, ln_b)


def _mlp_kernel(x_ref, sh_ref, sc_ref, g_ref, w_up_ref, b_up_ref, w_down_ref, b_down_ref,
                lng_ref, lnb_ref, o_ref):
    x = x_ref[...]
    hb = (x * (1.0 + sc_ref[...]) + sh_ref[...]).astype(BF16).reshape(-1, D_MODEL)
    ff = jnp.broadcast_to(b_down_ref[...], hb.shape)
    for c0 in range(0, D_FF, FF_CHUNK):
        cs = slice(c0, c0 + FF_CHUNK)
        a = jnp.maximum(_dot(hb, w_up_ref[:, cs]) + b_up_ref[:, cs], 0.0)
        ff = ff + _dot((a * a).astype(BF16), w_down_ref[cs, :])
    y = ALPHA * x + (1.0 + g_ref[...]) * ff.reshape(x.shape)
    o_ref[...] = _layer_norm(y, lng_ref[...], lnb_ref[...])


def _mlp(x, x_spec, grid, mod_specs, mods, w_up, b_up, w_down, b_down, ln_g, ln_b, *, name):
    return pl.pallas_call(
        _mlp_kernel,
        grid=grid,
        in_specs=[x_spec] + mod_specs + [
            _const_spec(w_up.shape),
            _const_spec(b_up.shape),
            _const_spec(w_down.shape),
            _const_spec(b_down.shape),
            _const_spec(ln_g.shape),
            _const_spec(ln_b.shape),
        ],
        out_specs=x_spec,
        out_shape=jax.ShapeDtypeStruct(x.shape, F32),
        compiler_params=pltpu.CompilerParams(
            dimension_semantics=("arbitrary",), vmem_limit_bytes=VMEM_LIMIT_BYTES),
        name=name,
    )(x, *mods, w_up, b_up, w_down, b_down, ln_g, ln_b)


PROMPT_TILE = 512
SAMPLE_SEQS = 16


def kernel(x_prompt, x_sample, c_prompt, c_sample, state_hgrn, state_conv, lb_logits, w_in, b_in,
           hg_norm_w, conv_w, conv_b, gn_g, gn_b, w_out, b_out, ln1_g, ln1_b, w_up, b_up, w_down,
           b_down, ln2_g, ln2_b, w_ada, b_ada):
    assert w_in.shape[0] == DEPTH
    batch, seq, _ = x_prompt.shape
    dec_batch, dec_seq, _ = x_sample.shape
    assert dec_seq == DEC_SEQ

    group = jnp.arange(MXU_WIDTH, dtype=jnp.int32) // (CV_WIDTH // CV_GROUPS)
    gmat = jnp.where(group[:, None] == group[None, :], CV_GROUPS / CV_WIDTH, 0.0).astype(BF16)

    xp = x_prompt
    xs = jnp.transpose(x_sample, (1, 0, 2))
    hp, cp, hs, cs = [], [], [], []
    for l in range(DEPTH):
        row = lambda a: a[l][None, :]
        w_in_b, w_out_b = _to_bf16(w_in[l]), _to_bf16(w_out[l])
        w_up_b, w_down_b = _to_bf16(w_up[l]), _to_bf16(w_down[l])

        mod = _adaln(jnp.concatenate([c_prompt, c_sample], axis=0), w_ada[l], row(b_ada))
        mod_p = mod[:batch].reshape(batch, 6, D_MODEL)
        mod_s = mod[batch:]

        mixer_params = (lb_logits, w_in_b, row(b_in), row(hg_norm_w), conv_w[l], row(conv_b),
                        row(gn_g), row(gn_b), gmat, w_out_b, row(b_out), row(ln1_g), row(ln1_b))
        mlp_params = (w_up_b, row(b_up), w_down_b, row(b_down), row(ln2_g), row(ln2_b))
        xp, sp, bp = _prompt_layer(xp, mod_p, *mixer_params, *mlp_params, tile=PROMPT_TILE)
        xs, ss, bs = _mixer_sample(xs, mod_s, state_hgrn[l], jnp.transpose(state_conv[l], (1, 0, 2)),
                                   *mixer_params, seqs=SAMPLE_SEQS)
        s_specs = [pl.BlockSpec((dec_batch, D_MODEL), functools.partial(lambda i, r: (0, r), r=r))
                   for r in (3, 4, 5)]
        xs = _mlp(xs, pl.BlockSpec((DEC_SEQ, dec_batch, D_MODEL), lambda i: (0, 0, 0)), (1,),
                  s_specs, (mod_s,) * 3, *mlp_params, name="mlp_sample")

        hp.append(sp); cp.append(bp); hs.append(ss); cs.append(jnp.transpose(bs, (1, 0, 2)))

    return (xp, jnp.transpose(xs, (1, 0, 2)), jnp.stack(hp), jnp.stack(cp), jnp.stack(hs),
            jnp.stack(cs))
```

```python
import functools

import jax
import jax.numpy as jnp
from jax import lax
from jax.experimental import pallas as pl
from jax.experimental.pallas import tpu as pltpu

F32 = jnp.float32
BF16 = jnp.bfloat16

D_MODEL = 1024
HG_WIDTH = 512
CV_WIDTH = 512
HG_HEADS = 4
HG_DK = 128
CV_GROUPS = 8
CV_KERNEL = 31
CV_BUF = CV_KERNEL - 1
D_FF = 4 * D_MODEL
N_IN = 4 * HG_WIDTH + 2 * CV_WIDTH
DEPTH = 1
ALPHA = (2.0 * DEPTH) ** 0.25
EPS = 1e-5

CHUNK = 64
SUBLANES = 8
MXU_WIDTH = 256
CONV_PAD = 32
CONV_ROWS = 128
UP_CHUNK = 512
DOWN_CHUNK = 256
FF_CHUNK = 1024
DEC_SEQ = 4
SEQ_GROUP = 4

VMEM_LIMIT_BYTES = 52 * 1024 * 1024
PROMPT_VMEM_LIMIT_BYTES = 60 * 1024 * 1024
CAST_BLOCK_BYTES = 4 * 1024 * 1024
ADALN_COLS = 1024

_NT = (((1,), (1,)), ((), ()))
_TN = (((0,), (0,)), ((), ()))


def _dot(a, b):
    return jnp.dot(a, b, preferred_element_type=F32)


def _dot_nt(a, b):
    return lax.dot_general(a, b, _NT, preferred_element_type=F32)


def _dot_tn(a, b):
    return lax.dot_general(a, b, _TN, preferred_element_type=F32)


def _silu(x):
    return x * jax.nn.sigmoid(x)


def _split3(x):
    hi = x.astype(BF16).astype(F32)
    r = x - hi
    mid = r.astype(BF16).astype(F32)
    lo = (r - mid).astype(BF16).astype(F32)
    return hi, mid, lo


def _layer_norm(y, g, b):
    mu = jnp.mean(y, axis=-1, keepdims=True)
    yc = y - mu
    var = jnp.mean(yc * yc, axis=-1, keepdims=True)
    return yc * lax.rsqrt(var + EPS) * g + b


def _adaln_kernel(cp_ref, cs_ref, w_ref, b_ref, op_ref, os_ref):
    a = _silu(jnp.concatenate([cp_ref[...], cs_ref[...]], axis=0))
    a_hi = a.astype(BF16)
    a_lo = (a - a_hi.astype(F32)).astype(BF16)
    w = w_ref[...].astype(BF16)
    mod = _dot(a_hi, w) + _dot(a_lo, w) + b_ref[...]
    n_prompt = cp_ref.shape[0]
    op_ref[...] = mod[:n_prompt]
    os_ref[...] = mod[n_prompt:]


def _adaln(c_prompt, c_sample, w_ada, b_ada):
    n = w_ada.shape[1]
    tn = ADALN_COLS
    rows = lambda c: pl.BlockSpec((c.shape[0], D_MODEL), lambda j: (0, 0))
    cols = lambda c: pl.BlockSpec((c.shape[0], tn), lambda j: (0, j))
    return pl.pallas_call(
        _adaln_kernel,
        grid=(n // tn,),
        in_specs=[
            rows(c_prompt),
            rows(c_sample),
            pl.BlockSpec((D_MODEL, tn), lambda j: (0, j)),
            pl.BlockSpec((1, tn), lambda j: (0, j)),
        ],
        out_specs=[cols(c_prompt), cols(c_sample)],
        out_shape=[jax.ShapeDtypeStruct((c.shape[0], n), F32) for c in (c_prompt, c_sample)],
        compiler_params=pltpu.CompilerParams(
            dimension_semantics=("arbitrary",), vmem_limit_bytes=VMEM_LIMIT_BYTES),
        name="adaln",
    )(c_prompt, c_sample, w_ada, b_ada)


def _cast_kernel(w_ref, o_ref):
    o_ref[...] = w_ref[...].astype(o_ref.dtype)


def _to_bf16(w):
    rows, cols = w.shape
    block_rows = rows
    while block_rows * cols * w.dtype.itemsize > CAST_BLOCK_BYTES and block_rows % 16 == 0:
        block_rows //= 2
    spec = pl.BlockSpec((block_rows, cols), lambda i: (i, 0))
    return pl.pallas_call(
        _cast_kernel,
        grid=(rows // block_rows,),
        in_specs=[spec],
        out_specs=spec,
        out_shape=jax.ShapeDtypeStruct(w.shape, BF16),
        compiler_params=pltpu.CompilerParams(
            dimension_semantics=("arbitrary",), vmem_limit_bytes=VMEM_LIMIT_BYTES),
        name="to_bf16",
    )(w)


def _lower_bound(lb_ref):
    lg = lb_ref[...]
    e = jnp.exp(lg - jnp.max(lg, axis=0, keepdims=True))
    p = e / jnp.sum(e, axis=0, keepdims=True)
    return jnp.sum(p[0:DEPTH, :], axis=0, keepdims=True)


def _section(hb, w_in_ref, b_in_ref, i, width):
    return _dot(hb, w_in_ref[:, i:i + width]) + b_in_ref[:, i:i + width]


def _in_projection(hb, lb, w_in_ref, b_in_ref):
    sec = functools.partial(_section, hb, w_in_ref, b_in_ref)

    q = _silu(sec(0, HG_WIDTH))
    f = lb + (1.0 - lb) * jax.nn.sigmoid(sec(HG_WIDTH, HG_WIDTH))
    v = sec(2 * HG_WIDTH, HG_WIDTH)
    gate = _silu(sec(3 * HG_WIDTH, HG_WIDTH))
    za = sec(4 * HG_WIDTH, CV_WIDTH)
    zb = sec(4 * HG_WIDTH + CV_WIDTH, CV_WIDTH)
    return q, 1.0 - f, jnp.log(f), v, gate, za * jax.nn.sigmoid(zb)


def _head_norm(o, hgw, gate):
    ms = jnp.mean(o * o, axis=-1, keepdims=True)
    return o * lax.rsqrt(ms + EPS) * hgw * gate


def _group_norm_swish(uc, gmat_ref, gn_g, gn_b):
    width = gmat_ref.shape[0]

    def group_mean(a):
        ab = a.astype(BF16)
        return jnp.concatenate([_dot(ab[:, c0:c0 + width], gmat_ref[...])
                                for c0 in range(0, a.shape[1], width)], axis=-1)

    d = uc - group_mean(uc)
    var = group_mean(d * d)
    return _silu(d * lax.rsqrt(var + EPS) * gn_g + gn_b)


def _out_projection(mix_bf16, w_out_ref, b_out_ref):
    return _dot(mix_bf16, w_out_ref[...]) + b_out_ref[...]


def _prompt_kernel(x_ref, mod_ref, lb_ref, w_in_ref, b_in_ref, hgw_ref, cw_ref, cb_ref,
                   gng_ref, gnb_ref, gmat_ref, w_out_ref, b_out_ref, ln1g_ref, ln1b_ref,
                   w_up_ref, b_up_ref, w_down_ref, b_down_ref, ln2g_ref, ln2b_ref,
                   y_ref, s_out_ref, cbuf_out_ref,
                   q_s, k_s, lf_s, v_s, g_s, ubuf, ush, uc_s, mix_s, st_s,
                   x1_s, hb2_s, res2_s, act_s, ff_s, *, tile, tiles_per_seq, n_tiles):
    i = pl.program_id(0)
    mixer_tile = jnp.minimum(i, n_tiles - 1)
    b_mix = mixer_tile // tiles_per_seq
    t = mixer_tile % tiles_per_seq
    b_mlp = jnp.maximum(i - 1, 0) // tiles_per_seq

    @pl.when(i == 0)
    def _():
        x1_s[...] = jnp.zeros_like(x1_s)

    @pl.when(t == 0)
    def _():
        st_s[...] = jnp.zeros_like(st_s)
        ubuf[0:CONV_PAD, :] = jnp.zeros((CONV_PAD, CV_WIDTH), F32)

    x1_prev = x1_s[...]
    sh2 = mod_ref[b_mlp, 3:4, :]
    sc2 = mod_ref[b_mlp, 4:5, :]
    g2 = mod_ref[b_mlp, 5:6, :]
    hb2_s[...] = (x1_prev * (1.0 + sc2) + sh2).astype(BF16)
    res2_s[...] = ALPHA * x1_prev

    def mlp_up(c0):
        cs = slice(c0, c0 + UP_CHUNK)
        a = jnp.maximum(_dot(hb2_s[...], w_up_ref[:, cs]) + b_up_ref[:, cs], 0.0)
        act_s[:, cs] = (a * a).astype(BF16)

    def mlp_down(c0):
        cs = slice(c0, c0 + DOWN_CHUNK)
        ff_s[:, cs] = _dot(act_s[...], w_down_ref[:, cs]) + b_down_ref[:, cs]

    x = x_ref[0]
    sh1 = mod_ref[b_mix, 0:1, :]
    sc1 = mod_ref[b_mix, 1:2, :]
    g1 = mod_ref[b_mix, 2:3, :]
    hb = (x * (1.0 + sc1) + sh1).astype(BF16)
    sec = functools.partial(_section, hb, w_in_ref, b_in_ref)
    lb = _lower_bound(lb_ref)

    up_chunks = list(range(0, D_FF, UP_CHUNK))
    down_chunks = list(range(0, D_MODEL, DOWN_CHUNK))

    ubuf[CONV_PAD:CONV_PAD + tile, :] = (sec(4 * HG_WIDTH, CV_WIDTH)
                                         * jax.nn.sigmoid(sec(4 * HG_WIDTH + CV_WIDTH, CV_WIDTH)))
    first_off = CONV_PAD - CV_BUF
    shifted_rows = ush.shape[1]

    def conv_piece(r0, cs):
        acc = jnp.broadcast_to(cb_ref[:, cs], (CONV_ROWS, 128))
        for phase in range(SUBLANES):
            taps = [j for j in range(CV_KERNEL) if (first_off + j) % SUBLANES == phase]
            span = (first_off + taps[-1]) // SUBLANES * SUBLANES
            rows = slice(r0, r0 + span + CONV_ROWS)
            win = ubuf[rows, cs] if phase == 0 else ush[phase - 1, rows, :]
            for j in taps:
                a0 = first_off + j - phase
                acc = acc + cw_ref[j:j + 1, cs] * win[a0:a0 + CONV_ROWS, :]
        uc_s[r0:r0 + CONV_ROWS, cs] = acc

    def conv_pieces(n):
        cs = slice(n * 128, (n + 1) * 128)
        for phase in range(1, SUBLANES):
            ush[phase - 1] = ubuf[phase:phase + shifted_rows, cs]
        for r0 in range(0, tile, CONV_ROWS):
            conv_piece(r0, cs)

    mlp_up(up_chunks.pop(0))
    q_s[...] = _silu(sec(0, HG_WIDTH))
    conv_pieces(0)
    mlp_up(up_chunks.pop(0))
    f = lb + (1.0 - lb) * jax.nn.sigmoid(sec(HG_WIDTH, HG_WIDTH))
    k_s[...] = 1.0 - f
    lf_s[...] = jnp.log(f)
    conv_pieces(1)
    mlp_up(up_chunks.pop(0))
    v_s[...] = sec(2 * HG_WIDTH, HG_WIDTH).astype(BF16)
    conv_pieces(2)
    mlp_up(up_chunks.pop(0))
    g_s[...] = _silu(sec(3 * HG_WIDTH, HG_WIDTH))
    conv_pieces(3)

    row = lax.broadcasted_iota(jnp.int32, (CHUNK, HG_WIDTH), 0)
    causal = (lax.broadcasted_iota(jnp.int32, (CHUNK, CHUNK), 0)
              >= lax.broadcasted_iota(jnp.int32, (CHUNK, CHUNK), 1))
    hgw = hgw_ref[...]
    for c in range(tile // CHUNK):
        if up_chunks:
            mlp_up(up_chunks.pop(0))
        rows = slice(c * CHUNK, (c + 1) * CHUNK)
        b = lf_s[rows, :]
        shift = 1
        while shift < CHUNK:
            b = b + jnp.where(row >= shift, pltpu.roll(b, shift, axis=0), 0.0)
            shift *= 2
        b_last = b[CHUNK - 1:CHUNK, :]
        kc = k_s[rows, :]
        qe = (q_s[rows, :] * jnp.exp(b)).astype(BF16)
        ke = (kc * jnp.exp(-b)).astype(BF16)
        kd = (kc * jnp.exp(b_last - b)).astype(BF16)
        decay = jnp.exp(b_last)
        vb = v_s[rows, :]
        for h in range(HG_HEADS):
            hs = slice(h * HG_DK, (h + 1) * HG_DK)
            att = jnp.where(causal, _dot_nt(qe[:, hs], ke[:, hs]), 0.0).astype(BF16)
            st = st_s[h]
            o = _dot(att, vb[:, hs]) + _dot_nt(qe[:, hs], st.astype(BF16))
            st_s[h] = st * decay[:, hs] + _dot_tn(vb[:, hs], kd[:, hs])
            mix_s[rows, hs] = _head_norm(o, hgw, g_s[rows, hs]).astype(BF16)
    while up_chunks:
        mlp_up(up_chunks.pop(0))

    mlp_down(down_chunks.pop(0))
    mix_s[:, HG_WIDTH:] = _group_norm_swish(
        uc_s[...], gmat_ref, gng_ref[...], gnb_ref[...]).astype(BF16)
    ubuf[0:CONV_PAD, :] = ubuf[tile:tile + CONV_PAD, :]
    mlp_down(down_chunks.pop(0))
    mix = _out_projection(mix_s[...], w_out_ref, b_out_ref)
    x1_s[...] = _layer_norm(ALPHA * x + (1.0 + g1) * mix, ln1g_ref[...], ln1b_ref[...])
    while down_chunks:
        mlp_down(down_chunks.pop(0))
    y_ref[0] = _layer_norm(res2_s[...] + (1.0 + g2) * ff_s[...], ln2g_ref[...], ln2b_ref[...])

    @pl.when((t == tiles_per_seq - 1) & (i < n_tiles))
    def _():
        for h in range(HG_HEADS):
            s_out_ref[0, h] = st_s[h].T
        cbuf_out_ref[0] = ubuf[CONV_PAD - CV_BUF:CONV_PAD, :]


def _const_spec(shape):
    zeros = (0,) * len(shape)
    return pl.BlockSpec(shape, lambda *_: zeros, pipeline_mode=pl.Buffered(1))


def _prompt_layer(x, mod, lb_logits, w_in, b_in, hgw, conv_w, conv_b, gn_g, gn_b, gmat,
                  w_out, b_out, ln1_g, ln1_b, w_up, b_up, w_down, b_down, ln2_g, ln2_b, *, tile):
    batch, seq, _ = x.shape
    tiles_per_seq = seq // tile
    n_tiles = batch * tiles_per_seq
    kernel = functools.partial(_prompt_kernel, tile=tile, tiles_per_seq=tiles_per_seq,
                               n_tiles=n_tiles)

    def mixer_block(i):
        m = jnp.minimum(i, n_tiles - 1)
        return m // tiles_per_seq, m % tiles_per_seq

    def mlp_block(i):
        m = jnp.maximum(i - 1, 0)
        return m // tiles_per_seq, m % tiles_per_seq

    consts = (mod, lb_logits, w_in, b_in, hgw, conv_w, conv_b, gn_g, gn_b, gmat, w_out, b_out,
              ln1_g, ln1_b, w_up, b_up, w_down, b_down, ln2_g, ln2_b)
    return pl.pallas_call(
        kernel,
        grid=(n_tiles + 1,),
        in_specs=[pl.BlockSpec((1, tile, D_MODEL), lambda i: (*mixer_block(i), 0))]
        + [_const_spec(a.shape) for a in consts],
        out_specs=[
            pl.BlockSpec((1, tile, D_MODEL), lambda i: (*mlp_block(i), 0)),
            pl.BlockSpec((1, HG_HEADS, HG_DK, HG_DK), lambda i: (mixer_block(i)[0], 0, 0, 0)),
            pl.BlockSpec((1, CV_BUF, CV_WIDTH), lambda i: (mixer_block(i)[0], 0, 0)),
        ],
        out_shape=[
            jax.ShapeDtypeStruct((batch, seq, D_MODEL), F32),
            jax.ShapeDtypeStruct((batch, HG_HEADS, HG_DK, HG_DK), F32),
            jax.ShapeDtypeStruct((batch, CV_BUF, CV_WIDTH), F32),
        ],
        scratch_shapes=[
            pltpu.VMEM((tile, HG_WIDTH), F32),
            pltpu.VMEM((tile, HG_WIDTH), F32),
            pltpu.VMEM((tile, HG_WIDTH), F32),
            pltpu.VMEM((tile, HG_WIDTH), BF16),
            pltpu.VMEM((tile, HG_WIDTH), F32),
            pltpu.VMEM((CONV_PAD + tile, CV_WIDTH), F32),
            pltpu.VMEM((SUBLANES - 1, CONV_PAD + tile - SUBLANES, 128), F32),
            pltpu.VMEM((tile, CV_WIDTH), F32),
            pltpu.VMEM((tile, D_MODEL), BF16),
            pltpu.VMEM((HG_HEADS, HG_DK, HG_DK), F32),
            pltpu.VMEM((tile, D_MODEL), F32),
            pltpu.VMEM((tile, D_MODEL), BF16),
            pltpu.VMEM((tile, D_MODEL), F32),
            pltpu.VMEM((tile, D_FF), BF16),
            pltpu.VMEM((tile, D_MODEL), F32),
        ],
        compiler_params=pltpu.CompilerParams(
            dimension_semantics=("arbitrary",), vmem_limit_bytes=PROMPT_VMEM_LIMIT_BYTES),
        name="prompt_layer",
    )(x, *consts)


def _mixer_sample_kernel(x_ref, mod_ref, s_in_ref, cbuf_in_ref, lb_ref, w_in_ref, b_in_ref, hgw_ref,
                         cw_ref, cb_ref, gng_ref, gnb_ref, gmat_ref, w_out_ref, b_out_ref,
                         lng_ref, lnb_ref,
                         x1_ref, s_out_ref, cbuf_out_ref,
                         qe_s, ke_s, kd_s, v_s, d_s, o_s, *, seqs):
    rows_n = seqs * DEC_SEQ
    x = x_ref[...]
    sh1 = mod_ref[:, 0:D_MODEL]
    sc1 = mod_ref[:, D_MODEL:2 * D_MODEL]
    g1 = mod_ref[:, 2 * D_MODEL:3 * D_MODEL]
    hb = (x * (1.0 + sc1) + sh1).astype(BF16).reshape(rows_n, D_MODEL)
    q, k, lf, v, gate, u = _in_projection(hb, _lower_bound(lb_ref), w_in_ref, b_in_ref)

    slab = lambda a, t: a[t * seqs:(t + 1) * seqs, :]
    b = [slab(lf, 0)]
    for t in range(1, DEC_SEQ):
        b.append(b[-1] + slab(lf, t))
    b_last = b[-1]
    parts = _split3(jnp.exp(b_last)) + (jnp.zeros_like(b_last),)

    def put(ref, t, val):
        for h in range(HG_HEADS):
            ref[h, t * seqs:(t + 1) * seqs, :] = val[:, h * HG_DK:(h + 1) * HG_DK]

    for t in range(DEC_SEQ):
        put(qe_s, t, slab(q, t) * jnp.exp(b[t]))
        put(ke_s, t, slab(k, t) * jnp.exp(-b[t]))
        put(kd_s, t, slab(k, t) * jnp.exp(b_last - b[t]))
        put(d_s, t, parts[t])
        put(v_s, t, slab(v, t))

    stride = seqs // SEQ_GROUP
    n_rows = SEQ_GROUP * DEC_SEQ
    member = lax.broadcasted_iota(jnp.int32, (n_rows, HG_DK), 0) % SEQ_GROUP
    lanes = lambda g: slice(g * HG_DK, (g + 1) * HG_DK)
    per_member = lambda a: jnp.concatenate(
        [jnp.where(member == g, a, 0.0) for g in range(SEQ_GROUP)], axis=1)
    ones_g = per_member(jnp.ones((n_rows, HG_DK), F32)).astype(BF16)
    ri = lax.broadcasted_iota(jnp.int32, (n_rows, n_rows), 0)
    ci = lax.broadcasted_iota(jnp.int32, (n_rows, n_rows), 1)
    causal = (ri // SEQ_GROUP >= ci // SEQ_GROUP) & (ri % SEQ_GROUP == ci % SEQ_GROUP)
    for s in range(stride):
        rows = pl.ds(s, n_rows, stride=stride)
        for h in range(HG_HEADS):
            qe_g = qe_s[h, rows, :].astype(BF16)
            ke_g = ke_s[h, rows, :].astype(BF16)
            kd_g = kd_s[h, rows, :].astype(BF16)
            d_g = d_s[h, rows, :].astype(BF16)
            v_g = v_s[h, rows, :]
            s_g = jnp.concatenate([s_in_ref[s + g * stride, h] for g in range(SEQ_GROUP)], axis=1)
            att = jnp.where(causal, _dot_nt(qe_g, ke_g), 0.0).astype(BF16)
            inter_all = _dot(qe_g, s_g.astype(BF16))
            inter = inter_all[:, lanes(0)]
            for g in range(1, SEQ_GROUP):
                inter = jnp.where(member == g, inter_all[:, lanes(g)], inter)
            o_s[h, rows, :] = _dot(att, v_g.astype(BF16)) + inter
            new = _dot_tn(d_g, ones_g) * s_g + _dot_tn(kd_g, per_member(v_g).astype(BF16))
            for g in range(SEQ_GROUP):
                s_out_ref[s + g * stride, h] = new[:, lanes(g)]
    hgw = hgw_ref[...]
    o_a = jnp.concatenate(
        [_head_norm(o_s[h], hgw, gate[:, h * HG_DK:(h + 1) * HG_DK]) for h in range(HG_HEADS)],
        axis=-1)

    full = lambda i: cbuf_in_ref[i] if i < CV_BUF else slab(u, i - CV_BUF)
    acc = [jnp.broadcast_to(cb_ref[...], (seqs, CV_WIDTH)) for _ in range(DEC_SEQ)]
    for i in range(CV_BUF + DEC_SEQ):
        f_i = full(i)
        for t in range(DEC_SEQ):
            if 0 <= i - t < CV_KERNEL:
                acc[t] = acc[t] + cw_ref[i - t:i - t + 1, :] * f_i
    for i in range(CV_BUF):
        cbuf_out_ref[i] = full(i + DEC_SEQ)
    o_b = _group_norm_swish(jnp.concatenate(acc, axis=0), gmat_ref, gng_ref[...], gnb_ref[...])

    mix = _out_projection(jnp.concatenate([o_a, o_b], axis=-1).astype(BF16), w_out_ref, b_out_ref)
    y = ALPHA * x + (1.0 + g1) * mix.reshape(DEC_SEQ, seqs, D_MODEL)
    x1_ref[...] = _layer_norm(y, lng_ref[...], lnb_ref[...])


def _mixer_sample(x, mod, s_in, cbuf_in, lb_logits, w_in, b_in, hgw, conv_w, conv_b, gn_g, gn_b,
                  gmat, w_out, b_out, ln_g, ln_b, *, seqs):
    n_seq = s_in.shape[0]
    rows = seqs * DEC_SEQ
    kernel = functools.partial(_mixer_sample_kernel, seqs=seqs)
    return pl.pallas_call(
        kernel,
        grid=(n_seq // seqs,),
        in_specs=[
            pl.BlockSpec((DEC_SEQ, seqs, D_MODEL), lambda i: (0, i, 0)),
            pl.BlockSpec((seqs, 3 * D_MODEL), lambda i: (i, 0)),
            pl.BlockSpec((seqs, HG_HEADS, HG_DK, HG_DK), lambda i: (i, 0, 0, 0)),
            pl.BlockSpec((CV_BUF, seqs, CV_WIDTH), lambda i: (0, i, 0)),
            _const_spec(lb_logits.shape),
            _const_spec(w_in.shape),
            _const_spec(b_in.shape),
            _const_spec(hgw.shape),
            _const_spec(conv_w.shape),
            _const_spec(conv_b.shape),
            _const_spec(gn_g.shape),
            _const_spec(gn_b.shape),
            _const_spec(gmat.shape),
            _const_spec(w_out.shape),
            _const_spec(b_out.shape),
            _const_spec(ln_g.shape),
            _const_spec(ln_b.shape),
        ],
        out_specs=[
            pl.BlockSpec((DEC_SEQ, seqs, D_MODEL), lambda i: (0, i, 0)),
            pl.BlockSpec((seqs, HG_HEADS, HG_DK, HG_DK), lambda i: (i, 0, 0, 0)),
            pl.BlockSpec((CV_BUF, seqs, CV_WIDTH), lambda i: (0, i, 0)),
        ],
        out_shape=[
            jax.ShapeDtypeStruct((DEC_SEQ, n_seq, D_MODEL), F32),
            jax.ShapeDtypeStruct(s_in.shape, F32),
            jax.ShapeDtypeStruct(cbuf_in.shape, F32),
        ],
        scratch_shapes=[pltpu.VMEM((HG_HEADS, rows, HG_DK), F32)] * 6,
        compiler_params=pltpu.CompilerParams(
            dimension_semantics=("arbitrary",), vmem_limit_bytes=VMEM_LIMIT_BYTES),
        name="mixer_sample",
    )(x, mod, s_in, cbuf_in, lb_logits, w_in, b_in, hgw, conv_w, conv_b, gn_g, gn_b, gmat,
      w_out, b_out, ln_g, ln_b)


def _mlp_kernel(x_ref, sh_ref, sc_ref, g_ref, w_up_ref, b_up_ref, w_down_ref, b_down_ref,
                lng_ref, lnb_ref, o_ref):
    x = x_ref[...]
    hb = (x * (1.0 + sc_ref[...]) + sh_ref[...]).astype(BF16).reshape(-1, D_MODEL)
    ff = jnp.broadcast_to(b_down_ref[...], hb.shape)
    for c0 in range(0, D_FF, FF_CHUNK):
        cs = slice(c0, c0 + FF_CHUNK)
        a = jnp.maximum(_dot(hb, w_up_ref[:, cs]) + b_up_ref[:, cs], 0.0)
        ff = ff + _dot((a * a).astype(BF16), w_down_ref[cs, :])
    y = ALPHA * x + (1.0 + g_ref[...]) * ff.reshape(x.shape)
    o_ref[...] = _layer_norm(y, lng_ref[...], lnb_ref[...])


def _mlp_sample(x, mod, w_up, b_up, w_down, b_down, ln_g, ln_b):
    x_spec = pl.BlockSpec(x.shape, lambda i: (0, 0, 0))
    mod_specs = [pl.BlockSpec((x.shape[1], D_MODEL), functools.partial(lambda i, r: (0, r), r=r))
                 for r in (3, 4, 5)]
    return pl.pallas_call(
        _mlp_kernel,
        grid=(1,),
        in_specs=[x_spec] + mod_specs + [
            _const_spec(w_up.shape),
            _const_spec(b_up.shape),
            _const_spec(w_down.shape),
            _const_spec(b_down.shape),
            _const_spec(ln_g.shape),
            _const_spec(ln_b.shape),
        ],
        out_specs=x_spec,
        out_shape=jax.ShapeDtypeStruct(x.shape, F32),
        compiler_params=pltpu.CompilerParams(
            dimension_semantics=("arbitrary",), vmem_limit_bytes=VMEM_LIMIT_BYTES),
        name="mlp_sample",
    )(x, mod, mod, mod, w_up, b_up, w_down, b_down, ln_g, ln_b)


PROMPT_TILE = 512
SAMPLE_SEQS = 16


def kernel(x_prompt, x_sample, c_prompt, c_sample, state_hgrn, state_conv, lb_logits, w_in, b_in,
           hg_norm_w, conv_w, conv_b, gn_g, gn_b, w_out, b_out, ln1_g, ln1_b, w_up, b_up, w_down,
           b_down, ln2_g, ln2_b, w_ada, b_ada):
    assert w_in.shape[0] == DEPTH
    batch, seq, _ = x_prompt.shape
    dec_batch, dec_seq, _ = x_sample.shape
    assert dec_seq == DEC_SEQ

    group = jnp.arange(MXU_WIDTH, dtype=jnp.int32) // (CV_WIDTH // CV_GROUPS)
    gmat = jnp.where(group[:, None] == group[None, :], CV_GROUPS / CV_WIDTH, 0.0).astype(BF16)

    xp = x_prompt
    xs = jnp.transpose(x_sample, (1, 0, 2))
    hp, cp, hs, cs = [], [], [], []
    for l in range(DEPTH):
        row = lambda a: a[l][None, :]
        w_in_b, w_out_b = _to_bf16(w_in[l]), _to_bf16(w_out[l])
        w_up_b, w_down_b = _to_bf16(w_up[l]), _to_bf16(w_down[l])

        mod_p, mod_s = _adaln(c_prompt, c_sample, w_ada[l], row(b_ada))
        mod_p = mod_p.reshape(batch, 6, D_MODEL)

        mixer_params = (lb_logits, w_in_b, row(b_in), row(hg_norm_w), conv_w[l], row(conv_b),
                        row(gn_g), row(gn_b), gmat, w_out_b, row(b_out), row(ln1_g), row(ln1_b))
        mlp_params = (w_up_b, row(b_up), w_down_b, row(b_down), row(ln2_g), row(ln2_b))
        xp, sp, bp = _prompt_layer(xp, mod_p, *mixer_params, *mlp_params, tile=PROMPT_TILE)
        xs, ss, bs = _mixer_sample(xs, mod_s, state_hgrn[l], jnp.transpose(state_conv[l], (1, 0, 2)),
                                   *mixer_params, seqs=SAMPLE_SEQS)
        xs = _mlp_sample(xs, mod_s, *mlp_params)

        hp.append(sp); cp.append(bp); hs.append(ss); cs.append(jnp.transpose(bs, (1, 0, 2)))

    return (xp, jnp.transpose(xs, (1, 0, 2)), jnp.stack(hp), jnp.stack(cp), jnp.stack(hs),
            jnp.stack(cs))
```

```python
import functools

import jax
import jax.numpy as jnp
from jax import lax
from jax.experimental import pallas as pl
from jax.experimental.pallas import tpu as pltpu

F32 = jnp.float32
BF16 = jnp.bfloat16

D_MODEL = 1024
HG_WIDTH = 512
CV_WIDTH = 512
HG_HEADS = 4
HG_DK = 128
CV_GROUPS = 8
CV_KERNEL = 31
CV_BUF = CV_KERNEL - 1
D_FF = 4 * D_MODEL
N_IN = 4 * HG_WIDTH + 2 * CV_WIDTH
DEPTH = 1
ALPHA = (2.0 * DEPTH) ** 0.25
EPS = 1e-5

CHUNK = 64
SUBLANES = 8
LANES = 128
MXU_WIDTH = 256
CONV_PAD = 32
CONV_ROWS = 128
CONV_CHAINS = 2
UP_CHUNK = 512
DOWN_CHUNK = 256
FF_CHUNK = 1024
DEC_SEQ = 4
SEQ_GROUP = 4

VMEM_LIMIT_BYTES = 52 * 1024 * 1024
PROMPT_VMEM_LIMIT_BYTES = 60 * 1024 * 1024
CAST_BLOCK_BYTES = 4 * 1024 * 1024
ADALN_COLS = 1024

_NT = (((1,), (1,)), ((), ()))
_TN = (((0,), (0,)), ((), ()))


def _dot(a, b):
    return jnp.dot(a, b, preferred_element_type=F32)


def _dot_nt(a, b):
    return lax.dot_general(a, b, _NT, preferred_element_type=F32)


def _dot_tn(a, b):
    return lax.dot_general(a, b, _TN, preferred_element_type=F32)


def _silu(x):
    return x * jax.nn.sigmoid(x)


def _split3(x):
    hi = x.astype(BF16).astype(F32)
    r = x - hi
    mid = r.astype(BF16).astype(F32)
    lo = (r - mid).astype(BF16).astype(F32)
    return hi, mid, lo


def _layer_norm(y, g, b):
    mu = jnp.mean(y, axis=-1, keepdims=True)
    yc = y - mu
    var = jnp.mean(yc * yc, axis=-1, keepdims=True)
    return yc * lax.rsqrt(var + EPS) * g + b


def _adaln_kernel(cp_ref, cs_ref, w_ref, b_ref, op_ref, os_ref):
    a = _silu(jnp.concatenate([cp_ref[...], cs_ref[...]], axis=0))
    a_hi = a.astype(BF16)
    a_lo = (a - a_hi.astype(F32)).astype(BF16)
    w = w_ref[...].astype(BF16)
    mod = _dot(a_hi, w) + _dot(a_lo, w) + b_ref[...]
    n_prompt = cp_ref.shape[0]
    op_ref[...] = mod[:n_prompt]
    os_ref[...] = mod[n_prompt:]


def _adaln(c_prompt, c_sample, w_ada, b_ada):
    n = w_ada.shape[1]
    tn = ADALN_COLS
    rows = lambda c: pl.BlockSpec((c.shape[0], D_MODEL), lambda j: (0, 0))
    cols = lambda c: pl.BlockSpec((c.shape[0], tn), lambda j: (0, j))
    return pl.pallas_call(
        _adaln_kernel,
        grid=(n // tn,),
        in_specs=[
            rows(c_prompt),
            rows(c_sample),
            pl.BlockSpec((D_MODEL, tn), lambda j: (0, j)),
            pl.BlockSpec((1, tn), lambda j: (0, j)),
        ],
        out_specs=[cols(c_prompt), cols(c_sample)],
        out_shape=[jax.ShapeDtypeStruct((c.shape[0], n), F32) for c in (c_prompt, c_sample)],
        compiler_params=pltpu.CompilerParams(
            dimension_semantics=("arbitrary",), vmem_limit_bytes=VMEM_LIMIT_BYTES),
        name="adaln",
    )(c_prompt, c_sample, w_ada, b_ada)


def _cast_kernel(w_ref, o_ref):
    o_ref[...] = w_ref[...].astype(o_ref.dtype)


def _to_bf16(w):
    rows, cols = w.shape
    block_rows = rows
    while block_rows * cols * w.dtype.itemsize > CAST_BLOCK_BYTES and block_rows % 16 == 0:
        block_rows //= 2
    spec = pl.BlockSpec((block_rows, cols), lambda i: (i, 0))
    return pl.pallas_call(
        _cast_kernel,
        grid=(rows // block_rows,),
        in_specs=[spec],
        out_specs=spec,
        out_shape=jax.ShapeDtypeStruct(w.shape, BF16),
        compiler_params=pltpu.CompilerParams(
            dimension_semantics=("arbitrary",), vmem_limit_bytes=VMEM_LIMIT_BYTES),
        name="to_bf16",
    )(w)


def _lower_bound(lb_ref):
    lg = lb_ref[...]
    e = jnp.exp(lg - jnp.max(lg, axis=0, keepdims=True))
    p = e / jnp.sum(e, axis=0, keepdims=True)
    return jnp.sum(p[0:DEPTH, :], axis=0, keepdims=True)


def _section(hb, w_in_ref, b_in_ref, i, width):
    return _dot(hb, w_in_ref[:, i:i + width]) + b_in_ref[:, i:i + width]


def _in_projection(hb, lb, w_in_ref, b_in_ref):
    sec = functools.partial(_section, hb, w_in_ref, b_in_ref)

    q = _silu(sec(0, HG_WIDTH))
    f = lb + (1.0 - lb) * jax.nn.sigmoid(sec(HG_WIDTH, HG_WIDTH))
    v = sec(2 * HG_WIDTH, HG_WIDTH)
    gate = _silu(sec(3 * HG_WIDTH, HG_WIDTH))
    za = sec(4 * HG_WIDTH, CV_WIDTH)
    zb = sec(4 * HG_WIDTH + CV_WIDTH, CV_WIDTH)
    return q, 1.0 - f, jnp.log(f), v, gate, za * jax.nn.sigmoid(zb)


def _head_norm(o, hgw, gate):
    ms = jnp.mean(o * o, axis=-1, keepdims=True)
    return o * lax.rsqrt(ms + EPS) * hgw * gate


def _group_norm_swish(uc, gmat_ref, gn_g, gn_b):
    width = gmat_ref.shape[0]

    def group_mean(ab):
        return jnp.concatenate([_dot(ab[:, c0:c0 + width], gmat_ref[...])
                                for c0 in range(0, ab.shape[1], width)], axis=-1)

    uc_hi = uc.astype(BF16)
    uc_lo = (uc - uc_hi.astype(F32)).astype(BF16)
    d = uc - (group_mean(uc_hi) + group_mean(uc_lo))
    var = group_mean((d * d).astype(BF16))
    return _silu(d * lax.rsqrt(var + EPS) * gn_g + gn_b)


def _out_projection(mix_bf16, w_out_ref, b_out_ref):
    return _dot(mix_bf16, w_out_ref[...]) + b_out_ref[...]


def _prompt_kernel(*refs, tile, tiles_per_seq, n_tiles):
    i = pl.program_id(0)
    step = functools.partial(_prompt_step, *refs, tile=tile, tiles_per_seq=tiles_per_seq,
                             n_tiles=n_tiles)
    pl.when(i < n_tiles)(functools.partial(step, run_mixer=True))
    pl.when(i == n_tiles)(functools.partial(step, run_mixer=False))


def _prompt_step(x_ref, mod_ref, lb_ref, w_in_ref, b_in_ref, hgw_ref, cw_ref, cb_ref,
                 gng_ref, gnb_ref, gmat_ref, w_out_ref, b_out_ref, ln1g_ref, ln1b_ref,
                 w_up_ref, b_up_ref, w_down_ref, b_down_ref, ln2g_ref, ln2b_ref,
                 y_ref, s_out_ref, cbuf_out_ref,
                 q_s, k_s, lf_s, v_s, g_s, ubuf, ush, uc_s, mix_s, st_s,
                 x1_s, hb2_s, res2_s, act_s, ff_s, *, tile, tiles_per_seq, n_tiles,
                 run_mixer):
    i = pl.program_id(0)
    mixer_tile = jnp.minimum(i, n_tiles - 1)
    b_mix = mixer_tile // tiles_per_seq
    t = mixer_tile % tiles_per_seq
    b_mlp = jnp.maximum(i - 1, 0) // tiles_per_seq

    def mlp_prologue():
        x1_prev = x1_s[...]
        sh2 = mod_ref[b_mlp, 3:4, :]
        sc2 = mod_ref[b_mlp, 4:5, :]
        hb2_s[...] = (x1_prev * (1.0 + sc2) + sh2).astype(BF16)
        res2_s[...] = ALPHA * x1_prev

    def mlp_epilogue():
        g2 = mod_ref[b_mlp, 5:6, :]
        y_ref[0] = _layer_norm(res2_s[...] + (1.0 + g2) * ff_s[...], ln2g_ref[...], ln2b_ref[...])

    def mlp_up(c0):
        cs = slice(c0, c0 + UP_CHUNK)
        a = jnp.maximum(_dot(hb2_s[...], w_up_ref[:, cs]) + b_up_ref[:, cs], 0.0)
        act_s[:, cs] = (a * a).astype(BF16)

    def mlp_down(c0):
        cs = slice(c0, c0 + DOWN_CHUNK)
        ff_s[:, cs] = _dot(act_s[...], w_down_ref[:, cs]) + b_down_ref[:, cs]

    up_chunks = list(range(0, D_FF, UP_CHUNK))
    down_chunks = list(range(0, D_MODEL, DOWN_CHUNK))

    def next_up():
        if up_chunks:
            mlp_up(up_chunks.pop(0))

    def next_down():
        if down_chunks:
            mlp_down(down_chunks.pop(0))

    @pl.when(i == 0)
    def _():
        x1_s[...] = jnp.zeros_like(x1_s)

    mlp_prologue()
    if not run_mixer:
        while up_chunks:
            next_up()
        while down_chunks:
            next_down()
        mlp_epilogue()
        return

    @pl.when(t == 0)
    def _():
        st_s[...] = jnp.zeros_like(st_s)
        ubuf[0:CONV_PAD, :] = jnp.zeros((CONV_PAD, CV_WIDTH), F32)

    x = x_ref[0]
    sh1 = mod_ref[b_mix, 0:1, :]
    sc1 = mod_ref[b_mix, 1:2, :]
    g1 = mod_ref[b_mix, 2:3, :]
    hb = (x * (1.0 + sc1) + sh1).astype(BF16)
    sec = functools.partial(_section, hb, w_in_ref, b_in_ref)
    lb = _lower_bound(lb_ref)

    ubuf[CONV_PAD:CONV_PAD + tile, :] = (sec(4 * HG_WIDTH, CV_WIDTH)
                                         * jax.nn.sigmoid(sec(4 * HG_WIDTH + CV_WIDTH, CV_WIDTH)))
    first_off = CONV_PAD - CV_BUF
    shifted_rows = ush.shape[1]

    def conv_piece(r0, cs):
        accs = [jnp.broadcast_to(cb_ref[:, cs], (CONV_ROWS, LANES))] + [None] * (CONV_CHAINS - 1)
        for phase in range(SUBLANES):
            taps = [j for j in range(CV_KERNEL) if (first_off + j) % SUBLANES == phase]
            span = (first_off + taps[-1]) // SUBLANES * SUBLANES
            rows = slice(r0, r0 + span + CONV_ROWS)
            win = ubuf[rows, cs] if phase == 0 else ush[phase - 1, rows, :]
            c = phase % CONV_CHAINS
            for j in taps:
                a0 = first_off + j - phase
                term = cw_ref[j:j + 1, cs] * win[a0:a0 + CONV_ROWS, :]
                accs[c] = term if accs[c] is None else accs[c] + term
        uc_s[r0:r0 + CONV_ROWS, cs] = functools.reduce(lambda a, b: a + b, accs)

    def conv_pieces(n):
        cs = slice(n * LANES, (n + 1) * LANES)
        for phase in range(1, SUBLANES):
            ush[phase - 1] = ubuf[phase:phase + shifted_rows, cs]
        for r0 in range(0, tile, CONV_ROWS):
            conv_piece(r0, cs)

    next_up()
    q_s[...] = _silu(sec(0, HG_WIDTH))
    conv_pieces(0)
    next_up()
    f = lb + (1.0 - lb) * jax.nn.sigmoid(sec(HG_WIDTH, HG_WIDTH))
    k_s[...] = 1.0 - f
    lf_s[...] = jnp.log(f)
    conv_pieces(1)
    next_up()
    v_s[...] = sec(2 * HG_WIDTH, HG_WIDTH).astype(BF16)
    conv_pieces(2)
    next_up()
    g_s[...] = _silu(sec(3 * HG_WIDTH, HG_WIDTH))
    conv_pieces(3)

    row = lax.broadcasted_iota(jnp.int32, (CHUNK, HG_WIDTH), 0)
    causal = (lax.broadcasted_iota(jnp.int32, (CHUNK, CHUNK), 0)
              >= lax.broadcasted_iota(jnp.int32, (CHUNK, CHUNK), 1))
    hgw = hgw_ref[...]
    for c in range(tile // CHUNK):
        next_up()
        rows = slice(c * CHUNK, (c + 1) * CHUNK)
        b = lf_s[rows, :]
        shift = 1
        while shift < CHUNK:
            b = b + jnp.where(row >= shift, pltpu.roll(b, shift, axis=0), 0.0)
            shift *= 2
        b_last = b[CHUNK - 1:CHUNK, :]
        kc = k_s[rows, :]
        qe = (q_s[rows, :] * jnp.exp(b)).astype(BF16)
        ke = (kc * jnp.exp(-b)).astype(BF16)
        kd = (kc * jnp.exp(b_last - b)).astype(BF16)
        decay = jnp.exp(b_last)
        vb = v_s[rows, :]
        for h in range(HG_HEADS):
            hs = slice(h * HG_DK, (h + 1) * HG_DK)
            att = jnp.where(causal, _dot_nt(qe[:, hs], ke[:, hs]), 0.0).astype(BF16)
            st = st_s[h]
            o = _dot(att, vb[:, hs]) + _dot_nt(qe[:, hs], st.astype(BF16))
            st_s[h] = st * decay[:, hs] + _dot_tn(vb[:, hs], kd[:, hs])
            mix_s[rows, hs] = _head_norm(o, hgw, g_s[rows, hs]).astype(BF16)
    while up_chunks:
        next_up()

    next_down()
    mix_s[:, HG_WIDTH:] = _group_norm_swish(
        uc_s[...], gmat_ref, gng_ref[...], gnb_ref[...]).astype(BF16)
    ubuf[0:CONV_PAD, :] = ubuf[tile:tile + CONV_PAD, :]
    next_down()
    mix = _out_projection(mix_s[...], w_out_ref, b_out_ref)
    x1_s[...] = _layer_norm(ALPHA * x + (1.0 + g1) * mix, ln1g_ref[...], ln1b_ref[...])
    while down_chunks:
        next_down()
    mlp_epilogue()

    @pl.when(t == tiles_per_seq - 1)
    def _():
        for h in range(HG_HEADS):
            s_out_ref[0, h] = st_s[h].T
        cbuf_out_ref[0] = ubuf[CONV_PAD - CV_BUF:CONV_PAD, :]


def _const_spec(shape):
    zeros = (0,) * len(shape)
    return pl.BlockSpec(shape, lambda *_: zeros, pipeline_mode=pl.Buffered(1))


def _prompt_layer(x, mod, lb_logits, w_in, b_in, hgw, conv_w, conv_b, gn_g, gn_b, gmat,
                  w_out, b_out, ln1_g, ln1_b, w_up, b_up, w_down, b_down, ln2_g, ln2_b, *, tile):
    batch, seq, _ = x.shape
    tiles_per_seq = seq // tile
    n_tiles = batch * tiles_per_seq
    kernel = functools.partial(_prompt_kernel, tile=tile, tiles_per_seq=tiles_per_seq,
                               n_tiles=n_tiles)

    def mixer_block(i):
        m = jnp.minimum(i, n_tiles - 1)
        return m // tiles_per_seq, m % tiles_per_seq

    def mlp_block(i):
        m = jnp.maximum(i - 1, 0)
        return m // tiles_per_seq, m % tiles_per_seq

    consts = (mod, lb_logits, w_in, b_in, hgw, conv_w, conv_b, gn_g, gn_b, gmat, w_out, b_out,
              ln1_g, ln1_b, w_up, b_up, w_down, b_down, ln2_g, ln2_b)
    return pl.pallas_call(
        kernel,
        grid=(n_tiles + 1,),
        in_specs=[pl.BlockSpec((1, tile, D_MODEL), lambda i: (*mixer_block(i), 0))]
        + [_const_spec(a.shape) for a in consts],
        out_specs=[
            pl.BlockSpec((1, tile, D_MODEL), lambda i: (*mlp_block(i), 0)),
            pl.BlockSpec((1, HG_HEADS, HG_DK, HG_DK), lambda i: (mixer_block(i)[0], 0, 0, 0)),
            pl.BlockSpec((1, CV_BUF, CV_WIDTH), lambda i: (mixer_block(i)[0], 0, 0)),
        ],
        out_shape=[
            jax.ShapeDtypeStruct((batch, seq, D_MODEL), F32),
            jax.ShapeDtypeStruct((batch, HG_HEADS, HG_DK, HG_DK), F32),
            jax.ShapeDtypeStruct((batch, CV_BUF, CV_WIDTH), F32),
        ],
        scratch_shapes=[
            pltpu.VMEM((tile, HG_WIDTH), F32),
            pltpu.VMEM((tile, HG_WIDTH), F32),
            pltpu.VMEM((tile, HG_WIDTH), F32),
            pltpu.VMEM((tile, HG_WIDTH), BF16),
            pltpu.VMEM((tile, HG_WIDTH), F32),
            pltpu.VMEM((CONV_PAD + tile, CV_WIDTH), F32),
            pltpu.VMEM((SUBLANES - 1, CONV_PAD + tile - SUBLANES, LANES), F32),
            pltpu.VMEM((tile, CV_WIDTH), F32),
            pltpu.VMEM((tile, D_MODEL), BF16),
            pltpu.VMEM((HG_HEADS, HG_DK, HG_DK), F32),
            pltpu.VMEM((tile, D_MODEL), F32),
            pltpu.VMEM((tile, D_MODEL), BF16),
            pltpu.VMEM((tile, D_MODEL), F32),
            pltpu.VMEM((tile, D_FF), BF16),
            pltpu.VMEM((tile, D_MODEL), F32),
        ],
        compiler_params=pltpu.CompilerParams(
            dimension_semantics=("arbitrary",), vmem_limit_bytes=PROMPT_VMEM_LIMIT_BYTES),
        name="prompt_layer",
    )(x, *consts)


def _mixer_sample_kernel(x_ref, mod_ref, s_in_ref, cbuf_in_ref, lb_ref, w_in_ref, b_in_ref, hgw_ref,
                         cw_ref, cb_ref, gng_ref, gnb_ref, gmat_ref, w_out_ref, b_out_ref,
                         lng_ref, lnb_ref,
                         x1_ref, s_out_ref, cbuf_out_ref,
                         qe_s, ke_s, kd_s, v_s, d_s, o_s, *, seqs):
    rows_n = seqs * DEC_SEQ
    x = x_ref[...]
    sh1 = mod_ref[:, 0:D_MODEL]
    sc1 = mod_ref[:, D_MODEL:2 * D_MODEL]
    g1 = mod_ref[:, 2 * D_MODEL:3 * D_MODEL]
    hb = (x * (1.0 + sc1) + sh1).astype(BF16).reshape(rows_n, D_MODEL)
    q, k, lf, v, gate, u = _in_projection(hb, _lower_bound(lb_ref), w_in_ref, b_in_ref)

    slab = lambda a, t: a[t * seqs:(t + 1) * seqs, :]
    b = [slab(lf, 0)]
    for t in range(1, DEC_SEQ):
        b.append(b[-1] + slab(lf, t))
    b_last = b[-1]
    parts = _split3(jnp.exp(b_last)) + (jnp.zeros_like(b_last),)

    def put(ref, t, val):
        for h in range(HG_HEADS):
            ref[h, t * seqs:(t + 1) * seqs, :] = val[:, h * HG_DK:(h + 1) * HG_DK]

    for t in range(DEC_SEQ):
        put(qe_s, t, slab(q, t) * jnp.exp(b[t]))
        put(ke_s, t, slab(k, t) * jnp.exp(-b[t]))
        put(kd_s, t, slab(k, t) * jnp.exp(b_last - b[t]))
        put(d_s, t, parts[t])
        put(v_s, t, slab(v, t))

    stride = seqs // SEQ_GROUP
    n_rows = SEQ_GROUP * DEC_SEQ
    member = lax.broadcasted_iota(jnp.int32, (n_rows, HG_DK), 0) % SEQ_GROUP
    lanes = lambda g: slice(g * HG_DK, (g + 1) * HG_DK)
    per_member = lambda a: jnp.concatenate(
        [jnp.where(member == g, a, 0.0) for g in range(SEQ_GROUP)], axis=1)
    ones_g = per_member(jnp.ones((n_rows, HG_DK), F32)).astype(BF16)
    ri = lax.broadcasted_iota(jnp.int32, (n_rows, n_rows), 0)
    ci = lax.broadcasted_iota(jnp.int32, (n_rows, n_rows), 1)
    causal = (ri // SEQ_GROUP >= ci // SEQ_GROUP) & (ri % SEQ_GROUP == ci % SEQ_GROUP)
    for s in range(stride):
        rows = pl.ds(s, n_rows, stride=stride)
        for h in range(HG_HEADS):
            qe_g = qe_s[h, rows, :].astype(BF16)
            ke_g = ke_s[h, rows, :].astype(BF16)
            kd_g = kd_s[h, rows, :].astype(BF16)
            d_g = d_s[h, rows, :].astype(BF16)
            v_g = v_s[h, rows, :]
            s_g = jnp.concatenate([s_in_ref[s + g * stride, h] for g in range(SEQ_GROUP)], axis=1)
            att = jnp.where(causal, _dot_nt(qe_g, ke_g), 0.0).astype(BF16)
            inter_all = _dot(qe_g, s_g.astype(BF16))
            inter = inter_all[:, lanes(0)]
            for g in range(1, SEQ_GROUP):
                inter = jnp.where(member == g, inter_all[:, lanes(g)], inter)
            o_s[h, rows, :] = _dot(att, v_g.astype(BF16)) + inter
            new = _dot_tn(d_g, ones_g) * s_g + _dot_tn(kd_g, per_member(v_g).astype(BF16))
            for g in range(SEQ_GROUP):
                s_out_ref[s + g * stride, h] = new[:, lanes(g)]
    hgw = hgw_ref[...]
    o_a = jnp.concatenate(
        [_head_norm(o_s[h], hgw, gate[:, h * HG_DK:(h + 1) * HG_DK]) for h in range(HG_HEADS)],
        axis=-1)

    full = lambda i: cbuf_in_ref[i] if i < CV_BUF else slab(u, i - CV_BUF)
    acc = [jnp.broadcast_to(cb_ref[...], (seqs, CV_WIDTH)) for _ in range(DEC_SEQ)]
    for i in range(CV_BUF + DEC_SEQ):
        f_i = full(i)
        for t in range(DEC_SEQ):
            if 0 <= i - t < CV_KERNEL:
                acc[t] = acc[t] + cw_ref[i - t:i - t + 1, :] * f_i
    for i in range(CV_BUF):
        cbuf_out_ref[i] = full(i + DEC_SEQ)
    o_b = _group_norm_swish(jnp.concatenate(acc, axis=0), gmat_ref, gng_ref[...], gnb_ref[...])

    mix = _out_projection(jnp.concatenate([o_a, o_b], axis=-1).astype(BF16), w_out_ref, b_out_ref)
    y = ALPHA * x + (1.0 + g1) * mix.reshape(DEC_SEQ, seqs, D_MODEL)
    x1_ref[...] = _layer_norm(y, lng_ref[...], lnb_ref[...])


def _mixer_sample(x, mod, s_in, cbuf_in, lb_logits, w_in, b_in, hgw, conv_w, conv_b, gn_g, gn_b,
                  gmat, w_out, b_out, ln_g, ln_b, *, seqs):
    n_seq = s_in.shape[0]
    rows = seqs * DEC_SEQ
    kernel = functools.partial(_mixer_sample_kernel, seqs=seqs)
    return pl.pallas_call(
        kernel,
        grid=(n_seq // seqs,),
        in_specs=[
            pl.BlockSpec((DEC_SEQ, seqs, D_MODEL), lambda i: (0, i, 0)),
            pl.BlockSpec((seqs, 3 * D_MODEL), lambda i: (i, 0)),
            pl.BlockSpec((seqs, HG_HEADS, HG_DK, HG_DK), lambda i: (i, 0, 0, 0)),
            pl.BlockSpec((CV_BUF, seqs, CV_WIDTH), lambda i: (0, i, 0)),
            _const_spec(lb_logits.shape),
            _const_spec(w_in.shape),
            _const_spec(b_in.shape),
            _const_spec(hgw.shape),
            _const_spec(conv_w.shape),
            _const_spec(conv_b.shape),
            _const_spec(gn_g.shape),
            _const_spec(gn_b.shape),
            _const_spec(gmat.shape),
            _const_spec(w_out.shape),
            _const_spec(b_out.shape),
            _const_spec(ln_g.shape),
            _const_spec(ln_b.shape),
        ],
        out_specs=[
            pl.BlockSpec((DEC_SEQ, seqs, D_MODEL), lambda i: (0, i, 0)),
            pl.BlockSpec((seqs, HG_HEADS, HG_DK, HG_DK), lambda i: (i, 0, 0, 0)),
            pl.BlockSpec((CV_BUF, seqs, CV_WIDTH), lambda i: (0, i, 0)),
        ],
        out_shape=[
            jax.ShapeDtypeStruct((DEC_SEQ, n_seq, D_MODEL), F32),
            jax.ShapeDtypeStruct(s_in.shape, F32),
            jax.ShapeDtypeStruct(cbuf_in.shape, F32),
        ],
        scratch_shapes=[pltpu.VMEM((HG_HEADS, rows, HG_DK), F32)] * 6,
        compiler_params=pltpu.CompilerParams(
            dimension_semantics=("arbitrary",), vmem_limit_bytes=VMEM_LIMIT_BYTES),
        name="mixer_sample",
    )(x, mod, s_in, cbuf_in, lb_logits, w_in, b_in, hgw, conv_w, conv_b, gn_g, gn_b, gmat,
      w_out, b_out, ln_g, ln_b)


def _mlp_kernel(x_ref, sh_ref, sc_ref, g_ref, w_up_ref, b_up_ref, w_down_ref, b_down_ref,
                lng_ref, lnb_ref, o_ref):
    x = x_ref[...]
    hb = (x * (1.0 + sc_ref[...]) + sh_ref[...]).astype(BF16).reshape(-1, D_MODEL)
    ff = jnp.broadcast_to(b_down_ref[...], hb.shape)
    for c0 in range(0, D_FF, FF_CHUNK):
        cs = slice(c0, c0 + FF_CHUNK)
        a = jnp.maximum(_dot(hb, w_up_ref[:, cs]) + b_up_ref[:, cs], 0.0)
        ff = ff + _dot((a * a).astype(BF16), w_down_ref[cs, :])
    y = ALPHA * x + (1.0 + g_ref[...]) * ff.reshape(x.shape)
    o_ref[...] = _layer_norm(y, lng_ref[...], lnb_ref[...])


def _mlp_sample(x, mod, w_up, b_up, w_down, b_down, ln_g, ln_b):
    x_spec = pl.BlockSpec(x.shape, lambda i: (0, 0, 0))
    mod_specs = [pl.BlockSpec((x.shape[1], D_MODEL), functools.partial(lambda i, r: (0, r), r=r))
                 for r in (3, 4, 5)]
    return pl.pallas_call(
        _mlp_kernel,
        grid=(1,),
        in_specs=[x_spec] + mod_specs + [
            _const_spec(w_up.shape),
            _const_spec(b_up.shape),
            _const_spec(w_down.shape),
            _const_spec(b_down.shape),
            _const_spec(ln_g.shape),
            _const_spec(ln_b.shape),
        ],
        out_specs=x_spec,
        out_shape=jax.ShapeDtypeStruct(x.shape, F32),
        compiler_params=pltpu.CompilerParams(
            dimension_semantics=("arbitrary",), vmem_limit_bytes=VMEM_LIMIT_BYTES),
        name="mlp_sample",
    )(x, mod, mod, mod, w_up, b_up, w_down, b_down, ln_g, ln_b)


PROMPT_TILE = 512
SAMPLE_SEQS = 16


def kernel(x_prompt, x_sample, c_prompt, c_sample, state_hgrn, state_conv, lb_logits, w_in, b_in,
           hg_norm_w, conv_w, conv_b, gn_g, gn_b, w_out, b_out, ln1_g, ln1_b, w_up, b_up, w_down,
           b_down, ln2_g, ln2_b, w_ada, b_ada):
    assert w_in.shape[0] == DEPTH
    batch, seq, _ = x_prompt.shape
    dec_batch, dec_seq, _ = x_sample.shape
    assert dec_seq == DEC_SEQ

    group = jnp.arange(MXU_WIDTH, dtype=jnp.int32) // (CV_WIDTH // CV_GROUPS)
    gmat = jnp.where(group[:, None] == group[None, :], CV_GROUPS / CV_WIDTH, 0.0).astype(BF16)

    xp = x_prompt
    xs = jnp.transpose(x_sample, (1, 0, 2))
    hp, cp, hs, cs = [], [], [], []
    for l in range(DEPTH):
        row = lambda a: a[l][None, :]
        w_in_b, w_out_b = _to_bf16(w_in[l]), _to_bf16(w_out[l])
        w_up_b, w_down_b = _to_bf16(w_up[l]), _to_bf16(w_down[l])

        mod_p, mod_s = _adaln(c_prompt, c_sample, w_ada[l], row(b_ada))
        mod_p = mod_p.reshape(batch, 6, D_MODEL)

        mixer_params = (lb_logits, w_in_b, row(b_in), row(hg_norm_w), conv_w[l], row(conv_b),
                        row(gn_g), row(gn_b), gmat, w_out_b, row(b_out), row(ln1_g), row(ln1_b))
        mlp_params = (w_up_b, row(b_up), w_down_b, row(b_down), row(ln2_g), row(ln2_b))
        xp, sp, bp = _prompt_layer(xp, mod_p, *mixer_params, *mlp_params, tile=PROMPT_TILE)
        xs, ss, bs = _mixer_sample(xs, mod_s, state_hgrn[l], jnp.transpose(state_conv[l], (1, 0, 2)),
                                   *mixer_params, seqs=SAMPLE_SEQS)
        xs = _mlp_sample(xs, mod_s, *mlp_params)

        hp.append(sp); cp.append(bp); hs.append(ss); cs.append(jnp.transpose(bs, (1, 0, 2)))

    return (xp, jnp.transpose(xs, (1, 0, 2)), jnp.stack(hp), jnp.stack(cp), jnp.stack(hs),
            jnp.stack(cs))
```

```python
import functools

import jax
import jax.numpy as jnp
from jax import lax
from jax.experimental import pallas as pl
from jax.experimental.pallas import tpu as pltpu

F32 = jnp.float32
BF16 = jnp.bfloat16

D_MODEL = 1024
HG_WIDTH = 512
CV_WIDTH = 512
HG_HEADS = 4
HG_DK = 128
CV_GROUPS = 8
CV_KERNEL = 31
CV_BUF = CV_KERNEL - 1
D_FF = 4 * D_MODEL
N_IN = 4 * HG_WIDTH + 2 * CV_WIDTH
DEPTH = 1
ALPHA = (2.0 * DEPTH) ** 0.25
EPS = 1e-5

CHUNK = 64
SUBLANES = 8
LANES = 128
MXU_WIDTH = 256
CONV_PAD = 32
CONV_ROWS = 128
CONV_CHAINS = 2
UP_CHUNK = 512
DOWN_CHUNK = 256
FF_CHUNK = 1024
DEC_SEQ = 4
SEQ_GROUP = 4

VMEM_LIMIT_BYTES = 52 * 1024 * 1024
PROMPT_VMEM_LIMIT_BYTES = 60 * 1024 * 1024
CAST_BLOCK_BYTES = 4 * 1024 * 1024
ADALN_COLS = 1024

_NT = (((1,), (1,)), ((), ()))
_TN = (((0,), (0,)), ((), ()))


def _dot(a, b):
    return jnp.dot(a, b, preferred_element_type=F32)


def _dot_nt(a, b):
    return lax.dot_general(a, b, _NT, preferred_element_type=F32)


def _dot_tn(a, b):
    return lax.dot_general(a, b, _TN, preferred_element_type=F32)


def _silu(x):
    return x * jax.nn.sigmoid(x)


def _split3(x):
    hi = x.astype(BF16).astype(F32)
    r = x - hi
    mid = r.astype(BF16).astype(F32)
    lo = (r - mid).astype(BF16).astype(F32)
    return hi, mid, lo


def _layer_norm(y, g, b):
    mu = jnp.mean(y, axis=-1, keepdims=True)
    yc = y - mu
    var = jnp.mean(yc * yc, axis=-1, keepdims=True)
    return yc * lax.rsqrt(var + EPS) * g + b


def _adaln_kernel(cp_ref, cs_ref, w_ref, b_ref, op_ref, os_ref):
    a = _silu(jnp.concatenate([cp_ref[...], cs_ref[...]], axis=0))
    a_hi = a.astype(BF16)
    a_lo = (a - a_hi.astype(F32)).astype(BF16)
    w = w_ref[...].astype(BF16)
    mod = _dot(a_hi, w) + _dot(a_lo, w) + b_ref[...]
    n_prompt = cp_ref.shape[0]
    op_ref[...] = mod[:n_prompt]
    os_ref[...] = mod[n_prompt:]


def _adaln(c_prompt, c_sample, w_ada, b_ada):
    n = w_ada.shape[1]
    tn = ADALN_COLS
    rows = lambda c: pl.BlockSpec((c.shape[0], D_MODEL), lambda j: (0, 0))
    cols = lambda c: pl.BlockSpec((c.shape[0], tn), lambda j: (0, j))
    return pl.pallas_call(
        _adaln_kernel,
        grid=(n // tn,),
        in_specs=[
            rows(c_prompt),
            rows(c_sample),
            pl.BlockSpec((D_MODEL, tn), lambda j: (0, j)),
            pl.BlockSpec((1, tn), lambda j: (0, j)),
        ],
        out_specs=[cols(c_prompt), cols(c_sample)],
        out_shape=[jax.ShapeDtypeStruct((c.shape[0], n), F32) for c in (c_prompt, c_sample)],
        compiler_params=pltpu.CompilerParams(
            dimension_semantics=("arbitrary",), vmem_limit_bytes=VMEM_LIMIT_BYTES),
        name="adaln",
    )(c_prompt, c_sample, w_ada, b_ada)


def _cast_kernel(w_ref, o_ref):
    o_ref[...] = w_ref[...].astype(o_ref.dtype)


def _to_bf16(w):
    rows, cols = w.shape
    block_rows = rows
    while block_rows * cols * w.dtype.itemsize > CAST_BLOCK_BYTES and block_rows % 16 == 0:
        block_rows //= 2
    spec = pl.BlockSpec((block_rows, cols), lambda i: (i, 0))
    return pl.pallas_call(
        _cast_kernel,
        grid=(rows // block_rows,),
        in_specs=[spec],
        out_specs=spec,
        out_shape=jax.ShapeDtypeStruct(w.shape, BF16),
        compiler_params=pltpu.CompilerParams(
            dimension_semantics=("arbitrary",), vmem_limit_bytes=VMEM_LIMIT_BYTES),
        name="to_bf16",
    )(w)


def _lower_bound(lb_ref):
    lg = lb_ref[...]
    e = jnp.exp(lg - jnp.max(lg, axis=0, keepdims=True))
    p = e / jnp.sum(e, axis=0, keepdims=True)
    return jnp.sum(p[0:DEPTH, :], axis=0, keepdims=True)


def _section(hb, w_in_ref, b_in_ref, i, width):
    return _dot(hb, w_in_ref[:, i:i + width]) + b_in_ref[:, i:i + width]


def _in_projection(hb, lb, w_in_ref, b_in_ref):
    sec = functools.partial(_section, hb, w_in_ref, b_in_ref)

    q = _silu(sec(0, HG_WIDTH))
    f = lb + (1.0 - lb) * jax.nn.sigmoid(sec(HG_WIDTH, HG_WIDTH))
    v = sec(2 * HG_WIDTH, HG_WIDTH)
    gate = _silu(sec(3 * HG_WIDTH, HG_WIDTH))
    za = sec(4 * HG_WIDTH, CV_WIDTH)
    zb = sec(4 * HG_WIDTH + CV_WIDTH, CV_WIDTH)
    return q, 1.0 - f, jnp.log(f), v, gate, za * jax.nn.sigmoid(zb)


def _head_norm(o, hgw, gate):
    ms = jnp.mean(o * o, axis=-1, keepdims=True)
    return o * lax.rsqrt(ms + EPS) * hgw * gate


def _group_norm_swish(uc, gmat_ref, gn_g, gn_b):
    width = gmat_ref.shape[1]

    def per_group(a, which):
        ab = a.astype(BF16)
        return jnp.concatenate([_dot(ab[:, c0:c0 + width], gmat_ref[which])
                                for c0 in range(0, ab.shape[1], width)], axis=-1)

    d0 = uc - per_group(uc, 0)
    d = d0 - per_group(d0, 1)
    var = per_group(d * d, 1)
    return _silu(d * lax.rsqrt(var + EPS) * gn_g + gn_b)


def _out_projection(mix_bf16, w_out_ref, b_out_ref):
    return _dot(mix_bf16, w_out_ref[...]) + b_out_ref[...]


def _prompt_kernel(*refs, tile, tiles_per_seq, n_tiles):
    i = pl.program_id(0)
    step = functools.partial(_prompt_step, *refs, tile=tile, tiles_per_seq=tiles_per_seq,
                             n_tiles=n_tiles)
    pl.when(i < n_tiles)(functools.partial(step, run_mixer=True))
    pl.when(i == n_tiles)(functools.partial(step, run_mixer=False))


def _prompt_step(x_ref, mod_ref, lb_ref, w_in_ref, b_in_ref, hgw_ref, cw_ref, cb_ref,
                 gng_ref, gnb_ref, gmat_ref, w_out_ref, b_out_ref, ln1g_ref, ln1b_ref,
                 w_up_ref, b_up_ref, w_down_ref, b_down_ref, ln2g_ref, ln2b_ref,
                 y_ref, s_out_ref, cbuf_out_ref,
                 q_s, k_s, lf_s, v_s, g_s, ubuf, ush, uc_s, mix_s, st_s,
                 x1_s, hb2_s, res2_s, act_s, ff_s, *, tile, tiles_per_seq, n_tiles,
                 run_mixer):
    i = pl.program_id(0)
    mixer_tile = jnp.minimum(i, n_tiles - 1)
    b_mix = mixer_tile // tiles_per_seq
    t = mixer_tile % tiles_per_seq
    b_mlp = jnp.maximum(i - 1, 0) // tiles_per_seq

    def mlp_prologue():
        x1_prev = x1_s[...]
        sh2 = mod_ref[b_mlp, 3:4, :]
        sc2 = mod_ref[b_mlp, 4:5, :]
        hb2_s[...] = (x1_prev * (1.0 + sc2) + sh2).astype(BF16)
        res2_s[...] = ALPHA * x1_prev

    def mlp_epilogue():
        g2 = mod_ref[b_mlp, 5:6, :]
        y_ref[0] = _layer_norm(res2_s[...] + (1.0 + g2) * ff_s[...], ln2g_ref[...], ln2b_ref[...])

    def mlp_up(c0):
        cs = slice(c0, c0 + UP_CHUNK)
        a = jnp.maximum(_dot(hb2_s[...], w_up_ref[:, cs]) + b_up_ref[:, cs], 0.0)
        act_s[:, cs] = (a * a).astype(BF16)

    def mlp_down(c0):
        cs = slice(c0, c0 + DOWN_CHUNK)
        ff_s[:, cs] = _dot(act_s[...], w_down_ref[:, cs]) + b_down_ref[:, cs]

    up_chunks = list(range(0, D_FF, UP_CHUNK))
    down_chunks = list(range(0, D_MODEL, DOWN_CHUNK))

    def next_up():
        if up_chunks:
            mlp_up(up_chunks.pop(0))

    def next_down():
        if down_chunks:
            mlp_down(down_chunks.pop(0))

    if not run_mixer:
        mlp_prologue()
        while up_chunks:
            next_up()
        while down_chunks:
            next_down()
        mlp_epilogue()
        return

    @pl.when(i == 0)
    def _():
        x1_s[...] = jnp.zeros_like(x1_s)

    @pl.when(t == 0)
    def _():
        st_s[...] = jnp.zeros_like(st_s)
        ubuf[0:CONV_PAD, :] = jnp.zeros((CONV_PAD, CV_WIDTH), F32)

    mlp_prologue()

    x = x_ref[0]
    sh1 = mod_ref[b_mix, 0:1, :]
    sc1 = mod_ref[b_mix, 1:2, :]
    g1 = mod_ref[b_mix, 2:3, :]
    hb = (x * (1.0 + sc1) + sh1).astype(BF16)
    sec = functools.partial(_section, hb, w_in_ref, b_in_ref)
    lb = _lower_bound(lb_ref)

    ubuf[CONV_PAD:CONV_PAD + tile, :] = (sec(4 * HG_WIDTH, CV_WIDTH)
                                         * jax.nn.sigmoid(sec(4 * HG_WIDTH + CV_WIDTH, CV_WIDTH)))
    first_off = CONV_PAD - CV_BUF
    shifted_rows = ush.shape[1]

    def conv_piece(r0, cs):
        accs = [jnp.broadcast_to(cb_ref[:, cs], (CONV_ROWS, LANES))] + [None] * (CONV_CHAINS - 1)
        for phase in range(SUBLANES):
            taps = [j for j in range(CV_KERNEL) if (first_off + j) % SUBLANES == phase]
            span = (first_off + taps[-1]) // SUBLANES * SUBLANES
            rows = slice(r0, r0 + span + CONV_ROWS)
            win = ubuf[rows, cs] if phase == 0 else ush[phase - 1, rows, :]
            c = phase % CONV_CHAINS
            for j in taps:
                a0 = first_off + j - phase
                term = cw_ref[j:j + 1, cs] * win[a0:a0 + CONV_ROWS, :]
                accs[c] = term if accs[c] is None else accs[c] + term
        uc_s[r0:r0 + CONV_ROWS, cs] = functools.reduce(lambda a, b: a + b, accs)

    def conv_pieces(n):
        cs = slice(n * LANES, (n + 1) * LANES)
        for phase in range(1, SUBLANES):
            ush[phase - 1] = ubuf[phase:phase + shifted_rows, cs]
        for r0 in range(0, tile, CONV_ROWS):
            conv_piece(r0, cs)

    next_up()
    q_s[...] = _silu(sec(0, HG_WIDTH))
    conv_pieces(0)
    next_up()
    f = lb + (1.0 - lb) * jax.nn.sigmoid(sec(HG_WIDTH, HG_WIDTH))
    k_s[...] = 1.0 - f
    lf_s[...] = jnp.log(f)
    conv_pieces(1)
    next_up()
    v_s[...] = sec(2 * HG_WIDTH, HG_WIDTH).astype(BF16)
    conv_pieces(2)
    next_up()
    g_s[...] = _silu(sec(3 * HG_WIDTH, HG_WIDTH))
    conv_pieces(3)

    row = lax.broadcasted_iota(jnp.int32, (CHUNK, HG_WIDTH), 0)
    causal = (lax.broadcasted_iota(jnp.int32, (CHUNK, CHUNK), 0)
              >= lax.broadcasted_iota(jnp.int32, (CHUNK, CHUNK), 1))
    hgw = hgw_ref[...]
    for c in range(tile // CHUNK):
        next_up()
        rows = slice(c * CHUNK, (c + 1) * CHUNK)
        b = lf_s[rows, :]
        shift = 1
        while shift < CHUNK:
            b = b + jnp.where(row >= shift, pltpu.roll(b, shift, axis=0), 0.0)
            shift *= 2
        b_last = b[CHUNK - 1:CHUNK, :]
        kc = k_s[rows, :]
        qe = (q_s[rows, :] * jnp.exp(b)).astype(BF16)
        ke = (kc * jnp.exp(-b)).astype(BF16)
        kd = (kc * jnp.exp(b_last - b)).astype(BF16)
        decay = jnp.exp(b_last)
        vb = v_s[rows, :]
        for h in range(HG_HEADS):
            hs = slice(h * HG_DK, (h + 1) * HG_DK)
            att = jnp.where(causal, _dot_nt(qe[:, hs], ke[:, hs]), 0.0).astype(BF16)
            st = st_s[h]
            o = _dot(att, vb[:, hs]) + _dot_nt(qe[:, hs], st.astype(BF16))
            st_s[h] = st * decay[:, hs] + _dot_tn(vb[:, hs], kd[:, hs])
            mix_s[rows, hs] = _head_norm(o, hgw, g_s[rows, hs]).astype(BF16)
    while up_chunks:
        next_up()

    next_down()
    mix_s[:, HG_WIDTH:] = _group_norm_swish(
        uc_s[...], gmat_ref, gng_ref[...], gnb_ref[...]).astype(BF16)
    ubuf[0:CONV_PAD, :] = ubuf[tile:tile + CONV_PAD, :]
    next_down()
    mix = _out_projection(mix_s[...], w_out_ref, b_out_ref)
    x1_s[...] = _layer_norm(ALPHA * x + (1.0 + g1) * mix, ln1g_ref[...], ln1b_ref[...])
    while down_chunks:
        next_down()
    mlp_epilogue()

    @pl.when(t == tiles_per_seq - 1)
    def _():
        for h in range(HG_HEADS):
            s_out_ref[0, h] = st_s[h].T
        cbuf_out_ref[0] = ubuf[CONV_PAD - CV_BUF:CONV_PAD, :]


def _const_spec(shape):
    zeros = (0,) * len(shape)
    return pl.BlockSpec(shape, lambda *_: zeros, pipeline_mode=pl.Buffered(1))


def _prompt_layer(x, mod, lb_logits, w_in, b_in, hgw, conv_w, conv_b, gn_g, gn_b, gmat,
                  w_out, b_out, ln1_g, ln1_b, w_up, b_up, w_down, b_down, ln2_g, ln2_b, *, tile):
    batch, seq, _ = x.shape
    tiles_per_seq = seq // tile
    n_tiles = batch * tiles_per_seq
    kernel = functools.partial(_prompt_kernel, tile=tile, tiles_per_seq=tiles_per_seq,
                               n_tiles=n_tiles)

    def mixer_block(i):
        m = jnp.minimum(i, n_tiles - 1)
        return m // tiles_per_seq, m % tiles_per_seq

    def mlp_block(i):
        m = jnp.maximum(i - 1, 0)
        return m // tiles_per_seq, m % tiles_per_seq

    consts = (mod, lb_logits, w_in, b_in, hgw, conv_w, conv_b, gn_g, gn_b, gmat, w_out, b_out,
              ln1_g, ln1_b, w_up, b_up, w_down, b_down, ln2_g, ln2_b)
    return pl.pallas_call(
        kernel,
        grid=(n_tiles + 1,),
        in_specs=[pl.BlockSpec((1, tile, D_MODEL), lambda i: (*mixer_block(i), 0))]
        + [_const_spec(a.shape) for a in consts],
        out_specs=[
            pl.BlockSpec((1, tile, D_MODEL), lambda i: (*mlp_block(i), 0)),
            pl.BlockSpec((1, HG_HEADS, HG_DK, HG_DK), lambda i: (mixer_block(i)[0], 0, 0, 0)),
            pl.BlockSpec((1, CV_BUF, CV_WIDTH), lambda i: (mixer_block(i)[0], 0, 0)),
        ],
        out_shape=[
            jax.ShapeDtypeStruct((batch, seq, D_MODEL), F32),
            jax.ShapeDtypeStruct((batch, HG_HEADS, HG_DK, HG_DK), F32),
            jax.ShapeDtypeStruct((batch, CV_BUF, CV_WIDTH), F32),
        ],
        scratch_shapes=[
            pltpu.VMEM((tile, HG_WIDTH), F32),
            pltpu.VMEM((tile, HG_WIDTH), F32),
            pltpu.VMEM((tile, HG_WIDTH), F32),
            pltpu.VMEM((tile, HG_WIDTH), BF16),
            pltpu.VMEM((tile, HG_WIDTH), F32),
            pltpu.VMEM((CONV_PAD + tile, CV_WIDTH), F32),
            pltpu.VMEM((SUBLANES - 1, CONV_PAD + tile - SUBLANES, LANES), F32),
            pltpu.VMEM((tile, CV_WIDTH), F32),
            pltpu.VMEM((tile, D_MODEL), BF16),
            pltpu.VMEM((HG_HEADS, HG_DK, HG_DK), F32),
            pltpu.VMEM((tile, D_MODEL), F32),
            pltpu.VMEM((tile, D_MODEL), BF16),
            pltpu.VMEM((tile, D_MODEL), F32),
            pltpu.VMEM((tile, D_FF), BF16),
            pltpu.VMEM((tile, D_MODEL), F32),
        ],
        compiler_params=pltpu.CompilerParams(
            dimension_semantics=("arbitrary",), vmem_limit_bytes=PROMPT_VMEM_LIMIT_BYTES),
        name="prompt_layer",
    )(x, *consts)


def _mixer_sample_kernel(x_ref, mod_ref, s_in_ref, cbuf_in_ref, lb_ref, w_in_ref, b_in_ref, hgw_ref,
                         cw_ref, cb_ref, gng_ref, gnb_ref, gmat_ref, w_out_ref, b_out_ref,
                         lng_ref, lnb_ref,
                         x1_ref, s_out_ref, cbuf_out_ref,
                         qe_s, ke_s, kd_s, v_s, d_s, o_s, *, seqs):
    rows_n = seqs * DEC_SEQ
    x = x_ref[...]
    sh1 = mod_ref[:, 0:D_MODEL]
    sc1 = mod_ref[:, D_MODEL:2 * D_MODEL]
    g1 = mod_ref[:, 2 * D_MODEL:3 * D_MODEL]
    hb = (x * (1.0 + sc1) + sh1).astype(BF16).reshape(rows_n, D_MODEL)
    q, k, lf, v, gate, u = _in_projection(hb, _lower_bound(lb_ref), w_in_ref, b_in_ref)

    slab = lambda a, t: a[t * seqs:(t + 1) * seqs, :]
    b = [slab(lf, 0)]
    for t in range(1, DEC_SEQ):
        b.append(b[-1] + slab(lf, t))
    b_last = b[-1]
    parts = _split3(jnp.exp(b_last)) + (jnp.zeros_like(b_last),)

    def put(ref, t, val):
        for h in range(HG_HEADS):
            ref[h, t * seqs:(t + 1) * seqs, :] = val[:, h * HG_DK:(h + 1) * HG_DK]

    for t in range(DEC_SEQ):
        put(qe_s, t, slab(q, t) * jnp.exp(b[t]))
        put(ke_s, t, slab(k, t) * jnp.exp(-b[t]))
        put(kd_s, t, slab(k, t) * jnp.exp(b_last - b[t]))
        put(d_s, t, parts[t])
        put(v_s, t, slab(v, t))

    stride = seqs // SEQ_GROUP
    n_rows = SEQ_GROUP * DEC_SEQ
    member = lax.broadcasted_iota(jnp.int32, (n_rows, HG_DK), 0) % SEQ_GROUP
    lanes = lambda g: slice(g * HG_DK, (g + 1) * HG_DK)
    per_member = lambda a: jnp.concatenate(
        [jnp.where(member == g, a, 0.0) for g in range(SEQ_GROUP)], axis=1)
    ones_g = per_member(jnp.ones((n_rows, HG_DK), F32)).astype(BF16)
    ri = lax.broadcasted_iota(jnp.int32, (n_rows, n_rows), 0)
    ci = lax.broadcasted_iota(jnp.int32, (n_rows, n_rows), 1)
    causal = (ri // SEQ_GROUP >= ci // SEQ_GROUP) & (ri % SEQ_GROUP == ci % SEQ_GROUP)
    for s in range(stride):
        rows = pl.ds(s, n_rows, stride=stride)
        for h in range(HG_HEADS):
            qe_g = qe_s[h, rows, :].astype(BF16)
            ke_g = ke_s[h, rows, :].astype(BF16)
            kd_g = kd_s[h, rows, :].astype(BF16)
            d_g = d_s[h, rows, :].astype(BF16)
            v_g = v_s[h, rows, :]
            s_g = jnp.concatenate([s_in_ref[s + g * stride, h] for g in range(SEQ_GROUP)], axis=1)
            att = jnp.where(causal, _dot_nt(qe_g, ke_g), 0.0).astype(BF16)
            inter_all = _dot(qe_g, s_g.astype(BF16))
            inter = inter_all[:, lanes(0)]
            for g in range(1, SEQ_GROUP):
                inter = jnp.where(member == g, inter_all[:, lanes(g)], inter)
            o_s[h, rows, :] = _dot(att, v_g.astype(BF16)) + inter
            new = _dot_tn(d_g, ones_g) * s_g + _dot_tn(kd_g, per_member(v_g).astype(BF16))
            for g in range(SEQ_GROUP):
                s_out_ref[s + g * stride, h] = new[:, lanes(g)]
    hgw = hgw_ref[...]
    o_a = jnp.concatenate(
        [_head_norm(o_s[h], hgw, gate[:, h * HG_DK:(h + 1) * HG_DK]) for h in range(HG_HEADS)],
        axis=-1)

    full = lambda i: cbuf_in_ref[i] if i < CV_BUF else slab(u, i - CV_BUF)
    acc = [jnp.broadcast_to(cb_ref[...], (seqs, CV_WIDTH)) for _ in range(DEC_SEQ)]
    for i in range(CV_BUF + DEC_SEQ):
        f_i = full(i)
        for t in range(DEC_SEQ):
            if 0 <= i - t < CV_KERNEL:
                acc[t] = acc[t] + cw_ref[i - t:i - t + 1, :] * f_i
    for i in range(CV_BUF):
        cbuf_out_ref[i] = full(i + DEC_SEQ)
    o_b = _group_norm_swish(jnp.concatenate(acc, axis=0), gmat_ref, gng_ref[...], gnb_ref[...])

    mix = _out_projection(jnp.concatenate([o_a, o_b], axis=-1).astype(BF16), w_out_ref, b_out_ref)
    y = ALPHA * x + (1.0 + g1) * mix.reshape(DEC_SEQ, seqs, D_MODEL)
    x1_ref[...] = _layer_norm(y, lng_ref[...], lnb_ref[...])


def _mixer_sample(x, mod, s_in, cbuf_in, lb_logits, w_in, b_in, hgw, conv_w, conv_b, gn_g, gn_b,
                  gmat, w_out, b_out, ln_g, ln_b, *, seqs):
    n_seq = s_in.shape[0]
    rows = seqs * DEC_SEQ
    kernel = functools.partial(_mixer_sample_kernel, seqs=seqs)
    return pl.pallas_call(
        kernel,
        grid=(n_seq // seqs,),
        in_specs=[
            pl.BlockSpec((DEC_SEQ, seqs, D_MODEL), lambda i: (0, i, 0)),
            pl.BlockSpec((seqs, 3 * D_MODEL), lambda i: (i, 0)),
            pl.BlockSpec((seqs, HG_HEADS, HG_DK, HG_DK), lambda i: (i, 0, 0, 0)),
            pl.BlockSpec((CV_BUF, seqs, CV_WIDTH), lambda i: (0, i, 0)),
            _const_spec(lb_logits.shape),
            _const_spec(w_in.shape),
            _const_spec(b_in.shape),
            _const_spec(hgw.shape),
            _const_spec(conv_w.shape),
            _const_spec(conv_b.shape),
            _const_spec(gn_g.shape),
            _const_spec(gn_b.shape),
            _const_spec(gmat.shape),
            _const_spec(w_out.shape),
            _const_spec(b_out.shape),
            _const_spec(ln_g.shape),
            _const_spec(ln_b.shape),
        ],
        out_specs=[
            pl.BlockSpec((DEC_SEQ, seqs, D_MODEL), lambda i: (0, i, 0)),
            pl.BlockSpec((seqs, HG_HEADS, HG_DK, HG_DK), lambda i: (i, 0, 0, 0)),
            pl.BlockSpec((CV_BUF, seqs, CV_WIDTH), lambda i: (0, i, 0)),
        ],
        out_shape=[
            jax.ShapeDtypeStruct((DEC_SEQ, n_seq, D_MODEL), F32),
            jax.ShapeDtypeStruct(s_in.shape, F32),
            jax.ShapeDtypeStruct(cbuf_in.shape, F32),
        ],
        scratch_shapes=[pltpu.VMEM((HG_HEADS, rows, HG_DK), F32)] * 6,
        compiler_params=pltpu.CompilerParams(
            dimension_semantics=("arbitrary",), vmem_limit_bytes=VMEM_LIMIT_BYTES),
        name="mixer_sample",
    )(x, mod, s_in, cbuf_in, lb_logits, w_in, b_in, hgw, conv_w, conv_b, gn_g, gn_b, gmat,
      w_out, b_out, ln_g, ln_b)


def _mlp_kernel(x_ref, sh_ref, sc_ref, g_ref, w_up_ref, b_up_ref, w_down_ref, b_down_ref,
                lng_ref, lnb_ref, o_ref):
    x = x_ref[...]
    hb = (x * (1.0 + sc_ref[...]) + sh_ref[...]).astype(BF16).reshape(-1, D_MODEL)
    ff = jnp.broadcast_to(b_down_ref[...], hb.shape)
    for c0 in range(0, D_FF, FF_CHUNK):
        cs = slice(c0, c0 + FF_CHUNK)
        a = jnp.maximum(_dot(hb, w_up_ref[:, cs]) + b_up_ref[:, cs], 0.0)
        ff = ff + _dot((a * a).astype(BF16), w_down_ref[cs, :])
    y = ALPHA * x + (1.0 + g_ref[...]) * ff.reshape(x.shape)
    o_ref[...] = _layer_norm(y, lng_ref[...], lnb_ref[...])


def _mlp_sample(x, mod, w_up, b_up, w_down, b_down, ln_g, ln_b):
    x_spec = pl.BlockSpec(x.shape, lambda i: (0, 0, 0))
    mod_specs = [pl.BlockSpec((x.shape[1], D_MODEL), functools.partial(lambda i, r: (0, r), r=r))
                 for r in (3, 4, 5)]
    return pl.pallas_call(
        _mlp_kernel,
        grid=(1,),
        in_specs=[x_spec] + mod_specs + [
            _const_spec(w_up.shape),
            _const_spec(b_up.shape),
            _const_spec(w_down.shape),
            _const_spec(b_down.shape),
            _const_spec(ln_g.shape),
            _const_spec(ln_b.shape),
        ],
        out_specs=x_spec,
        out_shape=jax.ShapeDtypeStruct(x.shape, F32),
        compiler_params=pltpu.CompilerParams(
            dimension_semantics=("arbitrary",), vmem_limit_bytes=VMEM_LIMIT_BYTES),
        name="mlp_sample",
    )(x, mod, mod, mod, w_up, b_up, w_down, b_down, ln_g, ln_b)


PROMPT_TILE = 512
SAMPLE_SEQS = 16


def kernel(x_prompt, x_sample, c_prompt, c_sample, state_hgrn, state_conv, lb_logits, w_in, b_in,
           hg_norm_w, conv_w, conv_b, gn_g, gn_b, w_out, b_out, ln1_g, ln1_b, w_up, b_up, w_down,
           b_down, ln2_g, ln2_b, w_ada, b_ada):
    assert w_in.shape[0] == DEPTH
    batch, seq, _ = x_prompt.shape
    dec_batch, dec_seq, _ = x_sample.shape
    assert dec_seq == DEC_SEQ

    group = jnp.arange(MXU_WIDTH, dtype=jnp.int32) // (CV_WIDTH // CV_GROUPS)
    first = jnp.arange(MXU_WIDTH, dtype=jnp.int32)[:, None] == group[None, :] * (CV_WIDTH // CV_GROUPS)
    gmat = jnp.stack([jnp.where(first, 1.0, 0.0),
                      jnp.where(group[:, None] == group[None, :], CV_GROUPS / CV_WIDTH, 0.0)]
                     ).astype(BF16)

    xp = x_prompt
    xs = jnp.transpose(x_sample, (1, 0, 2))
    hp, cp, hs, cs = [], [], [], []
    for l in range(DEPTH):
        row = lambda a: a[l][None, :]
        w_in_b, w_out_b = _to_bf16(w_in[l]), _to_bf16(w_out[l])
        w_up_b, w_down_b = _to_bf16(w_up[l]), _to_bf16(w_down[l])

        mod_p, mod_s = _adaln(c_prompt, c_sample, w_ada[l], row(b_ada))
        mod_p = mod_p.reshape(batch, 6, D_MODEL)

        mixer_params = (lb_logits, w_in_b, row(b_in), row(hg_norm_w), conv_w[l], row(conv_b),
                        row(gn_g), row(gn_b), gmat, w_out_b, row(b_out), row(ln1_g), row(ln1_b))
        mlp_params = (w_up_b, row(b_up), w_down_b, row(b_down), row(ln2_g), row(ln2_b))
        xp, sp, bp = _prompt_layer(xp, mod_p, *mixer_params, *mlp_params, tile=PROMPT_TILE)
        xs, ss, bs = _mixer_sample(xs, mod_s, state_hgrn[l], jnp.transpose(state_conv[l], (1, 0, 2)),
                                   *mixer_params, seqs=SAMPLE_SEQS)
        xs = _mlp_sample(xs, mod_s, *mlp_params)

        hp.append(sp); cp.append(bp); hs.append(ss); cs.append(jnp.transpose(bs, (1, 0, 2)))

    return (xp, jnp.transpose(xs, (1, 0, 2)), jnp.stack(hp), jnp.stack(cp), jnp.stack(hs),
            jnp.stack(cs))
```

```python
import functools

import jax
import jax.numpy as jnp
from jax import lax
from jax.experimental import pallas as pl
from jax.experimental.pallas import tpu as pltpu

F32 = jnp.float32
BF16 = jnp.bfloat16

D_MODEL = 1024
HG_WIDTH = 512
CV_WIDTH = 512
HG_HEADS = 4
HG_DK = 128
CV_GROUPS = 8
CV_KERNEL = 31
CV_BUF = CV_KERNEL - 1
D_FF = 4 * D_MODEL
N_IN = 4 * HG_WIDTH + 2 * CV_WIDTH
DEPTH = 1
ALPHA = (2.0 * DEPTH) ** 0.25
EPS = 1e-5

CHUNK = 64
SUBLANES = 8
LANES = 128
MXU_WIDTH = 256
CONV_PAD = 32
CONV_ROWS = 128
CONV_CHAINS = 2
UP_CHUNK = 512
DOWN_CHUNK = 256
FF_CHUNK = 1024
DEC_SEQ = 4
SEQ_GROUP = 4

VMEM_LIMIT_BYTES = 52 * 1024 * 1024
PROMPT_VMEM_LIMIT_BYTES = 60 * 1024 * 1024
CAST_BLOCK_BYTES = 4 * 1024 * 1024
ADALN_COLS = 1024

_NT = (((1,), (1,)), ((), ()))
_TN = (((0,), (0,)), ((), ()))


def _dot(a, b):
    return jnp.dot(a, b, preferred_element_type=F32)


def _dot_nt(a, b):
    return lax.dot_general(a, b, _NT, preferred_element_type=F32)


def _dot_tn(a, b):
    return lax.dot_general(a, b, _TN, preferred_element_type=F32)


def _silu(x):
    return x * jax.nn.sigmoid(x)


def _split3(x):
    hi = x.astype(BF16).astype(F32)
    r = x - hi
    mid = r.astype(BF16).astype(F32)
    lo = (r - mid).astype(BF16).astype(F32)
    return hi, mid, lo


def _layer_norm(y, g, b):
    mu = jnp.mean(y, axis=-1, keepdims=True)
    yc = y - mu
    var = jnp.mean(yc * yc, axis=-1, keepdims=True)
    return yc * lax.rsqrt(var + EPS) * g + b


def _adaln_kernel(cp_ref, cs_ref, w_ref, b_ref, op_ref, os_ref):
    a = _silu(jnp.concatenate([cp_ref[...], cs_ref[...]], axis=0))
    a_hi = a.astype(BF16)
    a_lo = (a - a_hi.astype(F32)).astype(BF16)
    w = w_ref[...].astype(BF16)
    mod = _dot(a_hi, w) + _dot(a_lo, w) + b_ref[...]
    n_prompt = cp_ref.shape[0]
    op_ref[...] = mod[:n_prompt]
    os_ref[...] = mod[n_prompt:]


def _adaln(c_prompt, c_sample, w_ada, b_ada):
    n = w_ada.shape[1]
    tn = ADALN_COLS
    rows = lambda c: pl.BlockSpec((c.shape[0], D_MODEL), lambda j: (0, 0))
    cols = lambda c: pl.BlockSpec((c.shape[0], tn), lambda j: (0, j))
    return pl.pallas_call(
        _adaln_kernel,
        grid=(n // tn,),
        in_specs=[
            rows(c_prompt),
            rows(c_sample),
            pl.BlockSpec((D_MODEL, tn), lambda j: (0, j)),
            pl.BlockSpec((1, tn), lambda j: (0, j)),
        ],
        out_specs=[cols(c_prompt), cols(c_sample)],
        out_shape=[jax.ShapeDtypeStruct((c.shape[0], n), F32) for c in (c_prompt, c_sample)],
        compiler_params=pltpu.CompilerParams(
            dimension_semantics=("arbitrary",), vmem_limit_bytes=VMEM_LIMIT_BYTES),
        name="adaln",
    )(c_prompt, c_sample, w_ada, b_ada)


def _cast_kernel(w_ref, o_ref):
    o_ref[...] = w_ref[...].astype(o_ref.dtype)


def _to_bf16(w):
    rows, cols = w.shape
    block_rows = rows
    while block_rows * cols * w.dtype.itemsize > CAST_BLOCK_BYTES and block_rows % 16 == 0:
        block_rows //= 2
    spec = pl.BlockSpec((block_rows, cols), lambda i: (i, 0))
    return pl.pallas_call(
        _cast_kernel,
        grid=(rows // block_rows,),
        in_specs=[spec],
        out_specs=spec,
        out_shape=jax.ShapeDtypeStruct(w.shape, BF16),
        compiler_params=pltpu.CompilerParams(
            dimension_semantics=("arbitrary",), vmem_limit_bytes=VMEM_LIMIT_BYTES),
        name="to_bf16",
    )(w)


def _lower_bound(lb_ref):
    lg = lb_ref[...]
    e = jnp.exp(lg - jnp.max(lg, axis=0, keepdims=True))
    p = e / jnp.sum(e, axis=0, keepdims=True)
    return jnp.sum(p[0:DEPTH, :], axis=0, keepdims=True)


def _section(hb, w_in_ref, b_in_ref, i, width):
    return _dot(hb, w_in_ref[:, i:i + width]) + b_in_ref[:, i:i + width]


def _in_projection(hb, lb, w_in_ref, b_in_ref):
    sec = functools.partial(_section, hb, w_in_ref, b_in_ref)

    q = _silu(sec(0, HG_WIDTH))
    f = lb + (1.0 - lb) * jax.nn.sigmoid(sec(HG_WIDTH, HG_WIDTH))
    v = sec(2 * HG_WIDTH, HG_WIDTH)
    gate = _silu(sec(3 * HG_WIDTH, HG_WIDTH))
    za = sec(4 * HG_WIDTH, CV_WIDTH)
    zb = sec(4 * HG_WIDTH + CV_WIDTH, CV_WIDTH)
    return q, 1.0 - f, jnp.log(f), v, gate, za * jax.nn.sigmoid(zb)


def _head_norm(o, hgw, gate):
    ms = jnp.mean(o * o, axis=-1, keepdims=True)
    return o * lax.rsqrt(ms + EPS) * hgw * gate


def _group_norm_swish(uc, gmat_ref, gn_g, gn_b):
    width = gmat_ref.shape[1]

    def per_group(a, which):
        ab = a.astype(BF16)
        return jnp.concatenate([_dot(ab[:, c0:c0 + width], gmat_ref[which])
                                for c0 in range(0, ab.shape[1], width)], axis=-1)

    d0 = uc - per_group(uc, 0)
    d = d0 - per_group(d0, 1)
    var = per_group(d * d, 1)
    return _silu(d * lax.rsqrt(var + EPS) * gn_g + gn_b)


def _out_projection(mix_bf16, w_out_ref, b_out_ref):
    return _dot(mix_bf16, w_out_ref[...]) + b_out_ref[...]


def _prompt_kernel(x_ref, mod_ref, lb_ref, w_in_ref, b_in_ref, hgw_ref, cw_ref, cb_ref,
                   gng_ref, gnb_ref, gmat_ref, w_out_ref, b_out_ref, ln1g_ref, ln1b_ref,
                   w_up_ref, b_up_ref, w_down_ref, b_down_ref, ln2g_ref, ln2b_ref,
                   y_ref, s_out_ref, cbuf_out_ref,
                   q_s, k_s, lf_s, v_s, g_s, ubuf, ush, uc_s, mix_s, st_s,
                   x1_s, hb2_s, res2_s, act_s, ff_s, *, tile, tiles_per_seq, n_tiles):
    i = pl.program_id(0)
    mixer_tile = jnp.minimum(i, n_tiles - 1)
    b_mix = mixer_tile // tiles_per_seq
    t = mixer_tile % tiles_per_seq
    b_mlp = jnp.maximum(i - 1, 0) // tiles_per_seq

    def mlp_prologue():
        x1_prev = x1_s[...]
        sh2 = mod_ref[b_mlp, 3:4, :]
        sc2 = mod_ref[b_mlp, 4:5, :]
        hb2_s[...] = (x1_prev * (1.0 + sc2) + sh2).astype(BF16)
        res2_s[...] = ALPHA * x1_prev

    def mlp_epilogue():
        g2 = mod_ref[b_mlp, 5:6, :]
        y_ref[0] = _layer_norm(res2_s[...] + (1.0 + g2) * ff_s[...], ln2g_ref[...], ln2b_ref[...])

    def mlp_up(c0):
        cs = slice(c0, c0 + UP_CHUNK)
        a = jnp.maximum(_dot(hb2_s[...], w_up_ref[:, cs]) + b_up_ref[:, cs], 0.0)
        act_s[:, cs] = (a * a).astype(BF16)

    def mlp_down(c0):
        cs = slice(c0, c0 + DOWN_CHUNK)
        ff_s[:, cs] = _dot(act_s[...], w_down_ref[:, cs]) + b_down_ref[:, cs]

    up_chunks = list(range(0, D_FF, UP_CHUNK))
    down_chunks = list(range(0, D_MODEL, DOWN_CHUNK))

    def next_up():
        if up_chunks:
            mlp_up(up_chunks.pop(0))

    def next_down():
        if down_chunks:
            mlp_down(down_chunks.pop(0))

    @pl.when(i == 0)
    def _():
        x1_s[...] = jnp.zeros_like(x1_s)

    @pl.when(t == 0)
    def _():
        st_s[...] = jnp.zeros_like(st_s)
        ubuf[0:CONV_PAD, :] = jnp.zeros((CONV_PAD, CV_WIDTH), F32)

    mlp_prologue()

    x = x_ref[0]
    sh1 = mod_ref[b_mix, 0:1, :]
    sc1 = mod_ref[b_mix, 1:2, :]
    g1 = mod_ref[b_mix, 2:3, :]
    hb = (x * (1.0 + sc1) + sh1).astype(BF16)
    sec = functools.partial(_section, hb, w_in_ref, b_in_ref)
    lb = _lower_bound(lb_ref)

    ubuf[CONV_PAD:CONV_PAD + tile, :] = (sec(4 * HG_WIDTH, CV_WIDTH)
                                         * jax.nn.sigmoid(sec(4 * HG_WIDTH + CV_WIDTH, CV_WIDTH)))
    first_off = CONV_PAD - CV_BUF
    shifted_rows = ush.shape[1]

    def conv_piece(r0, cs):
        accs = [jnp.broadcast_to(cb_ref[:, cs], (CONV_ROWS, LANES))] + [None] * (CONV_CHAINS - 1)
        for phase in range(SUBLANES):
            taps = [j for j in range(CV_KERNEL) if (first_off + j) % SUBLANES == phase]
            span = (first_off + taps[-1]) // SUBLANES * SUBLANES
            rows = slice(r0, r0 + span + CONV_ROWS)
            win = ubuf[rows, cs] if phase == 0 else ush[phase - 1, rows, :]
            c = phase % CONV_CHAINS
            for j in taps:
                a0 = first_off + j - phase
                term = cw_ref[j:j + 1, cs] * win[a0:a0 + CONV_ROWS, :]
                accs[c] = term if accs[c] is None else accs[c] + term
        uc_s[r0:r0 + CONV_ROWS, cs] = functools.reduce(lambda a, b: a + b, accs)

    def conv_pieces(n):
        cs = slice(n * LANES, (n + 1) * LANES)
        for phase in range(1, SUBLANES):
            ush[phase - 1] = ubuf[phase:phase + shifted_rows, cs]
        for r0 in range(0, tile, CONV_ROWS):
            conv_piece(r0, cs)

    next_up()
    q_s[...] = _silu(sec(0, HG_WIDTH))
    conv_pieces(0)
    next_up()
    f = lb + (1.0 - lb) * jax.nn.sigmoid(sec(HG_WIDTH, HG_WIDTH))
    k_s[...] = 1.0 - f
    lf_s[...] = jnp.log(f)
    conv_pieces(1)
    next_up()
    v_s[...] = sec(2 * HG_WIDTH, HG_WIDTH).astype(BF16)
    conv_pieces(2)
    next_up()
    g_s[...] = _silu(sec(3 * HG_WIDTH, HG_WIDTH))
    conv_pieces(3)

    row = lax.broadcasted_iota(jnp.int32, (CHUNK, HG_WIDTH), 0)
    causal = (lax.broadcasted_iota(jnp.int32, (CHUNK, CHUNK), 0)
              >= lax.broadcasted_iota(jnp.int32, (CHUNK, CHUNK), 1))
    hgw = hgw_ref[...]
    for c in range(tile // CHUNK):
        next_up()
        rows = slice(c * CHUNK, (c + 1) * CHUNK)
        b = lf_s[rows, :]
        shift = 1
        while shift < CHUNK:
            b = b + jnp.where(row >= shift, pltpu.roll(b, shift, axis=0), 0.0)
            shift *= 2
        b_last = b[CHUNK - 1:CHUNK, :]
        kc = k_s[rows, :]
        qe = (q_s[rows, :] * jnp.exp(b)).astype(BF16)
        ke = (kc * jnp.exp(-b)).astype(BF16)
        kd = (kc * jnp.exp(b_last - b)).astype(BF16)
        decay = jnp.exp(b_last)
        vb = v_s[rows, :]
        for h in range(HG_HEADS):
            hs = slice(h * HG_DK, (h + 1) * HG_DK)
            att = jnp.where(causal, _dot_nt(qe[:, hs], ke[:, hs]), 0.0).astype(BF16)
            st = st_s[h]
            o = _dot(att, vb[:, hs]) + _dot_nt(qe[:, hs], st.astype(BF16))
            st_s[h] = st * decay[:, hs] + _dot_tn(vb[:, hs], kd[:, hs])
            mix_s[rows, hs] = _head_norm(o, hgw, g_s[rows, hs]).astype(BF16)
    while up_chunks:
        next_up()

    next_down()
    mix_s[:, HG_WIDTH:] = _group_norm_swish(
        uc_s[...], gmat_ref, gng_ref[...], gnb_ref[...]).astype(BF16)
    ubuf[0:CONV_PAD, :] = ubuf[tile:tile + CONV_PAD, :]
    next_down()
    mix = _out_projection(mix_s[...], w_out_ref, b_out_ref)
    x1_s[...] = _layer_norm(ALPHA * x + (1.0 + g1) * mix, ln1g_ref[...], ln1b_ref[...])
    while down_chunks:
        next_down()
    mlp_epilogue()

    @pl.when((t == tiles_per_seq - 1) & (i < n_tiles))
    def _():
        for h in range(HG_HEADS):
            s_out_ref[0, h] = st_s[h].T
        cbuf_out_ref[0] = ubuf[CONV_PAD - CV_BUF:CONV_PAD, :]


def _const_spec(shape):
    zeros = (0,) * len(shape)
    return pl.BlockSpec(shape, lambda *_: zeros, pipeline_mode=pl.Buffered(1))


def _prompt_layer(x, mod, lb_logits, w_in, b_in, hgw, conv_w, conv_b, gn_g, gn_b, gmat,
                  w_out, b_out, ln1_g, ln1_b, w_up, b_up, w_down, b_down, ln2_g, ln2_b, *, tile):
    batch, seq, _ = x.shape
    tiles_per_seq = seq // tile
    n_tiles = batch * tiles_per_seq
    kernel = functools.partial(_prompt_kernel, tile=tile, tiles_per_seq=tiles_per_seq,
                               n_tiles=n_tiles)

    def mixer_block(i):
        m = jnp.minimum(i, n_tiles - 1)
        return m // tiles_per_seq, m % tiles_per_seq

    def mlp_block(i):
        m = jnp.maximum(i - 1, 0)
        return m // tiles_per_seq, m % tiles_per_seq

    consts = (mod, lb_logits, w_in, b_in, hgw, conv_w, conv_b, gn_g, gn_b, gmat, w_out, b_out,
              ln1_g, ln1_b, w_up, b_up, w_down, b_down, ln2_g, ln2_b)
    return pl.pallas_call(
        kernel,
        grid=(n_tiles + 1,),
        in_specs=[pl.BlockSpec((1, tile, D_MODEL), lambda i: (*mixer_block(i), 0))]
        + [_const_spec(a.shape) for a in consts],
        out_specs=[
            pl.BlockSpec((1, tile, D_MODEL), lambda i: (*mlp_block(i), 0)),
            pl.BlockSpec((1, HG_HEADS, HG_DK, HG_DK), lambda i: (mixer_block(i)[0], 0, 0, 0)),
            pl.BlockSpec((1, CV_BUF, CV_WIDTH), lambda i: (mixer_block(i)[0], 0, 0)),
        ],
        out_shape=[
            jax.ShapeDtypeStruct((batch, seq, D_MODEL), F32),
            jax.ShapeDtypeStruct((batch, HG_HEADS, HG_DK, HG_DK), F32),
            jax.ShapeDtypeStruct((batch, CV_BUF, CV_WIDTH), F32),
        ],
        scratch_shapes=[
            pltpu.VMEM((tile, HG_WIDTH), F32),
            pltpu.VMEM((tile, HG_WIDTH), F32),
            pltpu.VMEM((tile, HG_WIDTH), F32),
            pltpu.VMEM((tile, HG_WIDTH), BF16),
            pltpu.VMEM((tile, HG_WIDTH), F32),
            pltpu.VMEM((CONV_PAD + tile, CV_WIDTH), F32),
            pltpu.VMEM((SUBLANES - 1, CONV_PAD + tile - SUBLANES, LANES), F32),
            pltpu.VMEM((tile, CV_WIDTH), F32),
            pltpu.VMEM((tile, D_MODEL), BF16),
            pltpu.VMEM((HG_HEADS, HG_DK, HG_DK), F32),
            pltpu.VMEM((tile, D_MODEL), F32),
            pltpu.VMEM((tile, D_MODEL), BF16),
            pltpu.VMEM((tile, D_MODEL), F32),
            pltpu.VMEM((tile, D_FF), BF16),
            pltpu.VMEM((tile, D_MODEL), F32),
        ],
        compiler_params=pltpu.CompilerParams(
            dimension_semantics=("arbitrary",), vmem_limit_bytes=PROMPT_VMEM_LIMIT_BYTES),
        name="prompt_layer",
    )(x, *consts)


def _mixer_sample_kernel(x_ref, mod_ref, s_in_ref, cbuf_in_ref, lb_ref, w_in_ref, b_in_ref, hgw_ref,
                         cw_ref, cb_ref, gng_ref, gnb_ref, gmat_ref, w_out_ref, b_out_ref,
                         lng_ref, lnb_ref,
                         x1_ref, s_out_ref, cbuf_out_ref,
                         qe_s, ke_s, kd_s, v_s, d_s, o_s, *, seqs):
    rows_n = seqs * DEC_SEQ
    x = x_ref[...]
    sh1 = mod_ref[:, 0:D_MODEL]
    sc1 = mod_ref[:, D_MODEL:2 * D_MODEL]
    g1 = mod_ref[:, 2 * D_MODEL:3 * D_MODEL]
    hb = (x * (1.0 + sc1) + sh1).astype(BF16).reshape(rows_n, D_MODEL)
    q, k, lf, v, gate, u = _in_projection(hb, _lower_bound(lb_ref), w_in_ref, b_in_ref)

    slab = lambda a, t: a[t * seqs:(t + 1) * seqs, :]
    b = [slab(lf, 0)]
    for t in range(1, DEC_SEQ):
        b.append(b[-1] + slab(lf, t))
    b_last = b[-1]
    parts = _split3(jnp.exp(b_last)) + (jnp.zeros_like(b_last),)

    def put(ref, t, val):
        for h in range(HG_HEADS):
            ref[h, t * seqs:(t + 1) * seqs, :] = val[:, h * HG_DK:(h + 1) * HG_DK]

    for t in range(DEC_SEQ):
        put(qe_s, t, slab(q, t) * jnp.exp(b[t]))
        put(ke_s, t, slab(k, t) * jnp.exp(-b[t]))
        put(kd_s, t, slab(k, t) * jnp.exp(b_last - b[t]))
        put(d_s, t, parts[t])
        put(v_s, t, slab(v, t))

    stride = seqs // SEQ_GROUP
    n_rows = SEQ_GROUP * DEC_SEQ
    member = lax.broadcasted_iota(jnp.int32, (n_rows, HG_DK), 0) % SEQ_GROUP
    lanes = lambda g: slice(g * HG_DK, (g + 1) * HG_DK)
    per_member = lambda a: jnp.concatenate(
        [jnp.where(member == g, a, 0.0) for g in range(SEQ_GROUP)], axis=1)
    ones_g = per_member(jnp.ones((n_rows, HG_DK), F32)).astype(BF16)
    ri = lax.broadcasted_iota(jnp.int32, (n_rows, n_rows), 0)
    ci = lax.broadcasted_iota(jnp.int32, (n_rows, n_rows), 1)
    causal = (ri // SEQ_GROUP >= ci // SEQ_GROUP) & (ri % SEQ_GROUP == ci % SEQ_GROUP)
    for s in range(stride):
        rows = pl.ds(s, n_rows, stride=stride)
        for h in range(HG_HEADS):
            qe_g = qe_s[h, rows, :].astype(BF16)
            ke_g = ke_s[h, rows, :].astype(BF16)
            kd_g = kd_s[h, rows, :].astype(BF16)
            d_g = d_s[h, rows, :].astype(BF16)
            v_g = v_s[h, rows, :]
            s_g = jnp.concatenate([s_in_ref[s + g * stride, h] for g in range(SEQ_GROUP)], axis=1)
            att = jnp.where(causal, _dot_nt(qe_g, ke_g), 0.0).astype(BF16)
            inter_all = _dot(qe_g, s_g.astype(BF16))
            inter = inter_all[:, lanes(0)]
            for g in range(1, SEQ_GROUP):
                inter = jnp.where(member == g, inter_all[:, lanes(g)], inter)
            o_s[h, rows, :] = _dot(att, v_g.astype(BF16)) + inter
            new = _dot_tn(d_g, ones_g) * s_g + _dot_tn(kd_g, per_member(v_g).astype(BF16))
            for g in range(SEQ_GROUP):
                s_out_ref[s + g * stride, h] = new[:, lanes(g)]
    hgw = hgw_ref[...]
    o_a = jnp.concatenate(
        [_head_norm(o_s[h], hgw, gate[:, h * HG_DK:(h + 1) * HG_DK]) for h in range(HG_HEADS)],
        axis=-1)

    full = lambda i: cbuf_in_ref[i] if i < CV_BUF else slab(u, i - CV_BUF)
    acc = [jnp.broadcast_to(cb_ref[...], (seqs, CV_WIDTH)) for _ in range(DEC_SEQ)]
    for i in range(CV_BUF + DEC_SEQ):
        f_i = full(i)
        for t in range(DEC_SEQ):
            if 0 <= i - t < CV_KERNEL:
                acc[t] = acc[t] + cw_ref[i - t:i - t + 1, :] * f_i
    for i in range(CV_BUF):
        cbuf_out_ref[i] = full(i + DEC_SEQ)
    o_b = _group_norm_swish(jnp.concatenate(acc, axis=0), gmat_ref, gng_ref[...], gnb_ref[...])

    mix = _out_projection(jnp.concatenate([o_a, o_b], axis=-1).astype(BF16), w_out_ref, b_out_ref)
    y = ALPHA * x + (1.0 + g1) * mix.reshape(DEC_SEQ, seqs, D_MODEL)
    x1_ref[...] = _layer_norm(y, lng_ref[...], lnb_ref[...])


def _mixer_sample(x, mod, s_in, cbuf_in, lb_logits, w_in, b_in, hgw, conv_w, conv_b, gn_g, gn_b,
                  gmat, w_out, b_out, ln_g, ln_b, *, seqs):
    n_seq = s_in.shape[0]
    rows = seqs * DEC_SEQ
    kernel = functools.partial(_mixer_sample_kernel, seqs=seqs)
    return pl.pallas_call(
        kernel,
        grid=(n_seq // seqs,),
        in_specs=[
            pl.BlockSpec((DEC_SEQ, seqs, D_MODEL), lambda i: (0, i, 0)),
            pl.BlockSpec((seqs, 3 * D_MODEL), lambda i: (i, 0)),
            pl.BlockSpec((seqs, HG_HEADS, HG_DK, HG_DK), lambda i: (i, 0, 0, 0)),
            pl.BlockSpec((CV_BUF, seqs, CV_WIDTH), lambda i: (0, i, 0)),
            _const_spec(lb_logits.shape),
            _const_spec(w_in.shape),
            _const_spec(b_in.shape),
            _const_spec(hgw.shape),
            _const_spec(conv_w.shape),
            _const_spec(conv_b.shape),
            _const_spec(gn_g.shape),
            _const_spec(gn_b.shape),
            _const_spec(gmat.shape),
            _const_spec(w_out.shape),
            _const_spec(b_out.shape),
            _const_spec(ln_g.shape),
            _const_spec(ln_b.shape),
        ],
        out_specs=[
            pl.BlockSpec((DEC_SEQ, seqs, D_MODEL), lambda i: (0, i, 0)),
            pl.BlockSpec((seqs, HG_HEADS, HG_DK, HG_DK), lambda i: (i, 0, 0, 0)),
            pl.BlockSpec((CV_BUF, seqs, CV_WIDTH), lambda i: (0, i, 0)),
        ],
        out_shape=[
            jax.ShapeDtypeStruct((DEC_SEQ, n_seq, D_MODEL), F32),
            jax.ShapeDtypeStruct(s_in.shape, F32),
            jax.ShapeDtypeStruct(cbuf_in.shape, F32),
        ],
        scratch_shapes=[pltpu.VMEM((HG_HEADS, rows, HG_DK), F32)] * 6,
        compiler_params=pltpu.CompilerParams(
            dimension_semantics=("arbitrary",), vmem_limit_bytes=VMEM_LIMIT_BYTES),
        name="mixer_sample",
    )(x, mod, s_in, cbuf_in, lb_logits, w_in, b_in, hgw, conv_w, conv_b, gn_g, gn_b, gmat,
      w_out, b_out, ln_g, ln_b)


def _mlp_kernel(x_ref, sh_ref, sc_ref, g_ref, w_up_ref, b_up_ref, w_down_ref, b_down_ref,
                lng_ref, lnb_ref, o_ref):
    x = x_ref[...]
    hb = (x * (1.0 + sc_ref[...]) + sh_ref[...]).astype(BF16).reshape(-1, D_MODEL)
    ff = jnp.broadcast_to(b_down_ref[...], hb.shape)
    for c0 in range(0, D_FF, FF_CHUNK):
        cs = slice(c0, c0 + FF_CHUNK)
        a = jnp.maximum(_dot(hb, w_up_ref[:, cs]) + b_up_ref[:, cs], 0.0)
        ff = ff + _dot((a * a).astype(BF16), w_down_ref[cs, :])
    y = ALPHA * x + (1.0 + g_ref[...]) * ff.reshape(x.shape)
    o_ref[...] = _layer_norm(y, lng_ref[...], lnb_ref[...])


def _mlp_sample(x, mod, w_up, b_up, w_down, b_down, ln_g, ln_b):
    x_spec = pl.BlockSpec(x.shape, lambda i: (0, 0, 0))
    mod_specs = [pl.BlockSpec((x.shape[1], D_MODEL), functools.partial(lambda i, r: (0, r), r=r))
                 for r in (3, 4, 5)]
    return pl.pallas_call(
        _mlp_kernel,
        grid=(1,),
        in_specs=[x_spec] + mod_specs + [
            _const_spec(w_up.shape),
            _const_spec(b_up.shape),
            _const_spec(w_down.shape),
            _const_spec(b_down.shape),
            _const_spec(ln_g.shape),
            _const_spec(ln_b.shape),
        ],
        out_specs=x_spec,
        out_shape=jax.ShapeDtypeStruct(x.shape, F32),
        compiler_params=pltpu.CompilerParams(
            dimension_semantics=("arbitrary",), vmem_limit_bytes=VMEM_LIMIT_BYTES),
        name="mlp_sample",
    )(x, mod, mod, mod, w_up, b_up, w_down, b_down, ln_g, ln_b)


PROMPT_TILE = 512
SAMPLE_SEQS = 16


def kernel(x_prompt, x_sample, c_prompt, c_sample, state_hgrn, state_conv, lb_logits, w_in, b_in,
           hg_norm_w, conv_w, conv_b, gn_g, gn_b, w_out, b_out, ln1_g, ln1_b, w_up, b_up, w_down,
           b_down, ln2_g, ln2_b, w_ada, b_ada):
    assert w_in.shape[0] == DEPTH
    batch, seq, _ = x_prompt.shape
    dec_batch, dec_seq, _ = x_sample.shape
    assert dec_seq == DEC_SEQ

    group = jnp.arange(MXU_WIDTH, dtype=jnp.int32) // (CV_WIDTH // CV_GROUPS)
    first = jnp.arange(MXU_WIDTH, dtype=jnp.int32)[:, None] == group[None, :] * (CV_WIDTH // CV_GROUPS)
    gmat = jnp.stack([jnp.where(first, 1.0, 0.0),
                      jnp.where(group[:, None] == group[None, :], CV_GROUPS / CV_WIDTH, 0.0)]
                     ).astype(BF16)

    xp = x_prompt
    xs = jnp.transpose(x_sample, (1, 0, 2))
    hp, cp, hs, cs = [], [], [], []
    for l in range(DEPTH):
        row = lambda a: a[l][None, :]
        w_in_b, w_out_b = _to_bf16(w_in[l]), _to_bf16(w_out[l])
        w_up_b, w_down_b = _to_bf16(w_up[l]), _to_bf16(w_down[l])

        mod_p, mod_s = _adaln(c_prompt, c_sample, w_ada[l], row(b_ada))
        mod_p = mod_p.reshape(batch, 6, D_MODEL)

        mixer_params = (lb_logits, w_in_b, row(b_in), row(hg_norm_w), conv_w[l], row(conv_b),
                        row(gn_g), row(gn_b), gmat, w_out_b, row(b_out), row(ln1_g), row(ln1_b))
        mlp_params = (w_up_b, row(b_up), w_down_b, row(b_down), row(ln2_g), row(ln2_b))
        xp, sp, bp = _prompt_layer(xp, mod_p, *mixer_params, *mlp_params, tile=PROMPT_TILE)
        xs, ss, bs = _mixer_sample(xs, mod_s, state_hgrn[l], jnp.transpose(state_conv[l], (1, 0, 2)),
                                   *mixer_params, seqs=SAMPLE_SEQS)
        xs = _mlp_sample(xs, mod_s, *mlp_params)

        hp.append(sp); cp.append(bp); hs.append(ss); cs.append(jnp.transpose(bs, (1, 0, 2)))

    return (xp, jnp.transpose(xs, (1, 0, 2)), jnp.stack(hp), jnp.stack(cp), jnp.stack(hs),
            jnp.stack(cs))
```

```python
import functools

import jax
import jax.numpy as jnp
from jax import lax
from jax.experimental import pallas as pl
from jax.experimental.pallas import tpu as pltpu

F32 = jnp.float32
BF16 = jnp.bfloat16

D_MODEL = 1024
HG_WIDTH = 512
CV_WIDTH = 512
HG_HEADS = 4
HG_DK = 128
CV_GROUPS = 8
CV_KERNEL = 31
CV_BUF = CV_KERNEL - 1
D_FF = 4 * D_MODEL
N_IN = 4 * HG_WIDTH + 2 * CV_WIDTH
DEPTH = 1
ALPHA = (2.0 * DEPTH) ** 0.25
EPS = 1e-5

CHUNK = 64
SUBLANES = 8
LANES = 128
MXU_WIDTH = 256
CONV_PAD = 32
CONV_ROWS = 128
CONV_CHAINS = 2
UP_CHUNK = 512
DOWN_CHUNK = 256
FF_CHUNK = 1024
DEC_SEQ = 4
SEQ_GROUP = 4

VMEM_LIMIT_BYTES = 52 * 1024 * 1024
BIG_VMEM_LIMIT_BYTES = 60 * 1024 * 1024
CAST_BLOCK_BYTES = 4 * 1024 * 1024
ADALN_COLS = 1024

_NT = (((1,), (1,)), ((), ()))
_TN = (((0,), (0,)), ((), ()))


def _dot(a, b):
    return jnp.dot(a, b, preferred_element_type=F32)


def _dot_nt(a, b):
    return lax.dot_general(a, b, _NT, preferred_element_type=F32)


def _dot_tn(a, b):
    return lax.dot_general(a, b, _TN, preferred_element_type=F32)


def _silu(x):
    return x * jax.nn.sigmoid(x)


def _layer_norm(y, g, b):
    mu = jnp.mean(y, axis=-1, keepdims=True)
    yc = y - mu
    var = jnp.mean(yc * yc, axis=-1, keepdims=True)
    return yc * lax.rsqrt(var + EPS) * g + b


def _adaln_kernel(cp_ref, cs_ref, w_ref, b_ref, op_ref, os_ref):
    a = _silu(jnp.concatenate([cp_ref[...], cs_ref[...]], axis=0))
    a_hi = a.astype(BF16)
    a_lo = (a - a_hi.astype(F32)).astype(BF16)
    w = w_ref[...].astype(BF16)
    mod = _dot(a_hi, w) + _dot(a_lo, w) + b_ref[...]
    n_prompt = cp_ref.shape[0]
    op_ref[...] = mod[:n_prompt]
    os_ref[...] = mod[n_prompt:]


def _adaln(c_prompt, c_sample, w_ada, b_ada):
    n = w_ada.shape[1]
    tn = ADALN_COLS
    rows = lambda c: pl.BlockSpec((c.shape[0], D_MODEL), lambda j: (0, 0))
    cols = lambda c: pl.BlockSpec((c.shape[0], tn), lambda j: (0, j))
    return pl.pallas_call(
        _adaln_kernel,
        grid=(n // tn,),
        in_specs=[
            rows(c_prompt),
            rows(c_sample),
            pl.BlockSpec((D_MODEL, tn), lambda j: (0, j)),
            pl.BlockSpec((1, tn), lambda j: (0, j)),
        ],
        out_specs=[cols(c_prompt), cols(c_sample)],
        out_shape=[jax.ShapeDtypeStruct((c.shape[0], n), F32) for c in (c_prompt, c_sample)],
        compiler_params=pltpu.CompilerParams(
            dimension_semantics=("arbitrary",), vmem_limit_bytes=VMEM_LIMIT_BYTES),
        name="adaln",
    )(c_prompt, c_sample, w_ada, b_ada)


def _cast_kernel(w_ref, o_ref):
    o_ref[...] = w_ref[...].astype(o_ref.dtype)


def _to_bf16(w):
    rows, cols = w.shape
    block_rows = rows
    while block_rows * cols * w.dtype.itemsize > CAST_BLOCK_BYTES and block_rows % 16 == 0:
        block_rows //= 2
    spec = pl.BlockSpec((block_rows, cols), lambda i: (i, 0))
    return pl.pallas_call(
        _cast_kernel,
        grid=(rows // block_rows,),
        in_specs=[spec],
        out_specs=spec,
        out_shape=jax.ShapeDtypeStruct(w.shape, BF16),
        compiler_params=pltpu.CompilerParams(
            dimension_semantics=("arbitrary",), vmem_limit_bytes=VMEM_LIMIT_BYTES),
        name="to_bf16",
    )(w)


def _lower_bound(lb_ref):
    lg = lb_ref[...]
    e = jnp.exp(lg - jnp.max(lg, axis=0, keepdims=True))
    p = e / jnp.sum(e, axis=0, keepdims=True)
    return jnp.sum(p[0:DEPTH, :], axis=0, keepdims=True)


def _section(hb, w_in_ref, b_in_ref, i, width):
    return _dot(hb, w_in_ref[:, i:i + width]) + b_in_ref[:, i:i + width]


def _in_projection(hb, lb, w_in_ref, b_in_ref):
    sec = functools.partial(_section, hb, w_in_ref, b_in_ref)

    q = _silu(sec(0, HG_WIDTH))
    f = lb + (1.0 - lb) * jax.nn.sigmoid(sec(HG_WIDTH, HG_WIDTH))
    v = sec(2 * HG_WIDTH, HG_WIDTH)
    gate = _silu(sec(3 * HG_WIDTH, HG_WIDTH))
    za = sec(4 * HG_WIDTH, CV_WIDTH)
    zb = sec(4 * HG_WIDTH + CV_WIDTH, CV_WIDTH)
    return q, 1.0 - f, jnp.log(f), v, gate, za * jax.nn.sigmoid(zb)


def _head_norm(o, hgw, gate):
    ms = jnp.mean(o * o, axis=-1, keepdims=True)
    return o * lax.rsqrt(ms + EPS) * hgw * gate


def _group_norm_swish(uc, gmat_ref, gn_g, gn_b):
    width = gmat_ref.shape[1]

    def per_group(a, which):
        ab = a.astype(BF16)
        return jnp.concatenate([_dot(ab[:, c0:c0 + width], gmat_ref[which])
                                for c0 in range(0, ab.shape[1], width)], axis=-1)

    d0 = uc - per_group(uc, 0)
    d = d0 - per_group(d0, 1)
    var = per_group(d * d, 1)
    return _silu(d * lax.rsqrt(var + EPS) * gn_g + gn_b)


def _out_projection(mix_bf16, w_out_ref, b_out_ref):
    return _dot(mix_bf16, w_out_ref[...]) + b_out_ref[...]


def _prompt_kernel(x_ref, mod_ref, lb_ref, w_in_ref, b_in_ref, hgw_ref, cw_ref, cb_ref,
                   gng_ref, gnb_ref, gmat_ref, w_out_ref, b_out_ref, ln1g_ref, ln1b_ref,
                   w_up_ref, b_up_ref, w_down_ref, b_down_ref, ln2g_ref, ln2b_ref,
                   y_ref, s_out_ref, cbuf_out_ref,
                   q_s, k_s, lf_s, v_s, g_s, ubuf, ush, uc_s, mix_s, st_s,
                   x1_s, hb2_s, res2_s, act_s, ff_s, *, tile, tiles_per_seq, n_tiles):
    i = pl.program_id(0)
    mixer_tile = jnp.minimum(i, n_tiles - 1)
    b_mix = mixer_tile // tiles_per_seq
    t = mixer_tile % tiles_per_seq
    b_mlp = jnp.maximum(i - 1, 0) // tiles_per_seq

    def mlp_prologue():
        x1_prev = x1_s[...]
        sh2 = mod_ref[b_mlp, 3:4, :]
        sc2 = mod_ref[b_mlp, 4:5, :]
        hb2_s[...] = (x1_prev * (1.0 + sc2) + sh2).astype(BF16)
        res2_s[...] = ALPHA * x1_prev

    def mlp_epilogue():
        g2 = mod_ref[b_mlp, 5:6, :]
        y_ref[0] = _layer_norm(res2_s[...] + (1.0 + g2) * ff_s[...], ln2g_ref[...], ln2b_ref[...])

    def mlp_up(c0):
        cs = slice(c0, c0 + UP_CHUNK)
        a = jnp.maximum(_dot(hb2_s[...], w_up_ref[:, cs]) + b_up_ref[:, cs], 0.0)
        act_s[:, cs] = (a * a).astype(BF16)

    def mlp_down(c0):
        cs = slice(c0, c0 + DOWN_CHUNK)
        ff_s[:, cs] = _dot(act_s[...], w_down_ref[:, cs]) + b_down_ref[:, cs]

    up_chunks = list(range(0, D_FF, UP_CHUNK))
    down_chunks = list(range(0, D_MODEL, DOWN_CHUNK))

    def next_up():
        if up_chunks:
            mlp_up(up_chunks.pop(0))

    def next_down():
        if down_chunks:
            mlp_down(down_chunks.pop(0))

    @pl.when(i == 0)
    def _():
        x1_s[...] = jnp.zeros_like(x1_s)

    @pl.when(t == 0)
    def _():
        st_s[...] = jnp.zeros_like(st_s)
        ubuf[0:CONV_PAD, :] = jnp.zeros((CONV_PAD, CV_WIDTH), F32)

    mlp_prologue()

    x = x_ref[0]
    sh1 = mod_ref[b_mix, 0:1, :]
    sc1 = mod_ref[b_mix, 1:2, :]
    g1 = mod_ref[b_mix, 2:3, :]
    hb = (x * (1.0 + sc1) + sh1).astype(BF16)
    sec = functools.partial(_section, hb, w_in_ref, b_in_ref)
    lb = _lower_bound(lb_ref)

    ubuf[CONV_PAD:CONV_PAD + tile, :] = (sec(4 * HG_WIDTH, CV_WIDTH)
                                         * jax.nn.sigmoid(sec(4 * HG_WIDTH + CV_WIDTH, CV_WIDTH)))
    first_off = CONV_PAD - CV_BUF
    shifted_rows = ush.shape[1]

    def conv_piece(r0, cs):
        accs = [jnp.broadcast_to(cb_ref[:, cs], (CONV_ROWS, LANES))] + [None] * (CONV_CHAINS - 1)
        for phase in range(SUBLANES):
            taps = [j for j in range(CV_KERNEL) if (first_off + j) % SUBLANES == phase]
            span = (first_off + taps[-1]) // SUBLANES * SUBLANES
            rows = slice(r0, r0 + span + CONV_ROWS)
            win = ubuf[rows, cs] if phase == 0 else ush[phase - 1, rows, :]
            c = phase % CONV_CHAINS
            for j in taps:
                a0 = first_off + j - phase
                term = cw_ref[j:j + 1, cs] * win[a0:a0 + CONV_ROWS, :]
                accs[c] = term if accs[c] is None else accs[c] + term
        uc_s[r0:r0 + CONV_ROWS, cs] = functools.reduce(lambda a, b: a + b, accs)

    def conv_pieces(n):
        cs = slice(n * LANES, (n + 1) * LANES)
        for phase in range(1, SUBLANES):
            ush[phase - 1] = ubuf[phase:phase + shifted_rows, cs]
        for r0 in range(0, tile, CONV_ROWS):
            conv_piece(r0, cs)

    next_up()
    q_s[...] = _silu(sec(0, HG_WIDTH))
    conv_pieces(0)
    next_up()
    f = lb + (1.0 - lb) * jax.nn.sigmoid(sec(HG_WIDTH, HG_WIDTH))
    k_s[...] = 1.0 - f
    lf_s[...] = jnp.log(f)
    conv_pieces(1)
    next_up()
    v_s[...] = sec(2 * HG_WIDTH, HG_WIDTH).astype(BF16)
    conv_pieces(2)
    next_up()
    g_s[...] = _silu(sec(3 * HG_WIDTH, HG_WIDTH))
    conv_pieces(3)

    row = lax.broadcasted_iota(jnp.int32, (CHUNK, HG_WIDTH), 0)
    causal = (lax.broadcasted_iota(jnp.int32, (CHUNK, CHUNK), 0)
              >= lax.broadcasted_iota(jnp.int32, (CHUNK, CHUNK), 1))
    hgw = hgw_ref[...]
    for c in range(tile // CHUNK):
        next_up()
        rows = slice(c * CHUNK, (c + 1) * CHUNK)
        b = lf_s[rows, :]
        shift = 1
        while shift < CHUNK:
            b = b + jnp.where(row >= shift, pltpu.roll(b, shift, axis=0), 0.0)
            shift *= 2
        b_last = b[CHUNK - 1:CHUNK, :]
        kc = k_s[rows, :]
        qe = (q_s[rows, :] * jnp.exp(b)).astype(BF16)
        ke = (kc * jnp.exp(-b)).astype(BF16)
        kd = (kc * jnp.exp(b_last - b)).astype(BF16)
        decay = jnp.exp(b_last)
        vb = v_s[rows, :]
        for h in range(HG_HEADS):
            hs = slice(h * HG_DK, (h + 1) * HG_DK)
            att = jnp.where(causal, _dot_nt(qe[:, hs], ke[:, hs]), 0.0).astype(BF16)
            st = st_s[h]
            o = _dot(att, vb[:, hs]) + _dot_nt(qe[:, hs], st.astype(BF16))
            st_s[h] = st * decay[:, hs] + _dot_tn(vb[:, hs], kd[:, hs])
            mix_s[rows, hs] = _head_norm(o, hgw, g_s[rows, hs]).astype(BF16)
    while up_chunks:
        next_up()

    next_down()
    mix_s[:, HG_WIDTH:] = _group_norm_swish(
        uc_s[...], gmat_ref, gng_ref[...], gnb_ref[...]).astype(BF16)
    ubuf[0:CONV_PAD, :] = ubuf[tile:tile + CONV_PAD, :]
    next_down()
    mix = _out_projection(mix_s[...], w_out_ref, b_out_ref)
    x1_s[...] = _layer_norm(ALPHA * x + (1.0 + g1) * mix, ln1g_ref[...], ln1b_ref[...])
    while down_chunks:
        next_down()
    mlp_epilogue()

    @pl.when((t == tiles_per_seq - 1) & (i < n_tiles))
    def _():
        for h in range(HG_HEADS):
            s_out_ref[0, h] = st_s[h].T
        cbuf_out_ref[0] = ubuf[CONV_PAD - CV_BUF:CONV_PAD, :]


def _const_spec(shape):
    zeros = (0,) * len(shape)
    return pl.BlockSpec(shape, lambda *_: zeros, pipeline_mode=pl.Buffered(1))


def _prompt_layer(x, mod, lb_logits, w_in, b_in, hgw, conv_w, conv_b, gn_g, gn_b, gmat,
                  w_out, b_out, ln1_g, ln1_b, w_up, b_up, w_down, b_down, ln2_g, ln2_b, *, tile):
    batch, seq, _ = x.shape
    tiles_per_seq = seq // tile
    n_tiles = batch * tiles_per_seq
    kernel = functools.partial(_prompt_kernel, tile=tile, tiles_per_seq=tiles_per_seq,
                               n_tiles=n_tiles)

    def mixer_block(i):
        m = jnp.minimum(i, n_tiles - 1)
        return m // tiles_per_seq, m % tiles_per_seq

    def mlp_block(i):
        m = jnp.maximum(i - 1, 0)
        return m // tiles_per_seq, m % tiles_per_seq

    consts = (mod, lb_logits, w_in, b_in, hgw, conv_w, conv_b, gn_g, gn_b, gmat, w_out, b_out,
              ln1_g, ln1_b, w_up, b_up, w_down, b_down, ln2_g, ln2_b)
    return pl.pallas_call(
        kernel,
        grid=(n_tiles + 1,),
        in_specs=[pl.BlockSpec((1, tile, D_MODEL), lambda i: (*mixer_block(i), 0))]
        + [_const_spec(a.shape) for a in consts],
        out_specs=[
            pl.BlockSpec((1, tile, D_MODEL), lambda i: (*mlp_block(i), 0)),
            pl.BlockSpec((1, HG_HEADS, HG_DK, HG_DK), lambda i: (mixer_block(i)[0], 0, 0, 0)),
            pl.BlockSpec((1, CV_BUF, CV_WIDTH), lambda i: (mixer_block(i)[0], 0, 0)),
        ],
        out_shape=[
            jax.ShapeDtypeStruct((batch, seq, D_MODEL), F32),
            jax.ShapeDtypeStruct((batch, HG_HEADS, HG_DK, HG_DK), F32),
            jax.ShapeDtypeStruct((batch, CV_BUF, CV_WIDTH), F32),
        ],
        scratch_shapes=[
            pltpu.VMEM((tile, HG_WIDTH), F32),
            pltpu.VMEM((tile, HG_WIDTH), F32),
            pltpu.VMEM((tile, HG_WIDTH), F32),
            pltpu.VMEM((tile, HG_WIDTH), BF16),
            pltpu.VMEM((tile, HG_WIDTH), F32),
            pltpu.VMEM((CONV_PAD + tile, CV_WIDTH), F32),
            pltpu.VMEM((SUBLANES - 1, CONV_PAD + tile - SUBLANES, LANES), F32),
            pltpu.VMEM((tile, CV_WIDTH), F32),
            pltpu.VMEM((tile, D_MODEL), BF16),
            pltpu.VMEM((HG_HEADS, HG_DK, HG_DK), F32),
            pltpu.VMEM((tile, D_MODEL), F32),
            pltpu.VMEM((tile, D_MODEL), BF16),
            pltpu.VMEM((tile, D_MODEL), F32),
            pltpu.VMEM((tile, D_FF), BF16),
            pltpu.VMEM((tile, D_MODEL), F32),
        ],
        compiler_params=pltpu.CompilerParams(
            dimension_semantics=("arbitrary",), vmem_limit_bytes=BIG_VMEM_LIMIT_BYTES),
        name="prompt_layer",
    )(x, *consts)


def _mixer_sample_kernel(x_ref, mod_ref, s_in_ref, cbuf_in_ref, lb_ref, w_in_ref, b_in_ref, hgw_ref,
                         cw_ref, cb_ref, gng_ref, gnb_ref, gmat_ref, w_out_ref, b_out_ref,
                         lng_ref, lnb_ref,
                         x1_ref, s_out_ref, cbuf_out_ref,
                         qe_s, ke_s, kd_s, v_s, o_s, *, seqs):
    rows_n = seqs * DEC_SEQ
    x = x_ref[...]
    sh1 = mod_ref[:, 0:D_MODEL]
    sc1 = mod_ref[:, D_MODEL:2 * D_MODEL]
    g1 = mod_ref[:, 2 * D_MODEL:3 * D_MODEL]
    hb = (x * (1.0 + sc1) + sh1).astype(BF16).reshape(rows_n, D_MODEL)
    q, k, lf, v, gate, u = _in_projection(hb, _lower_bound(lb_ref), w_in_ref, b_in_ref)

    slab = lambda a, t: a[t * seqs:(t + 1) * seqs, :]
    b = [slab(lf, 0)]
    for t in range(1, DEC_SEQ):
        b.append(b[-1] + slab(lf, t))
    b_last = b[-1]
    decay = jnp.exp(b_last)
    pad = jnp.zeros((LANES - seqs, HG_DK), F32)
    decay_t = [jnp.concatenate([decay[:, h * HG_DK:(h + 1) * HG_DK], pad], axis=0).T
               for h in range(HG_HEADS)]

    def put(ref, t, val):
        for h in range(HG_HEADS):
            ref[h, t * seqs:(t + 1) * seqs, :] = val[:, h * HG_DK:(h + 1) * HG_DK]

    for t in range(DEC_SEQ):
        put(qe_s, t, slab(q, t) * jnp.exp(b[t]))
        put(ke_s, t, slab(k, t) * jnp.exp(-b[t]))
        put(kd_s, t, slab(k, t) * jnp.exp(b_last - b[t]))
        put(v_s, t, slab(v, t))

    stride = seqs // SEQ_GROUP
    n_rows = SEQ_GROUP * DEC_SEQ
    member = lax.broadcasted_iota(jnp.int32, (n_rows, HG_DK), 0) % SEQ_GROUP
    lanes = lambda g: slice(g * HG_DK, (g + 1) * HG_DK)
    per_member = lambda a: jnp.concatenate(
        [jnp.where(member == g, a, 0.0) for g in range(SEQ_GROUP)], axis=1)
    ri = lax.broadcasted_iota(jnp.int32, (n_rows, n_rows), 0)
    ci = lax.broadcasted_iota(jnp.int32, (n_rows, n_rows), 1)
    causal = (ri // SEQ_GROUP >= ci // SEQ_GROUP) & (ri % SEQ_GROUP == ci % SEQ_GROUP)
    for s in range(stride):
        rows = pl.ds(s, n_rows, stride=stride)
        for h in range(HG_HEADS):
            qe_g = qe_s[h, rows, :].astype(BF16)
            ke_g = ke_s[h, rows, :].astype(BF16)
            kd_g = kd_s[h, rows, :].astype(BF16)
            v_g = v_s[h, rows, :]
            s_g = jnp.concatenate([s_in_ref[s + g * stride, h] for g in range(SEQ_GROUP)], axis=1)
            att = jnp.where(causal, _dot_nt(qe_g, ke_g), 0.0).astype(BF16)
            inter_all = _dot(qe_g, s_g.astype(BF16))
            inter = inter_all[:, lanes(0)]
            for g in range(1, SEQ_GROUP):
                inter = jnp.where(member == g, inter_all[:, lanes(g)], inter)
            o_s[h, rows, :] = _dot(att, v_g.astype(BF16)) + inter
            col = lambda g: decay_t[h][:, s + g * stride:s + g * stride + 1]
            decay_g = jnp.concatenate([jnp.broadcast_to(col(g), (HG_DK, HG_DK))
                                       for g in range(SEQ_GROUP)], axis=1)
            new = decay_g * s_g + _dot_tn(kd_g, per_member(v_g).astype(BF16))
            for g in range(SEQ_GROUP):
                s_out_ref[s + g * stride, h] = new[:, lanes(g)]
    hgw = hgw_ref[...]
    o_a = jnp.concatenate(
        [_head_norm(o_s[h], hgw, gate[:, h * HG_DK:(h + 1) * HG_DK]) for h in range(HG_HEADS)],
        axis=-1)

    full = lambda i: cbuf_in_ref[i] if i < CV_BUF else slab(u, i - CV_BUF)
    acc = [jnp.broadcast_to(cb_ref[...], (seqs, CV_WIDTH)) for _ in range(DEC_SEQ)]
    for i in range(CV_BUF + DEC_SEQ):
        f_i = full(i)
        for t in range(DEC_SEQ):
            if 0 <= i - t < CV_KERNEL:
                acc[t] = acc[t] + cw_ref[i - t:i - t + 1, :] * f_i
    for i in range(CV_BUF):
        cbuf_out_ref[i] = full(i + DEC_SEQ)
    o_b = _group_norm_swish(jnp.concatenate(acc, axis=0), gmat_ref, gng_ref[...], gnb_ref[...])

    mix = _out_projection(jnp.concatenate([o_a, o_b], axis=-1).astype(BF16), w_out_ref, b_out_ref)
    y = ALPHA * x + (1.0 + g1) * mix.reshape(DEC_SEQ, seqs, D_MODEL)
    x1_ref[...] = _layer_norm(y, lng_ref[...], lnb_ref[...])


def _mixer_sample(x, mod, s_in, cbuf_in, lb_logits, w_in, b_in, hgw, conv_w, conv_b, gn_g, gn_b,
                  gmat, w_out, b_out, ln_g, ln_b, *, seqs):
    n_seq = s_in.shape[0]
    rows = seqs * DEC_SEQ
    kernel = functools.partial(_mixer_sample_kernel, seqs=seqs)
    return pl.pallas_call(
        kernel,
        grid=(n_seq // seqs,),
        in_specs=[
            pl.BlockSpec((DEC_SEQ, seqs, D_MODEL), lambda i: (0, i, 0)),
            pl.BlockSpec((seqs, 3 * D_MODEL), lambda i: (i, 0)),
            pl.BlockSpec((seqs, HG_HEADS, HG_DK, HG_DK), lambda i: (i, 0, 0, 0)),
            pl.BlockSpec((CV_BUF, seqs, CV_WIDTH), lambda i: (0, i, 0)),
            _const_spec(lb_logits.shape),
            _const_spec(w_in.shape),
            _const_spec(b_in.shape),
            _const_spec(hgw.shape),
            _const_spec(conv_w.shape),
            _const_spec(conv_b.shape),
            _const_spec(gn_g.shape),
            _const_spec(gn_b.shape),
            _const_spec(gmat.shape),
            _const_spec(w_out.shape),
            _const_spec(b_out.shape),
            _const_spec(ln_g.shape),
            _const_spec(ln_b.shape),
        ],
        out_specs=[
            pl.BlockSpec((DEC_SEQ, seqs, D_MODEL), lambda i: (0, i, 0)),
            pl.BlockSpec((seqs, HG_HEADS, HG_DK, HG_DK), lambda i: (i, 0, 0, 0)),
            pl.BlockSpec((CV_BUF, seqs, CV_WIDTH), lambda i: (0, i, 0)),
        ],
        out_shape=[
            jax.ShapeDtypeStruct((DEC_SEQ, n_seq, D_MODEL), F32),
            jax.ShapeDtypeStruct(s_in.shape, F32),
            jax.ShapeDtypeStruct(cbuf_in.shape, F32),
        ],
        scratch_shapes=[pltpu.VMEM((HG_HEADS, rows, HG_DK), F32)] * 5,
        compiler_params=pltpu.CompilerParams(
            dimension_semantics=("arbitrary",), vmem_limit_bytes=BIG_VMEM_LIMIT_BYTES),
        name="mixer_sample",
    )(x, mod, s_in, cbuf_in, lb_logits, w_in, b_in, hgw, conv_w, conv_b, gn_g, gn_b, gmat,
      w_out, b_out, ln_g, ln_b)


def _mlp_kernel(x_ref, sh_ref, sc_ref, g_ref, w_up_ref, b_up_ref, w_down_ref, b_down_ref,
                lng_ref, lnb_ref, o_ref):
    x = x_ref[...]
    hb = (x * (1.0 + sc_ref[...]) + sh_ref[...]).astype(BF16).reshape(-1, D_MODEL)
    ff = jnp.broadcast_to(b_down_ref[...], hb.shape)
    for c0 in range(0, D_FF, FF_CHUNK):
        cs = slice(c0, c0 + FF_CHUNK)
        a = jnp.maximum(_dot(hb, w_up_ref[:, cs]) + b_up_ref[:, cs], 0.0)
        ff = ff + _dot((a * a).astype(BF16), w_down_ref[cs, :])
    y = ALPHA * x + (1.0 + g_ref[...]) * ff.reshape(x.shape)
    o_ref[...] = _layer_norm(y, lng_ref[...], lnb_ref[...])


def _mlp_sample(x, mod, w_up, b_up, w_down, b_down, ln_g, ln_b):
    x_spec = pl.BlockSpec(x.shape, lambda i: (0, 0, 0))
    mod_specs = [pl.BlockSpec((x.shape[1], D_MODEL), functools.partial(lambda i, r: (0, r), r=r))
                 for r in (3, 4, 5)]
    return pl.pallas_call(
        _mlp_kernel,
        grid=(1,),
        in_specs=[x_spec] + mod_specs + [
            _const_spec(w_up.shape),
            _const_spec(b_up.shape),
            _const_spec(w_down.shape),
            _const_spec(b_down.shape),
            _const_spec(ln_g.shape),
            _const_spec(ln_b.shape),
        ],
        out_specs=x_spec,
        out_shape=jax.ShapeDtypeStruct(x.shape, F32),
        compiler_params=pltpu.CompilerParams(
            dimension_semantics=("arbitrary",), vmem_limit_bytes=VMEM_LIMIT_BYTES),
        name="mlp_sample",
    )(x, mod, mod, mod, w_up, b_up, w_down, b_down, ln_g, ln_b)


PROMPT_TILE = 512
SAMPLE_SEQS = 32


def kernel(x_prompt, x_sample, c_prompt, c_sample, state_hgrn, state_conv, lb_logits, w_in, b_in,
           hg_norm_w, conv_w, conv_b, gn_g, gn_b, w_out, b_out, ln1_g, ln1_b, w_up, b_up, w_down,
           b_down, ln2_g, ln2_b, w_ada, b_ada):
    assert w_in.shape[0] == DEPTH
    batch, seq, _ = x_prompt.shape
    dec_batch, dec_seq, _ = x_sample.shape
    assert dec_seq == DEC_SEQ

    group = jnp.arange(MXU_WIDTH, dtype=jnp.int32) // (CV_WIDTH // CV_GROUPS)
    first = jnp.arange(MXU_WIDTH, dtype=jnp.int32)[:, None] == group[None, :] * (CV_WIDTH // CV_GROUPS)
    gmat = jnp.stack([jnp.where(first, 1.0, 0.0),
                      jnp.where(group[:, None] == group[None, :], CV_GROUPS / CV_WIDTH, 0.0)]
                     ).astype(BF16)

    xp = x_prompt
    xs = jnp.transpose(x_sample, (1, 0, 2))
    hp, cp, hs, cs = [], [], [], []
    for l in range(DEPTH):
        row = lambda a: a[l][None, :]
        w_in_b, w_out_b = _to_bf16(w_in[l]), _to_bf16(w_out[l])
        w_up_b, w_down_b = _to_bf16(w_up[l]), _to_bf16(w_down[l])

        mod_p, mod_s = _adaln(c_prompt, c_sample, w_ada[l], row(b_ada))
        mod_p = mod_p.reshape(batch, 6, D_MODEL)

        mixer_params = (lb_logits, w_in_b, row(b_in), row(hg_norm_w), conv_w[l], row(conv_b),
                        row(gn_g), row(gn_b), gmat, w_out_b, row(b_out), row(ln1_g), row(ln1_b))
        mlp_params = (w_up_b, row(b_up), w_down_b, row(b_down), row(ln2_g), row(ln2_b))
        xp, sp, bp = _prompt_layer(xp, mod_p, *mixer_params, *mlp_params, tile=PROMPT_TILE)
        xs, ss, bs = _mixer_sample(xs, mod_s, state_hgrn[l], jnp.transpose(state_conv[l], (1, 0, 2)),
                                   *mixer_params, seqs=SAMPLE_SEQS)
        xs = _mlp_sample(xs, mod_s, *mlp_params)

        hp.append(sp); cp.append(bp); hs.append(ss); cs.append(jnp.transpose(bs, (1, 0, 2)))

    return (xp, jnp.transpose(xs, (1, 0, 2)), jnp.stack(hp), jnp.stack(cp), jnp.stack(hs),
            jnp.stack(cs))
```

```python
import functools

import jax
import jax.numpy as jnp
from jax import lax
from jax.experimental import pallas as pl
from jax.experimental.pallas import tpu as pltpu

F32 = jnp.float32
BF16 = jnp.bfloat16

D_MODEL = 1024
HG_WIDTH = 512
CV_WIDTH = 512
HG_HEADS = 4
HG_DK = 128
CV_GROUPS = 8
CV_KERNEL = 31
CV_BUF = CV_KERNEL - 1
D_FF = 4 * D_MODEL
N_IN = 4 * HG_WIDTH + 2 * CV_WIDTH
DEPTH = 1
ALPHA = (2.0 * DEPTH) ** 0.25
EPS = 1e-5

CHUNK = 64
SUBLANES = 8
LANES = 128
MXU_WIDTH = 256
CONV_PAD = 32
CONV_ROWS = 128
CONV_CHAINS = 2
UP_CHUNK = 512
DOWN_CHUNK = 256
FF_CHUNK = 1024
DEC_SEQ = 4
SEQ_GROUP = 16

VMEM_LIMIT_BYTES = 52 * 1024 * 1024
BIG_VMEM_LIMIT_BYTES = 60 * 1024 * 1024
CAST_BLOCK_BYTES = 4 * 1024 * 1024
ADALN_COLS = 1024

_NT = (((1,), (1,)), ((), ()))
_TN = (((0,), (0,)), ((), ()))


def _dot(a, b):
    return jnp.dot(a, b, preferred_element_type=F32)


def _dot_nt(a, b):
    return lax.dot_general(a, b, _NT, preferred_element_type=F32)


def _dot_tn(a, b):
    return lax.dot_general(a, b, _TN, preferred_element_type=F32)


def _silu(x):
    return x * jax.nn.sigmoid(x)


def _layer_norm(y, g, b):
    mu = jnp.mean(y, axis=-1, keepdims=True)
    yc = y - mu
    var = jnp.mean(yc * yc, axis=-1, keepdims=True)
    return yc * lax.rsqrt(var + EPS) * g + b


def _adaln_kernel(cp_ref, cs_ref, w_ref, b_ref, op_ref, os_ref):
    a = _silu(jnp.concatenate([cp_ref[...], cs_ref[...]], axis=0))
    a_hi = a.astype(BF16)
    a_lo = (a - a_hi.astype(F32)).astype(BF16)
    w = w_ref[...].astype(BF16)
    mod = _dot(a_hi, w) + _dot(a_lo, w) + b_ref[...]
    n_prompt = cp_ref.shape[0]
    op_ref[...] = mod[:n_prompt]
    os_ref[...] = mod[n_prompt:]


def _adaln(c_prompt, c_sample, w_ada, b_ada):
    n = w_ada.shape[1]
    tn = ADALN_COLS
    rows = lambda c: pl.BlockSpec((c.shape[0], D_MODEL), lambda j: (0, 0))
    cols = lambda c: pl.BlockSpec((c.shape[0], tn), lambda j: (0, j))
    return pl.pallas_call(
        _adaln_kernel,
        grid=(n // tn,),
        in_specs=[
            rows(c_prompt),
            rows(c_sample),
            pl.BlockSpec((D_MODEL, tn), lambda j: (0, j)),
            pl.BlockSpec((1, tn), lambda j: (0, j)),
        ],
        out_specs=[cols(c_prompt), cols(c_sample)],
        out_shape=[jax.ShapeDtypeStruct((c.shape[0], n), F32) for c in (c_prompt, c_sample)],
        compiler_params=pltpu.CompilerParams(
            dimension_semantics=("arbitrary",), vmem_limit_bytes=VMEM_LIMIT_BYTES),
        name="adaln",
    )(c_prompt, c_sample, w_ada, b_ada)


def _cast_kernel(w_ref, o_ref):
    o_ref[...] = w_ref[...].astype(o_ref.dtype)


def _to_bf16(w):
    rows, cols = w.shape
    block_rows = rows
    while block_rows * cols * w.dtype.itemsize > CAST_BLOCK_BYTES and block_rows % 16 == 0:
        block_rows //= 2
    spec = pl.BlockSpec((block_rows, cols), lambda i: (i, 0))
    return pl.pallas_call(
        _cast_kernel,
        grid=(rows // block_rows,),
        in_specs=[spec],
        out_specs=spec,
        out_shape=jax.ShapeDtypeStruct(w.shape, BF16),
        compiler_params=pltpu.CompilerParams(
            dimension_semantics=("arbitrary",), vmem_limit_bytes=VMEM_LIMIT_BYTES),
        name="to_bf16",
    )(w)


def _lower_bound(lb_ref):
    lg = lb_ref[...]
    e = jnp.exp(lg - jnp.max(lg, axis=0, keepdims=True))
    p = e / jnp.sum(e, axis=0, keepdims=True)
    return jnp.sum(p[0:DEPTH, :], axis=0, keepdims=True)


def _section(hb, w_in_ref, b_in_ref, i, width):
    return _dot(hb, w_in_ref[:, i:i + width]) + b_in_ref[:, i:i + width]


def _in_projection(hb, lb, w_in_ref, b_in_ref):
    sec = functools.partial(_section, hb, w_in_ref, b_in_ref)

    q = _silu(sec(0, HG_WIDTH))
    f = lb + (1.0 - lb) * jax.nn.sigmoid(sec(HG_WIDTH, HG_WIDTH))
    v = sec(2 * HG_WIDTH, HG_WIDTH)
    gate = _silu(sec(3 * HG_WIDTH, HG_WIDTH))
    za = sec(4 * HG_WIDTH, CV_WIDTH)
    zb = sec(4 * HG_WIDTH + CV_WIDTH, CV_WIDTH)
    return q, 1.0 - f, jnp.log(f), v, gate, za * jax.nn.sigmoid(zb)


def _head_norm(o, hgw, gate):
    ms = jnp.mean(o * o, axis=-1, keepdims=True)
    return o * lax.rsqrt(ms + EPS) * hgw * gate


def _group_norm_swish(uc, gmat_ref, gn_g, gn_b):
    width = gmat_ref.shape[1]

    def per_group(a, which):
        ab = a.astype(BF16)
        return jnp.concatenate([_dot(ab[:, c0:c0 + width], gmat_ref[which])
                                for c0 in range(0, ab.shape[1], width)], axis=-1)

    d0 = uc - per_group(uc, 0)
    d = d0 - per_group(d0, 1)
    var = per_group(d * d, 1)
    return _silu(d * lax.rsqrt(var + EPS) * gn_g + gn_b)


def _out_projection(mix_bf16, w_out_ref, b_out_ref):
    return _dot(mix_bf16, w_out_ref[...]) + b_out_ref[...]


def _prompt_kernel(x_ref, mod_ref, lb_ref, w_in_ref, b_in_ref, hgw_ref, cw_ref, cb_ref,
                   gng_ref, gnb_ref, gmat_ref, w_out_ref, b_out_ref, ln1g_ref, ln1b_ref,
                   w_up_ref, b_up_ref, w_down_ref, b_down_ref, ln2g_ref, ln2b_ref,
                   y_ref, s_out_ref, cbuf_out_ref,
                   q_s, k_s, lf_s, v_s, g_s, ubuf, ush, uc_s, mix_s, st_s,
                   x1_s, hb2_s, res2_s, act_s, ff_s, *, tile, tiles_per_seq, n_tiles):
    i = pl.program_id(0)
    mixer_tile = jnp.minimum(i, n_tiles - 1)
    b_mix = mixer_tile // tiles_per_seq
    t = mixer_tile % tiles_per_seq
    b_mlp = jnp.maximum(i - 1, 0) // tiles_per_seq

    def mlp_prologue():
        x1_prev = x1_s[...]
        sh2 = mod_ref[b_mlp, 3:4, :]
        sc2 = mod_ref[b_mlp, 4:5, :]
        hb2_s[...] = (x1_prev * (1.0 + sc2) + sh2).astype(BF16)
        res2_s[...] = ALPHA * x1_prev

    def mlp_epilogue():
        g2 = mod_ref[b_mlp, 5:6, :]
        y_ref[0] = _layer_norm(res2_s[...] + (1.0 + g2) * ff_s[...], ln2g_ref[...], ln2b_ref[...])

    def mlp_up(c0):
        cs = slice(c0, c0 + UP_CHUNK)
        a = jnp.maximum(_dot(hb2_s[...], w_up_ref[:, cs]) + b_up_ref[:, cs], 0.0)
        act_s[:, cs] = (a * a).astype(BF16)

    def mlp_down(c0):
        cs = slice(c0, c0 + DOWN_CHUNK)
        ff_s[:, cs] = _dot(act_s[...], w_down_ref[:, cs]) + b_down_ref[:, cs]

    up_chunks = list(range(0, D_FF, UP_CHUNK))
    down_chunks = list(range(0, D_MODEL, DOWN_CHUNK))

    def next_up():
        if up_chunks:
            mlp_up(up_chunks.pop(0))

    def next_down():
        if down_chunks:
            mlp_down(down_chunks.pop(0))

    @pl.when(i == 0)
    def _():
        x1_s[...] = jnp.zeros_like(x1_s)

    @pl.when(t == 0)
    def _():
        st_s[...] = jnp.zeros_like(st_s)
        ubuf[0:CONV_PAD, :] = jnp.zeros((CONV_PAD, CV_WIDTH), F32)

    mlp_prologue()

    x = x_ref[0]
    sh1 = mod_ref[b_mix, 0:1, :]
    sc1 = mod_ref[b_mix, 1:2, :]
    g1 = mod_ref[b_mix, 2:3, :]
    hb = (x * (1.0 + sc1) + sh1).astype(BF16)
    sec = functools.partial(_section, hb, w_in_ref, b_in_ref)
    lb = _lower_bound(lb_ref)

    ubuf[CONV_PAD:CONV_PAD + tile, :] = (sec(4 * HG_WIDTH, CV_WIDTH)
                                         * jax.nn.sigmoid(sec(4 * HG_WIDTH + CV_WIDTH, CV_WIDTH)))
    first_off = CONV_PAD - CV_BUF
    shifted_rows = ush.shape[1]

    def conv_piece(r0, cs):
        accs = [jnp.broadcast_to(cb_ref[:, cs], (CONV_ROWS, LANES))] + [None] * (CONV_CHAINS - 1)
        for phase in range(SUBLANES):
            taps = [j for j in range(CV_KERNEL) if (first_off + j) % SUBLANES == phase]
            span = (first_off + taps[-1]) // SUBLANES * SUBLANES
            rows = slice(r0, r0 + span + CONV_ROWS)
            win = ubuf[rows, cs] if phase == 0 else ush[phase - 1, rows, :]
            c = phase % CONV_CHAINS
            for j in taps:
                a0 = first_off + j - phase
                term = cw_ref[j:j + 1, cs] * win[a0:a0 + CONV_ROWS, :]
                accs[c] = term if accs[c] is None else accs[c] + term
        uc_s[r0:r0 + CONV_ROWS, cs] = functools.reduce(lambda a, b: a + b, accs)

    def conv_pieces(n):
        cs = slice(n * LANES, (n + 1) * LANES)
        for phase in range(1, SUBLANES):
            ush[phase - 1] = ubuf[phase:phase + shifted_rows, cs]
        for r0 in range(0, tile, CONV_ROWS):
            conv_piece(r0, cs)

    next_up()
    q_s[...] = _silu(sec(0, HG_WIDTH))
    conv_pieces(0)
    next_up()
    f = lb + (1.0 - lb) * jax.nn.sigmoid(sec(HG_WIDTH, HG_WIDTH))
    k_s[...] = 1.0 - f
    lf_s[...] = jnp.log(f)
    conv_pieces(1)
    next_up()
    v_s[...] = sec(2 * HG_WIDTH, HG_WIDTH).astype(BF16)
    conv_pieces(2)
    next_up()
    g_s[...] = _silu(sec(3 * HG_WIDTH, HG_WIDTH))
    conv_pieces(3)

    row = lax.broadcasted_iota(jnp.int32, (CHUNK, HG_WIDTH), 0)
    causal = (lax.broadcasted_iota(jnp.int32, (CHUNK, CHUNK), 0)
              >= lax.broadcasted_iota(jnp.int32, (CHUNK, CHUNK), 1))
    hgw = hgw_ref[...]
    for c in range(tile // CHUNK):
        next_up()
        rows = slice(c * CHUNK, (c + 1) * CHUNK)
        b = lf_s[rows, :]
        shift = 1
        while shift < CHUNK:
            b = b + jnp.where(row >= shift, pltpu.roll(b, shift, axis=0), 0.0)
            shift *= 2
        b_last = b[CHUNK - 1:CHUNK, :]
        kc = k_s[rows, :]
        qe = (q_s[rows, :] * jnp.exp(b)).astype(BF16)
        ke = (kc * jnp.exp(-b)).astype(BF16)
        kd = (kc * jnp.exp(b_last - b)).astype(BF16)
        decay = jnp.exp(b_last)
        vb = v_s[rows, :]
        for h in range(HG_HEADS):
            hs = slice(h * HG_DK, (h + 1) * HG_DK)
            att = jnp.where(causal, _dot_nt(qe[:, hs], ke[:, hs]), 0.0).astype(BF16)
            st = st_s[h]
            o = _dot(att, vb[:, hs]) + _dot_nt(qe[:, hs], st.astype(BF16))
            st_s[h] = st * decay[:, hs] + _dot_tn(vb[:, hs], kd[:, hs])
            mix_s[rows, hs] = _head_norm(o, hgw, g_s[rows, hs]).astype(BF16)
    while up_chunks:
        next_up()

    next_down()
    mix_s[:, HG_WIDTH:] = _group_norm_swish(
        uc_s[...], gmat_ref, gng_ref[...], gnb_ref[...]).astype(BF16)
    ubuf[0:CONV_PAD, :] = ubuf[tile:tile + CONV_PAD, :]
    next_down()
    mix = _out_projection(mix_s[...], w_out_ref, b_out_ref)
    x1_s[...] = _layer_norm(ALPHA * x + (1.0 + g1) * mix, ln1g_ref[...], ln1b_ref[...])
    while down_chunks:
        next_down()
    mlp_epilogue()

    @pl.when((t == tiles_per_seq - 1) & (i < n_tiles))
    def _():
        for h in range(HG_HEADS):
            s_out_ref[0, h] = st_s[h].T
        cbuf_out_ref[0] = ubuf[CONV_PAD - CV_BUF:CONV_PAD, :]


def _const_spec(shape):
    zeros = (0,) * len(shape)
    return pl.BlockSpec(shape, lambda *_: zeros, pipeline_mode=pl.Buffered(1))


def _prompt_layer(x, mod, lb_logits, w_in, b_in, hgw, conv_w, conv_b, gn_g, gn_b, gmat,
                  w_out, b_out, ln1_g, ln1_b, w_up, b_up, w_down, b_down, ln2_g, ln2_b, *, tile):
    batch, seq, _ = x.shape
    tiles_per_seq = seq // tile
    n_tiles = batch * tiles_per_seq
    kernel = functools.partial(_prompt_kernel, tile=tile, tiles_per_seq=tiles_per_seq,
                               n_tiles=n_tiles)

    def mixer_block(i):
        m = jnp.minimum(i, n_tiles - 1)
        return m // tiles_per_seq, m % tiles_per_seq

    def mlp_block(i):
        m = jnp.maximum(i - 1, 0)
        return m // tiles_per_seq, m % tiles_per_seq

    consts = (mod, lb_logits, w_in, b_in, hgw, conv_w, conv_b, gn_g, gn_b, gmat, w_out, b_out,
              ln1_g, ln1_b, w_up, b_up, w_down, b_down, ln2_g, ln2_b)
    return pl.pallas_call(
        kernel,
        grid=(n_tiles + 1,),
        in_specs=[pl.BlockSpec((1, tile, D_MODEL), lambda i: (*mixer_block(i), 0))]
        + [_const_spec(a.shape) for a in consts],
        out_specs=[
            pl.BlockSpec((1, tile, D_MODEL), lambda i: (*mlp_block(i), 0)),
            pl.BlockSpec((1, HG_HEADS, HG_DK, HG_DK), lambda i: (mixer_block(i)[0], 0, 0, 0)),
            pl.BlockSpec((1, CV_BUF, CV_WIDTH), lambda i: (mixer_block(i)[0], 0, 0)),
        ],
        out_shape=[
            jax.ShapeDtypeStruct((batch, seq, D_MODEL), F32),
            jax.ShapeDtypeStruct((batch, HG_HEADS, HG_DK, HG_DK), F32),
            jax.ShapeDtypeStruct((batch, CV_BUF, CV_WIDTH), F32),
        ],
        scratch_shapes=[
            pltpu.VMEM((tile, HG_WIDTH), F32),
            pltpu.VMEM((tile, HG_WIDTH), F32),
            pltpu.VMEM((tile, HG_WIDTH), F32),
            pltpu.VMEM((tile, HG_WIDTH), BF16),
            pltpu.VMEM((tile, HG_WIDTH), F32),
            pltpu.VMEM((CONV_PAD + tile, CV_WIDTH), F32),
            pltpu.VMEM((SUBLANES - 1, CONV_PAD + tile - SUBLANES, LANES), F32),
            pltpu.VMEM((tile, CV_WIDTH), F32),
            pltpu.VMEM((tile, D_MODEL), BF16),
            pltpu.VMEM((HG_HEADS, HG_DK, HG_DK), F32),
            pltpu.VMEM((tile, D_MODEL), F32),
            pltpu.VMEM((tile, D_MODEL), BF16),
            pltpu.VMEM((tile, D_MODEL), F32),
            pltpu.VMEM((tile, D_FF), BF16),
            pltpu.VMEM((tile, D_MODEL), F32),
        ],
        compiler_params=pltpu.CompilerParams(
            dimension_semantics=("arbitrary",), vmem_limit_bytes=BIG_VMEM_LIMIT_BYTES),
        name="prompt_layer",
    )(x, *consts)


def _mixer_sample_kernel(x_ref, mod_ref, s_in_ref, cbuf_in_ref, lb_ref, w_in_ref, b_in_ref, hgw_ref,
                         cw_ref, cb_ref, gng_ref, gnb_ref, gmat_ref, w_out_ref, b_out_ref,
                         lng_ref, lnb_ref,
                         x1_ref, s_out_ref, cbuf_out_ref,
                         qe_s, ke_s, kd_s, v_s, o_s, *, seqs):
    rows_n = seqs * DEC_SEQ
    x = x_ref[...]
    sh1 = mod_ref[:, 0:D_MODEL]
    sc1 = mod_ref[:, D_MODEL:2 * D_MODEL]
    g1 = mod_ref[:, 2 * D_MODEL:3 * D_MODEL]
    hb = (x * (1.0 + sc1) + sh1).astype(BF16).reshape(rows_n, D_MODEL)
    q, k, lf, v, gate, u = _in_projection(hb, _lower_bound(lb_ref), w_in_ref, b_in_ref)

    slab = lambda a, t: a[t * seqs:(t + 1) * seqs, :]
    b = [slab(lf, 0)]
    for t in range(1, DEC_SEQ):
        b.append(b[-1] + slab(lf, t))
    b_last = b[-1]
    decay = jnp.exp(b_last)
    pad = jnp.zeros((LANES - seqs, HG_DK), F32)
    decay_t = [jnp.concatenate([decay[:, h * HG_DK:(h + 1) * HG_DK], pad], axis=0).T
               for h in range(HG_HEADS)]

    def put(ref, t, val):
        for h in range(HG_HEADS):
            ref[h, t * seqs:(t + 1) * seqs, :] = val[:, h * HG_DK:(h + 1) * HG_DK]

    for t in range(DEC_SEQ):
        put(qe_s, t, slab(q, t) * jnp.exp(b[t]))
        put(ke_s, t, slab(k, t) * jnp.exp(-b[t]))
        put(kd_s, t, slab(k, t) * jnp.exp(b_last - b[t]))
        put(v_s, t, slab(v, t))

    stride = seqs // SEQ_GROUP
    n_rows = SEQ_GROUP * DEC_SEQ
    member = lax.broadcasted_iota(jnp.int32, (n_rows, HG_DK), 0) % SEQ_GROUP
    lanes = lambda g: slice(g * HG_DK, (g + 1) * HG_DK)
    per_member = lambda a: jnp.concatenate(
        [jnp.where(member == g, a, 0.0) for g in range(SEQ_GROUP)], axis=1)
    ri = lax.broadcasted_iota(jnp.int32, (n_rows, n_rows), 0)
    ci = lax.broadcasted_iota(jnp.int32, (n_rows, n_rows), 1)
    causal = (ri // SEQ_GROUP >= ci // SEQ_GROUP) & (ri % SEQ_GROUP == ci % SEQ_GROUP)
    for s in range(stride):
        rows = pl.ds(s, n_rows, stride=stride)
        for h in range(HG_HEADS):
            qe_g = qe_s[h, rows, :].astype(BF16)
            ke_g = ke_s[h, rows, :].astype(BF16)
            kd_g = kd_s[h, rows, :].astype(BF16)
            v_g = v_s[h, rows, :]
            s_g = jnp.concatenate([s_in_ref[s + g * stride, h] for g in range(SEQ_GROUP)], axis=1)
            att = jnp.where(causal, _dot_nt(qe_g, ke_g), 0.0).astype(BF16)
            inter_all = _dot(qe_g, s_g.astype(BF16))
            inter = inter_all[:, lanes(0)]
            for g in range(1, SEQ_GROUP):
                inter = jnp.where(member == g, inter_all[:, lanes(g)], inter)
            o_s[h, rows, :] = _dot(att, v_g.astype(BF16)) + inter
            col = lambda g: decay_t[h][:, s + g * stride:s + g * stride + 1]
            decay_g = jnp.concatenate([jnp.broadcast_to(col(g), (HG_DK, HG_DK))
                                       for g in range(SEQ_GROUP)], axis=1)
            new = decay_g * s_g + _dot_tn(kd_g, per_member(v_g).astype(BF16))
            for g in range(SEQ_GROUP):
                s_out_ref[s + g * stride, h] = new[:, lanes(g)]
    hgw = hgw_ref[...]
    o_a = jnp.concatenate(
        [_head_norm(o_s[h], hgw, gate[:, h * HG_DK:(h + 1) * HG_DK]) for h in range(HG_HEADS)],
        axis=-1)

    full = lambda i: cbuf_in_ref[i] if i < CV_BUF else slab(u, i - CV_BUF)
    acc = [jnp.broadcast_to(cb_ref[...], (seqs, CV_WIDTH)) for _ in range(DEC_SEQ)]
    for i in range(CV_BUF + DEC_SEQ):
        f_i = full(i)
        for t in range(DEC_SEQ):
            if 0 <= i - t < CV_KERNEL:
                acc[t] = acc[t] + cw_ref[i - t:i - t + 1, :] * f_i
    for i in range(CV_BUF):
        cbuf_out_ref[i] = full(i + DEC_SEQ)
    o_b = _group_norm_swish(jnp.concatenate(acc, axis=0), gmat_ref, gng_ref[...], gnb_ref[...])

    mix = _out_projection(jnp.concatenate([o_a, o_b], axis=-1).astype(BF16), w_out_ref, b_out_ref)
    y = ALPHA * x + (1.0 + g1) * mix.reshape(DEC_SEQ, seqs, D_MODEL)
    x1_ref[...] = _layer_norm(y, lng_ref[...], lnb_ref[...])


def _mixer_sample(x, mod, s_in, cbuf_in, lb_logits, w_in, b_in, hgw, conv_w, conv_b, gn_g, gn_b,
                  gmat, w_out, b_out, ln_g, ln_b, *, seqs):
    n_seq = s_in.shape[0]
    rows = seqs * DEC_SEQ
    kernel = functools.partial(_mixer_sample_kernel, seqs=seqs)
    return pl.pallas_call(
        kernel,
        grid=(n_seq // seqs,),
        in_specs=[
            pl.BlockSpec((DEC_SEQ, seqs, D_MODEL), lambda i: (0, i, 0)),
            pl.BlockSpec((seqs, 3 * D_MODEL), lambda i: (i, 0)),
            pl.BlockSpec((seqs, HG_HEADS, HG_DK, HG_DK), lambda i: (i, 0, 0, 0)),
            pl.BlockSpec((CV_BUF, seqs, CV_WIDTH), lambda i: (0, i, 0)),
            _const_spec(lb_logits.shape),
            _const_spec(w_in.shape),
            _const_spec(b_in.shape),
            _const_spec(hgw.shape),
            _const_spec(conv_w.shape),
            _const_spec(conv_b.shape),
            _const_spec(gn_g.shape),
            _const_spec(gn_b.shape),
            _const_spec(gmat.shape),
            _const_spec(w_out.shape),
            _const_spec(b_out.shape),
            _const_spec(ln_g.shape),
            _const_spec(ln_b.shape),
        ],
        out_specs=[
            pl.BlockSpec((DEC_SEQ, seqs, D_MODEL), lambda i: (0, i, 0)),
            pl.BlockSpec((seqs, HG_HEADS, HG_DK, HG_DK), lambda i: (i, 0, 0, 0)),
            pl.BlockSpec((CV_BUF, seqs, CV_WIDTH), lambda i: (0, i, 0)),
        ],
        out_shape=[
            jax.ShapeDtypeStruct((DEC_SEQ, n_seq, D_MODEL), F32),
            jax.ShapeDtypeStruct(s_in.shape, F32),
            jax.ShapeDtypeStruct(cbuf_in.shape, F32),
        ],
        scratch_shapes=[pltpu.VMEM((HG_HEADS, rows, HG_DK), F32)] * 5,
        compiler_params=pltpu.CompilerParams(
            dimension_semantics=("arbitrary",), vmem_limit_bytes=BIG_VMEM_LIMIT_BYTES),
        name="mixer_sample",
    )(x, mod, s_in, cbuf_in, lb_logits, w_in, b_in, hgw, conv_w, conv_b, gn_g, gn_b, gmat,
      w_out, b_out, ln_g, ln_b)


def _mlp_kernel(x_ref, sh_ref, sc_ref, g_ref, w_up_ref, b_up_ref, w_down_ref, b_down_ref,
                lng_ref, lnb_ref, o_ref):
    x = x_ref[...]
    hb = (x * (1.0 + sc_ref[...]) + sh_ref[...]).astype(BF16).reshape(-1, D_MODEL)
    ff = jnp.broadcast_to(b_down_ref[...], hb.shape)
    for c0 in range(0, D_FF, FF_CHUNK):
        cs = slice(c0, c0 + FF_CHUNK)
        a = jnp.maximum(_dot(hb, w_up_ref[:, cs]) + b_up_ref[:, cs], 0.0)
        ff = ff + _dot((a * a).astype(BF16), w_down_ref[cs, :])
    y = ALPHA * x + (1.0 + g_ref[...]) * ff.reshape(x.shape)
    o_ref[...] = _layer_norm(y, lng_ref[...], lnb_ref[...])


def _mlp_sample(x, mod, w_up, b_up, w_down, b_down, ln_g, ln_b):
    x_spec = pl.BlockSpec(x.shape, lambda i: (0, 0, 0))
    mod_specs = [pl.BlockSpec((x.shape[1], D_MODEL), functools.partial(lambda i, r: (0, r), r=r))
                 for r in (3, 4, 5)]
    return pl.pallas_call(
        _mlp_kernel,
        grid=(1,),
        in_specs=[x_spec] + mod_specs + [
            _const_spec(w_up.shape),
            _const_spec(b_up.shape),
            _const_spec(w_down.shape),
            _const_spec(b_down.shape),
            _const_spec(ln_g.shape),
            _const_spec(ln_b.shape),
        ],
        out_specs=x_spec,
        out_shape=jax.ShapeDtypeStruct(x.shape, F32),
        compiler_params=pltpu.CompilerParams(
            dimension_semantics=("arbitrary",), vmem_limit_bytes=VMEM_LIMIT_BYTES),
        name="mlp_sample",
    )(x, mod, mod, mod, w_up, b_up, w_down, b_down, ln_g, ln_b)


PROMPT_TILE = 512
SAMPLE_SEQS = 32


def kernel(x_prompt, x_sample, c_prompt, c_sample, state_hgrn, state_conv, lb_logits, w_in, b_in,
           hg_norm_w, conv_w, conv_b, gn_g, gn_b, w_out, b_out, ln1_g, ln1_b, w_up, b_up, w_down,
           b_down, ln2_g, ln2_b, w_ada, b_ada):
    assert w_in.shape[0] == DEPTH
    batch, seq, _ = x_prompt.shape
    dec_batch, dec_seq, _ = x_sample.shape
    assert dec_seq == DEC_SEQ

    group = jnp.arange(MXU_WIDTH, dtype=jnp.int32) // (CV_WIDTH // CV_GROUPS)
    first = jnp.arange(MXU_WIDTH, dtype=jnp.int32)[:, None] == group[None, :] * (CV_WIDTH // CV_GROUPS)
    gmat = jnp.stack([jnp.where(first, 1.0, 0.0),
                      jnp.where(group[:, None] == group[None, :], CV_GROUPS / CV_WIDTH, 0.0)]
                     ).astype(BF16)

    xp = x_prompt
    xs = jnp.transpose(x_sample, (1, 0, 2))
    hp, cp, hs, cs = [], [], [], []
    for l in range(DEPTH):
        row = lambda a: a[l][None, :]
        w_in_b, w_out_b = _to_bf16(w_in[l]), _to_bf16(w_out[l])
        w_up_b, w_down_b = _to_bf16(w_up[l]), _to_bf16(w_down[l])

        mod_p, mod_s = _adaln(c_prompt, c_sample, w_ada[l], row(b_ada))
        mod_p = mod_p.reshape(batch, 6, D_MODEL)

        mixer_params = (lb_logits, w_in_b, row(b_in), row(hg_norm_w), conv_w[l], row(conv_b),
                        row(gn_g), row(gn_b), gmat, w_out_b, row(b_out), row(ln1_g), row(ln1_b))
        mlp_params = (w_up_b, row(b_up), w_down_b, row(b_down), row(ln2_g), row(ln2_b))
        xp, sp, bp = _prompt_layer(xp, mod_p, *mixer_params, *mlp_params, tile=PROMPT_TILE)
        xs, ss, bs = _mixer_sample(xs, mod_s, state_hgrn[l], jnp.transpose(state_conv[l], (1, 0, 2)),
                                   *mixer_params, seqs=SAMPLE_SEQS)
        xs = _mlp_sample(xs, mod_s, *mlp_params)

        hp.append(sp); cp.append(bp); hs.append(ss); cs.append(jnp.transpose(bs, (1, 0, 2)))

    return (xp, jnp.transpose(xs, (1, 0, 2)), jnp.stack(hp), jnp.stack(cp), jnp.stack(hs),
            jnp.stack(cs))
```

```python
import functools

import jax
import jax.numpy as jnp
from jax import lax
from jax.experimental import pallas as pl
from jax.experimental.pallas import tpu as pltpu

F32 = jnp.float32
BF16 = jnp.bfloat16

D_MODEL = 1024
HG_WIDTH = 512
CV_WIDTH = 512
HG_HEADS = 4
HG_DK = 128
CV_GROUPS = 8
CV_KERNEL = 31
CV_BUF = CV_KERNEL - 1
D_FF = 4 * D_MODEL
N_IN = 4 * HG_WIDTH + 2 * CV_WIDTH
DEPTH = 1
ALPHA = (2.0 * DEPTH) ** 0.25
EPS = 1e-5

CHUNK = 64
SUBLANES = 8
LANES = 128
MXU_WIDTH = 256
CONV_PAD = 32
CONV_ROWS = 128
CONV_CHAINS = 2
UP_CHUNK = 512
DOWN_CHUNK = 256
FF_CHUNK = 1024
DEC_SEQ = 4
SEQ_GROUP = 16

VMEM_LIMIT_BYTES = 52 * 1024 * 1024
BIG_VMEM_LIMIT_BYTES = 60 * 1024 * 1024
CAST_BLOCK_BYTES = 4 * 1024 * 1024
ADALN_COLS = 1024

_NT = (((1,), (1,)), ((), ()))
_TN = (((0,), (0,)), ((), ()))


def _dot(a, b):
    return jnp.dot(a, b, preferred_element_type=F32)


def _dot_nt(a, b):
    return lax.dot_general(a, b, _NT, preferred_element_type=F32)


def _dot_tn(a, b):
    return lax.dot_general(a, b, _TN, preferred_element_type=F32)


def _silu(x):
    return x * jax.nn.sigmoid(x)


def _layer_norm(y, g, b):
    mu = jnp.mean(y, axis=-1, keepdims=True)
    yc = y - mu
    var = jnp.mean(yc * yc, axis=-1, keepdims=True)
    return yc * lax.rsqrt(var + EPS) * g + b


def _adaln_kernel(cp_ref, cs_ref, w_ref, b_ref, op_ref, os_ref):
    a = _silu(jnp.concatenate([cp_ref[...], cs_ref[...]], axis=0))
    a_hi = a.astype(BF16)
    a_lo = (a - a_hi.astype(F32)).astype(BF16)
    w = w_ref[...].astype(BF16)
    mod = _dot(a_hi, w) + _dot(a_lo, w) + b_ref[...]
    n_prompt = cp_ref.shape[0]
    op_ref[...] = mod[:n_prompt]
    os_ref[...] = mod[n_prompt:]


def _adaln(c_prompt, c_sample, w_ada, b_ada):
    n = w_ada.shape[1]
    tn = ADALN_COLS
    rows = lambda c: pl.BlockSpec((c.shape[0], D_MODEL), lambda j: (0, 0))
    cols = lambda c: pl.BlockSpec((c.shape[0], tn), lambda j: (0, j))
    return pl.pallas_call(
        _adaln_kernel,
        grid=(n // tn,),
        in_specs=[
            rows(c_prompt),
            rows(c_sample),
            pl.BlockSpec((D_MODEL, tn), lambda j: (0, j)),
            pl.BlockSpec((1, tn), lambda j: (0, j)),
        ],
        out_specs=[cols(c_prompt), cols(c_sample)],
        out_shape=[jax.ShapeDtypeStruct((c.shape[0], n), F32) for c in (c_prompt, c_sample)],
        compiler_params=pltpu.CompilerParams(
            dimension_semantics=("arbitrary",), vmem_limit_bytes=VMEM_LIMIT_BYTES),
        name="adaln",
    )(c_prompt, c_sample, w_ada, b_ada)


def _cast_kernel(w_ref, o_ref):
    o_ref[...] = w_ref[...].astype(o_ref.dtype)


def _to_bf16(w):
    rows, cols = w.shape
    block_rows = rows
    while block_rows * cols * w.dtype.itemsize > CAST_BLOCK_BYTES and block_rows % 16 == 0:
        block_rows //= 2
    spec = pl.BlockSpec((block_rows, cols), lambda i: (i, 0))
    return pl.pallas_call(
        _cast_kernel,
        grid=(rows // block_rows,),
        in_specs=[spec],
        out_specs=spec,
        out_shape=jax.ShapeDtypeStruct(w.shape, BF16),
        compiler_params=pltpu.CompilerParams(
            dimension_semantics=("arbitrary",), vmem_limit_bytes=VMEM_LIMIT_BYTES),
        name="to_bf16",
    )(w)


def _lower_bound(lb_ref):
    lg = lb_ref[...]
    e = jnp.exp(lg - jnp.max(lg, axis=0, keepdims=True))
    p = e / jnp.sum(e, axis=0, keepdims=True)
    return jnp.sum(p[0:DEPTH, :], axis=0, keepdims=True)


def _section(hb, w_in_ref, b_in_ref, i, width):
    return _dot(hb, w_in_ref[:, i:i + width]) + b_in_ref[:, i:i + width]


def _in_projection(hb, lb, w_in_ref, b_in_ref):
    sec = functools.partial(_section, hb, w_in_ref, b_in_ref)

    q = _silu(sec(0, HG_WIDTH))
    f = lb + (1.0 - lb) * jax.nn.sigmoid(sec(HG_WIDTH, HG_WIDTH))
    v = sec(2 * HG_WIDTH, HG_WIDTH)
    gate = _silu(sec(3 * HG_WIDTH, HG_WIDTH))
    za = sec(4 * HG_WIDTH, CV_WIDTH)
    zb = sec(4 * HG_WIDTH + CV_WIDTH, CV_WIDTH)
    return q, 1.0 - f, jnp.log(f), v, gate, za * jax.nn.sigmoid(zb)


def _head_norm(o, hgw, gate):
    ms = jnp.mean(o * o, axis=-1, keepdims=True)
    return o * lax.rsqrt(ms + EPS) * hgw * gate


def _group_norm_swish(uc, gmat_ref, gn_g, gn_b):
    width = gmat_ref.shape[1]

    def per_group(a, which):
        ab = a.astype(BF16)
        return jnp.concatenate([_dot(ab[:, c0:c0 + width], gmat_ref[which])
                                for c0 in range(0, ab.shape[1], width)], axis=-1)

    d0 = uc - per_group(uc, 0)
    d = d0 - per_group(d0, 1)
    var = per_group(d * d, 1)
    return _silu(d * lax.rsqrt(var + EPS) * gn_g + gn_b)


def _out_projection(mix_bf16, w_out_ref, b_out_ref):
    return _dot(mix_bf16, w_out_ref[...]) + b_out_ref[...]


def _prompt_kernel(x_ref, mod_ref, lb_ref, w_in_ref, b_in_ref, hgw_ref, cw_ref, cb_ref,
                   gng_ref, gnb_ref, gmat_ref, w_out_ref, b_out_ref, ln1g_ref, ln1b_ref,
                   w_up_ref, b_up_ref, w_down_ref, b_down_ref, ln2g_ref, ln2b_ref,
                   y_ref, s_out_ref, cbuf_out_ref,
                   q_s, k_s, lf_s, v_s, g_s, ubuf, ush, uc_s, mix_s, st_s,
                   x1_s, hb2_s, res2_s, act_s, ff_s, *, tile, tiles_per_seq, n_tiles):
    i = pl.program_id(0)
    mixer_tile = jnp.minimum(i, n_tiles - 1)
    b_mix = mixer_tile // tiles_per_seq
    t = mixer_tile % tiles_per_seq
    b_mlp = jnp.maximum(i - 1, 0) // tiles_per_seq

    def mlp_prologue():
        x1_prev = x1_s[...]
        sh2 = mod_ref[b_mlp, 3:4, :]
        sc2 = mod_ref[b_mlp, 4:5, :]
        hb2_s[...] = (x1_prev * (1.0 + sc2) + sh2).astype(BF16)
        res2_s[...] = ALPHA * x1_prev

    def mlp_epilogue():
        g2 = mod_ref[b_mlp, 5:6, :]
        y_ref[0] = _layer_norm(res2_s[...] + (1.0 + g2) * ff_s[...], ln2g_ref[...], ln2b_ref[...])

    def mlp_up(c0):
        cs = slice(c0, c0 + UP_CHUNK)
        a = jnp.maximum(_dot(hb2_s[...], w_up_ref[:, cs]) + b_up_ref[:, cs], 0.0)
        act_s[:, cs] = (a * a).astype(BF16)

    def mlp_down(c0):
        cs = slice(c0, c0 + DOWN_CHUNK)
        ff_s[:, cs] = _dot(act_s[...], w_down_ref[:, cs]) + b_down_ref[:, cs]

    up_chunks = list(range(0, D_FF, UP_CHUNK))
    down_chunks = list(range(0, D_MODEL, DOWN_CHUNK))

    def next_up():
        if up_chunks:
            mlp_up(up_chunks.pop(0))

    def next_down():
        if down_chunks:
            mlp_down(down_chunks.pop(0))

    @pl.when(i == 0)
    def _():
        x1_s[...] = jnp.zeros_like(x1_s)

    @pl.when(t == 0)
    def _():
        st_s[...] = jnp.zeros_like(st_s)
        ubuf[0:CONV_PAD, :] = jnp.zeros((CONV_PAD, CV_WIDTH), F32)

    mlp_prologue()

    x = x_ref[0]
    sh1 = mod_ref[b_mix, 0:1, :]
    sc1 = mod_ref[b_mix, 1:2, :]
    g1 = mod_ref[b_mix, 2:3, :]
    hb = (x * (1.0 + sc1) + sh1).astype(BF16)
    sec = functools.partial(_section, hb, w_in_ref, b_in_ref)
    lb = _lower_bound(lb_ref)

    ubuf[CONV_PAD:CONV_PAD + tile, :] = (sec(4 * HG_WIDTH, CV_WIDTH)
                                         * jax.nn.sigmoid(sec(4 * HG_WIDTH + CV_WIDTH, CV_WIDTH)))
    first_off = CONV_PAD - CV_BUF
    shifted_rows = ush.shape[1]

    def conv_piece(r0, cs):
        accs = [jnp.broadcast_to(cb_ref[:, cs], (CONV_ROWS, LANES))] + [None] * (CONV_CHAINS - 1)
        for phase in range(SUBLANES):
            taps = [j for j in range(CV_KERNEL) if (first_off + j) % SUBLANES == phase]
            span = (first_off + taps[-1]) // SUBLANES * SUBLANES
            rows = slice(r0, r0 + span + CONV_ROWS)
            win = ubuf[rows, cs] if phase == 0 else ush[phase - 1, rows, :]
            c = phase % CONV_CHAINS
            for j in taps:
                a0 = first_off + j - phase
                term = cw_ref[j:j + 1, cs] * win[a0:a0 + CONV_ROWS, :]
                accs[c] = term if accs[c] is None else accs[c] + term
        uc_s[r0:r0 + CONV_ROWS, cs] = functools.reduce(lambda a, b: a + b, accs)

    def conv_pieces(n):
        cs = slice(n * LANES, (n + 1) * LANES)
        for phase in range(1, SUBLANES):
            ush[phase - 1] = ubuf[phase:phase + shifted_rows, cs]
        for r0 in range(0, tile, CONV_ROWS):
            conv_piece(r0, cs)

    next_up()
    q_s[...] = _silu(sec(0, HG_WIDTH))
    conv_pieces(0)
    next_up()
    f = lb + (1.0 - lb) * jax.nn.sigmoid(sec(HG_WIDTH, HG_WIDTH))
    k_s[...] = 1.0 - f
    lf_s[...] = jnp.log(f)
    conv_pieces(1)
    next_up()
    v_s[...] = sec(2 * HG_WIDTH, HG_WIDTH).astype(BF16)
    conv_pieces(2)
    next_up()
    g_s[...] = _silu(sec(3 * HG_WIDTH, HG_WIDTH))
    conv_pieces(3)

    row = lax.broadcasted_iota(jnp.int32, (CHUNK, HG_WIDTH), 0)
    causal = (lax.broadcasted_iota(jnp.int32, (CHUNK, CHUNK), 0)
              >= lax.broadcasted_iota(jnp.int32, (CHUNK, CHUNK), 1))
    hgw = hgw_ref[...]
    for c in range(tile // CHUNK):
        next_up()
        rows = slice(c * CHUNK, (c + 1) * CHUNK)
        b = lf_s[rows, :]
        shift = 1
        while shift < CHUNK:
            b = b + jnp.where(row >= shift, pltpu.roll(b, shift, axis=0), 0.0)
            shift *= 2
        b_last = b[CHUNK - 1:CHUNK, :]
        kc = k_s[rows, :]
        qe = (q_s[rows, :] * jnp.exp(b)).astype(BF16)
        ke = (kc * jnp.exp(-b)).astype(BF16)
        kd = (kc * jnp.exp(b_last - b)).astype(BF16)
        decay = jnp.exp(b_last)
        vb = v_s[rows, :]
        for h in range(HG_HEADS):
            hs = slice(h * HG_DK, (h + 1) * HG_DK)
            att = jnp.where(causal, _dot_nt(qe[:, hs], ke[:, hs]), 0.0).astype(BF16)
            st = st_s[h]
            o = _dot(att, vb[:, hs]) + _dot_nt(qe[:, hs], st.astype(BF16))
            st_s[h] = st * decay[:, hs] + _dot_tn(vb[:, hs], kd[:, hs])
            mix_s[rows, hs] = _head_norm(o, hgw, g_s[rows, hs]).astype(BF16)
    while up_chunks:
        next_up()

    next_down()
    mix_s[:, HG_WIDTH:] = _group_norm_swish(
        uc_s[...], gmat_ref, gng_ref[...], gnb_ref[...]).astype(BF16)
    ubuf[0:CONV_PAD, :] = ubuf[tile:tile + CONV_PAD, :]
    next_down()
    mix = _out_projection(mix_s[...], w_out_ref, b_out_ref)
    x1_s[...] = _layer_norm(ALPHA * x + (1.0 + g1) * mix, ln1g_ref[...], ln1b_ref[...])
    while down_chunks:
        next_down()
    mlp_epilogue()

    @pl.when((t == tiles_per_seq - 1) & (i < n_tiles))
    def _():
        for h in range(HG_HEADS):
            s_out_ref[0, h] = st_s[h].T
        cbuf_out_ref[0] = ubuf[CONV_PAD - CV_BUF:CONV_PAD, :]


def _const_spec(shape):
    zeros = (0,) * len(shape)
    return pl.BlockSpec(shape, lambda *_: zeros, pipeline_mode=pl.Buffered(1))


def _prompt_layer(x, mod, lb_logits, w_in, b_in, hgw, conv_w, conv_b, gn_g, gn_b, gmat,
                  w_out, b_out, ln1_g, ln1_b, w_up, b_up, w_down, b_down, ln2_g, ln2_b, *, tile):
    batch, seq, _ = x.shape
    tiles_per_seq = seq // tile
    n_tiles = batch * tiles_per_seq
    kernel = functools.partial(_prompt_kernel, tile=tile, tiles_per_seq=tiles_per_seq,
                               n_tiles=n_tiles)

    def mixer_block(i):
        m = jnp.minimum(i, n_tiles - 1)
        return m // tiles_per_seq, m % tiles_per_seq

    def mlp_block(i):
        m = jnp.maximum(i - 1, 0)
        return m // tiles_per_seq, m % tiles_per_seq

    consts = (mod, lb_logits, w_in, b_in, hgw, conv_w, conv_b, gn_g, gn_b, gmat, w_out, b_out,
              ln1_g, ln1_b, w_up, b_up, w_down, b_down, ln2_g, ln2_b)
    return pl.pallas_call(
        kernel,
        grid=(n_tiles + 1,),
        in_specs=[pl.BlockSpec((1, tile, D_MODEL), lambda i: (*mixer_block(i), 0))]
        + [_const_spec(a.shape) for a in consts],
        out_specs=[
            pl.BlockSpec((1, tile, D_MODEL), lambda i: (*mlp_block(i), 0)),
            pl.BlockSpec((1, HG_HEADS, HG_DK, HG_DK), lambda i: (mixer_block(i)[0], 0, 0, 0)),
            pl.BlockSpec((1, CV_BUF, CV_WIDTH), lambda i: (mixer_block(i)[0], 0, 0)),
        ],
        out_shape=[
            jax.ShapeDtypeStruct((batch, seq, D_MODEL), F32),
            jax.ShapeDtypeStruct((batch, HG_HEADS, HG_DK, HG_DK), F32),
            jax.ShapeDtypeStruct((batch, CV_BUF, CV_WIDTH), F32),
        ],
        scratch_shapes=[
            pltpu.VMEM((tile, HG_WIDTH), F32),
            pltpu.VMEM((tile, HG_WIDTH), F32),
            pltpu.VMEM((tile, HG_WIDTH), F32),
            pltpu.VMEM((tile, HG_WIDTH), BF16),
            pltpu.VMEM((tile, HG_WIDTH), F32),
            pltpu.VMEM((CONV_PAD + tile, CV_WIDTH), F32),
            pltpu.VMEM((SUBLANES - 1, CONV_PAD + tile - SUBLANES, LANES), F32),
            pltpu.VMEM((tile, CV_WIDTH), F32),
            pltpu.VMEM((tile, D_MODEL), BF16),
            pltpu.VMEM((HG_HEADS, HG_DK, HG_DK), F32),
            pltpu.VMEM((tile, D_MODEL), F32),
            pltpu.VMEM((tile, D_MODEL), BF16),
            pltpu.VMEM((tile, D_MODEL), F32),
            pltpu.VMEM((tile, D_FF), BF16),
            pltpu.VMEM((tile, D_MODEL), F32),
        ],
        compiler_params=pltpu.CompilerParams(
            dimension_semantics=("arbitrary",), vmem_limit_bytes=BIG_VMEM_LIMIT_BYTES),
        name="prompt_layer",
    )(x, *consts)


def _mixer_sample_kernel(x_ref, mod_ref, s_in_ref, cbuf_in_ref, lb_ref, w_in_ref, b_in_ref, hgw_ref,
                         cw_ref, cb_ref, gng_ref, gnb_ref, gmat_ref, w_out_ref, b_out_ref,
                         lng_ref, lnb_ref,
                         x1_ref, s_out_ref, cbuf_out_ref,
                         qe_s, ke_s, kd_s, v_s, o_s, *, seqs):
    rows_n = seqs * DEC_SEQ
    x = jnp.stack([x_ref[:, t, :] for t in range(DEC_SEQ)], axis=0)
    sh1 = mod_ref[:, 0:D_MODEL]
    sc1 = mod_ref[:, D_MODEL:2 * D_MODEL]
    g1 = mod_ref[:, 2 * D_MODEL:3 * D_MODEL]
    hb = (x * (1.0 + sc1) + sh1).astype(BF16).reshape(rows_n, D_MODEL)
    q, k, lf, v, gate, u = _in_projection(hb, _lower_bound(lb_ref), w_in_ref, b_in_ref)

    slab = lambda a, t: a[t * seqs:(t + 1) * seqs, :]
    b = [slab(lf, 0)]
    for t in range(1, DEC_SEQ):
        b.append(b[-1] + slab(lf, t))
    b_last = b[-1]
    decay = jnp.exp(b_last)
    pad = jnp.zeros((LANES - seqs, HG_DK), F32)
    decay_t = [jnp.concatenate([decay[:, h * HG_DK:(h + 1) * HG_DK], pad], axis=0).T
               for h in range(HG_HEADS)]

    def put(ref, t, val):
        for h in range(HG_HEADS):
            ref[h, t * seqs:(t + 1) * seqs, :] = val[:, h * HG_DK:(h + 1) * HG_DK]

    for t in range(DEC_SEQ):
        put(qe_s, t, slab(q, t) * jnp.exp(b[t]))
        put(ke_s, t, slab(k, t) * jnp.exp(-b[t]))
        put(kd_s, t, slab(k, t) * jnp.exp(b_last - b[t]))
        put(v_s, t, slab(v, t))

    stride = seqs // SEQ_GROUP
    n_rows = SEQ_GROUP * DEC_SEQ
    member = lax.broadcasted_iota(jnp.int32, (n_rows, HG_DK), 0) % SEQ_GROUP
    lanes = lambda g: slice(g * HG_DK, (g + 1) * HG_DK)
    per_member = lambda a: jnp.concatenate(
        [jnp.where(member == g, a, 0.0) for g in range(SEQ_GROUP)], axis=1)
    ri = lax.broadcasted_iota(jnp.int32, (n_rows, n_rows), 0)
    ci = lax.broadcasted_iota(jnp.int32, (n_rows, n_rows), 1)
    causal = (ri // SEQ_GROUP >= ci // SEQ_GROUP) & (ri % SEQ_GROUP == ci % SEQ_GROUP)
    for s in range(stride):
        rows = pl.ds(s, n_rows, stride=stride)
        for h in range(HG_HEADS):
            qe_g = qe_s[h, rows, :].astype(BF16)
            ke_g = ke_s[h, rows, :].astype(BF16)
            kd_g = kd_s[h, rows, :].astype(BF16)
            v_g = v_s[h, rows, :]
            s_g = jnp.concatenate([s_in_ref[s + g * stride, h] for g in range(SEQ_GROUP)], axis=1)
            att = jnp.where(causal, _dot_nt(qe_g, ke_g), 0.0).astype(BF16)
            inter_all = _dot(qe_g, s_g.astype(BF16))
            inter = inter_all[:, lanes(0)]
            for g in range(1, SEQ_GROUP):
                inter = jnp.where(member == g, inter_all[:, lanes(g)], inter)
            o_s[h, rows, :] = _dot(att, v_g.astype(BF16)) + inter
            col = lambda g: decay_t[h][:, s + g * stride:s + g * stride + 1]
            decay_g = jnp.concatenate([jnp.broadcast_to(col(g), (HG_DK, HG_DK))
                                       for g in range(SEQ_GROUP)], axis=1)
            new = decay_g * s_g + _dot_tn(kd_g, per_member(v_g).astype(BF16))
            for g in range(SEQ_GROUP):
                s_out_ref[s + g * stride, h] = new[:, lanes(g)]
    hgw = hgw_ref[...]
    o_a = jnp.concatenate(
        [_head_norm(o_s[h], hgw, gate[:, h * HG_DK:(h + 1) * HG_DK]) for h in range(HG_HEADS)],
        axis=-1)

    full = lambda i: cbuf_in_ref[i] if i < CV_BUF else slab(u, i - CV_BUF)
    acc = [jnp.broadcast_to(cb_ref[...], (seqs, CV_WIDTH)) for _ in range(DEC_SEQ)]
    for i in range(CV_BUF + DEC_SEQ):
        f_i = full(i)
        for t in range(DEC_SEQ):
            if 0 <= i - t < CV_KERNEL:
                acc[t] = acc[t] + cw_ref[i - t:i - t + 1, :] * f_i
    for i in range(CV_BUF):
        cbuf_out_ref[i] = full(i + DEC_SEQ)
    o_b = _group_norm_swish(jnp.concatenate(acc, axis=0), gmat_ref, gng_ref[...], gnb_ref[...])

    mix = _out_projection(jnp.concatenate([o_a, o_b], axis=-1).astype(BF16), w_out_ref, b_out_ref)
    y = ALPHA * x + (1.0 + g1) * mix.reshape(DEC_SEQ, seqs, D_MODEL)
    x1_ref[...] = _layer_norm(y, lng_ref[...], lnb_ref[...])


def _mixer_sample(x, mod, s_in, cbuf_in, lb_logits, w_in, b_in, hgw, conv_w, conv_b, gn_g, gn_b,
                  gmat, w_out, b_out, ln_g, ln_b, *, seqs):
    n_seq = s_in.shape[0]
    rows = seqs * DEC_SEQ
    kernel = functools.partial(_mixer_sample_kernel, seqs=seqs)
    return pl.pallas_call(
        kernel,
        grid=(n_seq // seqs,),
        in_specs=[
            pl.BlockSpec((seqs, DEC_SEQ, D_MODEL), lambda i: (i, 0, 0)),
            pl.BlockSpec((seqs, 3 * D_MODEL), lambda i: (i, 0)),
            pl.BlockSpec((seqs, HG_HEADS, HG_DK, HG_DK), lambda i: (i, 0, 0, 0)),
            pl.BlockSpec((CV_BUF, seqs, CV_WIDTH), lambda i: (0, i, 0)),
            _const_spec(lb_logits.shape),
            _const_spec(w_in.shape),
            _const_spec(b_in.shape),
            _const_spec(hgw.shape),
            _const_spec(conv_w.shape),
            _const_spec(conv_b.shape),
            _const_spec(gn_g.shape),
            _const_spec(gn_b.shape),
            _const_spec(gmat.shape),
            _const_spec(w_out.shape),
            _const_spec(b_out.shape),
            _const_spec(ln_g.shape),
            _const_spec(ln_b.shape),
        ],
        out_specs=[
            pl.BlockSpec((DEC_SEQ, seqs, D_MODEL), lambda i: (0, i, 0)),
            pl.BlockSpec((seqs, HG_HEADS, HG_DK, HG_DK), lambda i: (i, 0, 0, 0)),
            pl.BlockSpec((CV_BUF, seqs, CV_WIDTH), lambda i: (0, i, 0)),
        ],
        out_shape=[
            jax.ShapeDtypeStruct((DEC_SEQ, n_seq, D_MODEL), F32),
            jax.ShapeDtypeStruct(s_in.shape, F32),
            jax.ShapeDtypeStruct(cbuf_in.shape, F32),
        ],
        scratch_shapes=[pltpu.VMEM((HG_HEADS, rows, HG_DK), F32)] * 5,
        compiler_params=pltpu.CompilerParams(
            dimension_semantics=("arbitrary",), vmem_limit_bytes=BIG_VMEM_LIMIT_BYTES),
        name="mixer_sample",
    )(x, mod, s_in, cbuf_in, lb_logits, w_in, b_in, hgw, conv_w, conv_b, gn_g, gn_b, gmat,
      w_out, b_out, ln_g, ln_b)


def _mlp_kernel(x_ref, sh_ref, sc_ref, g_ref, w_up_ref, b_up_ref, w_down_ref, b_down_ref,
                lng_ref, lnb_ref, o_ref):
    x = x_ref[...]
    hb = (x * (1.0 + sc_ref[...]) + sh_ref[...]).astype(BF16).reshape(-1, D_MODEL)
    ff = jnp.broadcast_to(b_down_ref[...], hb.shape)
    for c0 in range(0, D_FF, FF_CHUNK):
        cs = slice(c0, c0 + FF_CHUNK)
        a = jnp.maximum(_dot(hb, w_up_ref[:, cs]) + b_up_ref[:, cs], 0.0)
        ff = ff + _dot((a * a).astype(BF16), w_down_ref[cs, :])
    y = ALPHA * x + (1.0 + g_ref[...]) * ff.reshape(x.shape)
    out = _layer_norm(y, lng_ref[...], lnb_ref[...])
    for t in range(DEC_SEQ):
        o_ref[:, t, :] = out[t]


def _mlp_sample(x, mod, w_up, b_up, w_down, b_down, ln_g, ln_b):
    x_spec = pl.BlockSpec(x.shape, lambda i: (0, 0, 0))
    out_shape = (x.shape[1], x.shape[0], x.shape[2])
    mod_specs = [pl.BlockSpec((x.shape[1], D_MODEL), functools.partial(lambda i, r: (0, r), r=r))
                 for r in (3, 4, 5)]
    return pl.pallas_call(
        _mlp_kernel,
        grid=(1,),
        in_specs=[x_spec] + mod_specs + [
            _const_spec(w_up.shape),
            _const_spec(b_up.shape),
            _const_spec(w_down.shape),
            _const_spec(b_down.shape),
            _const_spec(ln_g.shape),
            _const_spec(ln_b.shape),
        ],
        out_specs=pl.BlockSpec(out_shape, lambda i: (0, 0, 0)),
        out_shape=jax.ShapeDtypeStruct(out_shape, F32),
        compiler_params=pltpu.CompilerParams(
            dimension_semantics=("arbitrary",), vmem_limit_bytes=VMEM_LIMIT_BYTES),
        name="mlp_sample",
    )(x, mod, mod, mod, w_up, b_up, w_down, b_down, ln_g, ln_b)


PROMPT_TILE = 512
SAMPLE_SEQS = 32


def kernel(x_prompt, x_sample, c_prompt, c_sample, state_hgrn, state_conv, lb_logits, w_in, b_in,
           hg_norm_w, conv_w, conv_b, gn_g, gn_b, w_out, b_out, ln1_g, ln1_b, w_up, b_up, w_down,
           b_down, ln2_g, ln2_b, w_ada, b_ada):
    assert w_in.shape[0] == DEPTH
    batch, seq, _ = x_prompt.shape
    dec_batch, dec_seq, _ = x_sample.shape
    assert dec_seq == DEC_SEQ

    group = jnp.arange(MXU_WIDTH, dtype=jnp.int32) // (CV_WIDTH // CV_GROUPS)
    first = jnp.arange(MXU_WIDTH, dtype=jnp.int32)[:, None] == group[None, :] * (CV_WIDTH // CV_GROUPS)
    gmat = jnp.stack([jnp.where(first, 1.0, 0.0),
                      jnp.where(group[:, None] == group[None, :], CV_GROUPS / CV_WIDTH, 0.0)]
                     ).astype(BF16)

    xp = x_prompt
    xs = x_sample
    hp, cp, hs, cs = [], [], [], []
    for l in range(DEPTH):
        row = lambda a: a[l][None, :]
        w_in_b, w_out_b = _to_bf16(w_in[l]), _to_bf16(w_out[l])
        w_up_b, w_down_b = _to_bf16(w_up[l]), _to_bf16(w_down[l])

        mod_p, mod_s = _adaln(c_prompt, c_sample, w_ada[l], row(b_ada))
        mod_p = mod_p.reshape(batch, 6, D_MODEL)

        mixer_params = (lb_logits, w_in_b, row(b_in), row(hg_norm_w), conv_w[l], row(conv_b),
                        row(gn_g), row(gn_b), gmat, w_out_b, row(b_out), row(ln1_g), row(ln1_b))
        mlp_params = (w_up_b, row(b_up), w_down_b, row(b_down), row(ln2_g), row(ln2_b))
        xp, sp, bp = _prompt_layer(xp, mod_p, *mixer_params, *mlp_params, tile=PROMPT_TILE)
        xs, ss, bs = _mixer_sample(xs, mod_s, state_hgrn[l], jnp.transpose(state_conv[l], (1, 0, 2)),
                                   *mixer_params, seqs=SAMPLE_SEQS)
        xs = _mlp_sample(xs, mod_s, *mlp_params)

        hp.append(sp); cp.append(bp); hs.append(ss); cs.append(jnp.transpose(bs, (1, 0, 2)))

    return (xp, xs, jnp.stack(hp), jnp.stack(cp), jnp.stack(hs),
            jnp.stack(cs))
```

```python
import functools

import jax
import jax.numpy as jnp
from jax import lax
from jax.experimental import pallas as pl
from jax.experimental.pallas import tpu as pltpu

F32 = jnp.float32
BF16 = jnp.bfloat16

D_MODEL = 1024
HG_WIDTH = 512
CV_WIDTH = 512
HG_HEADS = 4
HG_DK = 128
CV_GROUPS = 8
CV_KERNEL = 31
CV_BUF = CV_KERNEL - 1
D_FF = 4 * D_MODEL
N_IN = 4 * HG_WIDTH + 2 * CV_WIDTH
DEPTH = 1
ALPHA = (2.0 * DEPTH) ** 0.25
EPS = 1e-5

CHUNK = 64
SUBLANES = 8
LANES = 128
MXU_WIDTH = 256
CONV_PAD = 32
CONV_ROWS = 128
CONV_CHAINS = 2
UP_CHUNK = 512
DOWN_CHUNK = 256
FF_CHUNK = 1024
DEC_SEQ = 4
SEQ_GROUP = 16

VMEM_LIMIT_BYTES = 52 * 1024 * 1024
BIG_VMEM_LIMIT_BYTES = 60 * 1024 * 1024
CAST_BLOCK_BYTES = 8 * 1024 * 1024
ADALN_COLS = 2048

_NT = (((1,), (1,)), ((), ()))
_TN = (((0,), (0,)), ((), ()))


def _dot(a, b):
    return jnp.dot(a, b, preferred_element_type=F32)


def _dot_nt(a, b):
    return lax.dot_general(a, b, _NT, preferred_element_type=F32)


def _dot_tn(a, b):
    return lax.dot_general(a, b, _TN, preferred_element_type=F32)


def _silu(x):
    return x * jax.nn.sigmoid(x)


def _layer_norm(y, g, b):
    mu = jnp.mean(y, axis=-1, keepdims=True)
    yc = y - mu
    var = jnp.mean(yc * yc, axis=-1, keepdims=True)
    return yc * lax.rsqrt(var + EPS) * g + b


def _adaln_kernel(cp_ref, cs_ref, w_ref, b_ref, op_ref, os_ref):
    a = _silu(jnp.concatenate([cp_ref[...], cs_ref[...]], axis=0))
    a_hi = a.astype(BF16)
    a_lo = (a - a_hi.astype(F32)).astype(BF16)
    w = w_ref[...].astype(BF16)
    mod = _dot(a_hi, w) + _dot(a_lo, w) + b_ref[...]
    n_prompt = cp_ref.shape[0]
    op_ref[...] = mod[:n_prompt]
    os_ref[...] = mod[n_prompt:]


def _adaln(c_prompt, c_sample, w_ada, b_ada):
    n = w_ada.shape[1]
    tn = ADALN_COLS
    rows = lambda c: pl.BlockSpec((c.shape[0], D_MODEL), lambda j: (0, 0))
    cols = lambda c: pl.BlockSpec((c.shape[0], tn), lambda j: (0, j))
    return pl.pallas_call(
        _adaln_kernel,
        grid=(n // tn,),
        in_specs=[
            rows(c_prompt),
            rows(c_sample),
            pl.BlockSpec((D_MODEL, tn), lambda j: (0, j)),
            pl.BlockSpec((1, tn), lambda j: (0, j)),
        ],
        out_specs=[cols(c_prompt), cols(c_sample)],
        out_shape=[jax.ShapeDtypeStruct((c.shape[0], n), F32) for c in (c_prompt, c_sample)],
        compiler_params=pltpu.CompilerParams(
            dimension_semantics=("arbitrary",), vmem_limit_bytes=VMEM_LIMIT_BYTES),
        name="adaln",
    )(c_prompt, c_sample, w_ada, b_ada)


def _cast_kernel(w_ref, o_ref):
    o_ref[...] = w_ref[...].astype(o_ref.dtype)


def _to_bf16(w):
    rows, cols = w.shape
    block_rows = rows
    while block_rows * cols * w.dtype.itemsize > CAST_BLOCK_BYTES and block_rows % 16 == 0:
        block_rows //= 2
    spec = pl.BlockSpec((block_rows, cols), lambda i: (i, 0))
    return pl.pallas_call(
        _cast_kernel,
        grid=(rows // block_rows,),
        in_specs=[spec],
        out_specs=spec,
        out_shape=jax.ShapeDtypeStruct(w.shape, BF16),
        compiler_params=pltpu.CompilerParams(
            dimension_semantics=("arbitrary",), vmem_limit_bytes=VMEM_LIMIT_BYTES),
        name="to_bf16",
    )(w)


def _lower_bound(lb_ref):
    lg = lb_ref[...]
    e = jnp.exp(lg - jnp.max(lg, axis=0, keepdims=True))
    p = e / jnp.sum(e, axis=0, keepdims=True)
    return jnp.sum(p[0:DEPTH, :], axis=0, keepdims=True)


def _section(hb, w_in_ref, b_in_ref, i, width):
    return _dot(hb, w_in_ref[:, i:i + width]) + b_in_ref[:, i:i + width]


def _in_projection(hb, lb, w_in_ref, b_in_ref):
    sec = functools.partial(_section, hb, w_in_ref, b_in_ref)

    q = _silu(sec(0, HG_WIDTH))
    f = lb + (1.0 - lb) * jax.nn.sigmoid(sec(HG_WIDTH, HG_WIDTH))
    v = sec(2 * HG_WIDTH, HG_WIDTH)
    gate = _silu(sec(3 * HG_WIDTH, HG_WIDTH))
    za = sec(4 * HG_WIDTH, CV_WIDTH)
    zb = sec(4 * HG_WIDTH + CV_WIDTH, CV_WIDTH)
    return q, 1.0 - f, jnp.log(f), v, gate, za * jax.nn.sigmoid(zb)


def _head_norm(o, hgw, gate):
    ms = jnp.mean(o * o, axis=-1, keepdims=True)
    return o * lax.rsqrt(ms + EPS) * hgw * gate


def _group_norm_swish(uc, gmat_ref, gn_g, gn_b):
    width = gmat_ref.shape[1]

    def per_group(a, which):
        ab = a.astype(BF16)
        return jnp.concatenate([_dot(ab[:, c0:c0 + width], gmat_ref[which])
                                for c0 in range(0, ab.shape[1], width)], axis=-1)

    d0 = uc - per_group(uc, 0)
    d = d0 - per_group(d0, 1)
    var = per_group(d * d, 1)
    return _silu(d * lax.rsqrt(var + EPS) * gn_g + gn_b)


def _out_projection(mix_bf16, w_out_ref, b_out_ref):
    return _dot(mix_bf16, w_out_ref[...]) + b_out_ref[...]


def _prompt_kernel(x_ref, mod_ref, lb_ref, w_in_ref, b_in_ref, hgw_ref, cw_ref, cb_ref,
                   gng_ref, gnb_ref, gmat_ref, w_out_ref, b_out_ref, ln1g_ref, ln1b_ref,
                   w_up_ref, b_up_ref, w_down_ref, b_down_ref, ln2g_ref, ln2b_ref,
                   y_ref, s_out_ref, cbuf_out_ref,
                   q_s, k_s, lf_s, v_s, g_s, ubuf, ush, uc_s, mix_s, st_s,
                   x1_s, hb2_s, res2_s, act_s, ff_s, *, tile, tiles_per_seq, n_tiles):
    i = pl.program_id(0)
    mixer_tile = jnp.minimum(i, n_tiles - 1)
    b_mix = mixer_tile // tiles_per_seq
    t = mixer_tile % tiles_per_seq
    b_mlp = jnp.maximum(i - 1, 0) // tiles_per_seq

    def mlp_prologue():
        x1_prev = x1_s[...]
        sh2 = mod_ref[b_mlp, 3:4, :]
        sc2 = mod_ref[b_mlp, 4:5, :]
        hb2_s[...] = (x1_prev * (1.0 + sc2) + sh2).astype(BF16)
        res2_s[...] = ALPHA * x1_prev

    def mlp_epilogue():
        g2 = mod_ref[b_mlp, 5:6, :]
        y_ref[0] = _layer_norm(res2_s[...] + (1.0 + g2) * ff_s[...], ln2g_ref[...], ln2b_ref[...])

    def mlp_up(c0):
        cs = slice(c0, c0 + UP_CHUNK)
        a = jnp.maximum(_dot(hb2_s[...], w_up_ref[:, cs]) + b_up_ref[:, cs], 0.0)
        act_s[:, cs] = (a * a).astype(BF16)

    def mlp_down(c0):
        cs = slice(c0, c0 + DOWN_CHUNK)
        ff_s[:, cs] = _dot(act_s[...], w_down_ref[:, cs]) + b_down_ref[:, cs]

    up_chunks = list(range(0, D_FF, UP_CHUNK))
    down_chunks = list(range(0, D_MODEL, DOWN_CHUNK))

    def next_up():
        if up_chunks:
            mlp_up(up_chunks.pop(0))

    def next_down():
        if down_chunks:
            mlp_down(down_chunks.pop(0))

    @pl.when(i == 0)
    def _():
        x1_s[...] = jnp.zeros_like(x1_s)

    @pl.when(t == 0)
    def _():
        st_s[...] = jnp.zeros_like(st_s)
        ubuf[0:CONV_PAD, :] = jnp.zeros((CONV_PAD, CV_WIDTH), F32)

    mlp_prologue()

    x = x_ref[0]
    sh1 = mod_ref[b_mix, 0:1, :]
    sc1 = mod_ref[b_mix, 1:2, :]
    g1 = mod_ref[b_mix, 2:3, :]
    hb = (x * (1.0 + sc1) + sh1).astype(BF16)
    sec = functools.partial(_section, hb, w_in_ref, b_in_ref)
    lb = _lower_bound(lb_ref)

    ubuf[CONV_PAD:CONV_PAD + tile, :] = (sec(4 * HG_WIDTH, CV_WIDTH)
                                         * jax.nn.sigmoid(sec(4 * HG_WIDTH + CV_WIDTH, CV_WIDTH)))
    first_off = CONV_PAD - CV_BUF
    shifted_rows = ush.shape[1]

    def conv_piece(r0, cs):
        accs = [jnp.broadcast_to(cb_ref[:, cs], (CONV_ROWS, LANES))] + [None] * (CONV_CHAINS - 1)
        for phase in range(SUBLANES):
            taps = [j for j in range(CV_KERNEL) if (first_off + j) % SUBLANES == phase]
            span = (first_off + taps[-1]) // SUBLANES * SUBLANES
            rows = slice(r0, r0 + span + CONV_ROWS)
            win = ubuf[rows, cs] if phase == 0 else ush[phase - 1, rows, :]
            c = phase % CONV_CHAINS
            for j in taps:
                a0 = first_off + j - phase
                term = cw_ref[j:j + 1, cs] * win[a0:a0 + CONV_ROWS, :]
                accs[c] = term if accs[c] is None else accs[c] + term
        uc_s[r0:r0 + CONV_ROWS, cs] = functools.reduce(lambda a, b: a + b, accs)

    def conv_pieces(n):
        cs = slice(n * LANES, (n + 1) * LANES)
        for phase in range(1, SUBLANES):
            ush[phase - 1] = ubuf[phase:phase + shifted_rows, cs]
        for r0 in range(0, tile, CONV_ROWS):
            conv_piece(r0, cs)

    next_up()
    q_s[...] = _silu(sec(0, HG_WIDTH))
    conv_pieces(0)
    next_up()
    f = lb + (1.0 - lb) * jax.nn.sigmoid(sec(HG_WIDTH, HG_WIDTH))
    k_s[...] = 1.0 - f
    lf_s[...] = jnp.log(f)
    conv_pieces(1)
    next_up()
    v_s[...] = sec(2 * HG_WIDTH, HG_WIDTH).astype(BF16)
    conv_pieces(2)
    next_up()
    g_s[...] = _silu(sec(3 * HG_WIDTH, HG_WIDTH))
    conv_pieces(3)

    row = lax.broadcasted_iota(jnp.int32, (CHUNK, HG_WIDTH), 0)
    causal = (lax.broadcasted_iota(jnp.int32, (CHUNK, CHUNK), 0)
              >= lax.broadcasted_iota(jnp.int32, (CHUNK, CHUNK), 1))
    hgw = hgw_ref[...]
    for c in range(tile // CHUNK):
        next_up()
        rows = slice(c * CHUNK, (c + 1) * CHUNK)
        b = lf_s[rows, :]
        shift = 1
        while shift < CHUNK:
            b = b + jnp.where(row >= shift, pltpu.roll(b, shift, axis=0), 0.0)
            shift *= 2
        b_last = b[CHUNK - 1:CHUNK, :]
        kc = k_s[rows, :]
        qe = (q_s[rows, :] * jnp.exp(b)).astype(BF16)
        ke = (kc * jnp.exp(-b)).astype(BF16)
        kd = (kc * jnp.exp(b_last - b)).astype(BF16)
        decay = jnp.exp(b_last)
        vb = v_s[rows, :]
        for h in range(HG_HEADS):
            hs = slice(h * HG_DK, (h + 1) * HG_DK)
            att = jnp.where(causal, _dot_nt(qe[:, hs], ke[:, hs]), 0.0).astype(BF16)
            st = st_s[h]
            o = _dot(att, vb[:, hs]) + _dot_nt(qe[:, hs], st.astype(BF16))
            st_s[h] = st * decay[:, hs] + _dot_tn(vb[:, hs], kd[:, hs])
            mix_s[rows, hs] = _head_norm(o, hgw, g_s[rows, hs]).astype(BF16)
    while up_chunks:
        next_up()

    next_down()
    mix_s[:, HG_WIDTH:] = _group_norm_swish(
        uc_s[...], gmat_ref, gng_ref[...], gnb_ref[...]).astype(BF16)
    ubuf[0:CONV_PAD, :] = ubuf[tile:tile + CONV_PAD, :]
    next_down()
    mix = _out_projection(mix_s[...], w_out_ref, b_out_ref)
    x1_s[...] = _layer_norm(ALPHA * x + (1.0 + g1) * mix, ln1g_ref[...], ln1b_ref[...])
    while down_chunks:
        next_down()
    mlp_epilogue()

    @pl.when((t == tiles_per_seq - 1) & (i < n_tiles))
    def _():
        for h in range(HG_HEADS):
            s_out_ref[0, h] = st_s[h].T
        cbuf_out_ref[0] = ubuf[CONV_PAD - CV_BUF:CONV_PAD, :]


def _const_spec(shape):
    zeros = (0,) * len(shape)
    return pl.BlockSpec(shape, lambda *_: zeros, pipeline_mode=pl.Buffered(1))


def _prompt_layer(x, mod, lb_logits, w_in, b_in, hgw, conv_w, conv_b, gn_g, gn_b, gmat,
                  w_out, b_out, ln1_g, ln1_b, w_up, b_up, w_down, b_down, ln2_g, ln2_b, *, tile):
    batch, seq, _ = x.shape
    tiles_per_seq = seq // tile
    n_tiles = batch * tiles_per_seq
    kernel = functools.partial(_prompt_kernel, tile=tile, tiles_per_seq=tiles_per_seq,
                               n_tiles=n_tiles)

    def mixer_block(i):
        m = jnp.minimum(i, n_tiles - 1)
        return m // tiles_per_seq, m % tiles_per_seq

    def mlp_block(i):
        m = jnp.maximum(i - 1, 0)
        return m // tiles_per_seq, m % tiles_per_seq

    consts = (mod, lb_logits, w_in, b_in, hgw, conv_w, conv_b, gn_g, gn_b, gmat, w_out, b_out,
              ln1_g, ln1_b, w_up, b_up, w_down, b_down, ln2_g, ln2_b)
    return pl.pallas_call(
        kernel,
        grid=(n_tiles + 1,),
        in_specs=[pl.BlockSpec((1, tile, D_MODEL), lambda i: (*mixer_block(i), 0))]
        + [_const_spec(a.shape) for a in consts],
        out_specs=[
            pl.BlockSpec((1, tile, D_MODEL), lambda i: (*mlp_block(i), 0)),
            pl.BlockSpec((1, HG_HEADS, HG_DK, HG_DK), lambda i: (mixer_block(i)[0], 0, 0, 0)),
            pl.BlockSpec((1, CV_BUF, CV_WIDTH), lambda i: (mixer_block(i)[0], 0, 0)),
        ],
        out_shape=[
            jax.ShapeDtypeStruct((batch, seq, D_MODEL), F32),
            jax.ShapeDtypeStruct((batch, HG_HEADS, HG_DK, HG_DK), F32),
            jax.ShapeDtypeStruct((batch, CV_BUF, CV_WIDTH), F32),
        ],
        scratch_shapes=[
            pltpu.VMEM((tile, HG_WIDTH), F32),
            pltpu.VMEM((tile, HG_WIDTH), F32),
            pltpu.VMEM((tile, HG_WIDTH), F32),
            pltpu.VMEM((tile, HG_WIDTH), BF16),
            pltpu.VMEM((tile, HG_WIDTH), F32),
            pltpu.VMEM((CONV_PAD + tile, CV_WIDTH), F32),
            pltpu.VMEM((SUBLANES - 1, CONV_PAD + tile - SUBLANES, LANES), F32),
            pltpu.VMEM((tile, CV_WIDTH), F32),
            pltpu.VMEM((tile, D_MODEL), BF16),
            pltpu.VMEM((HG_HEADS, HG_DK, HG_DK), F32),
            pltpu.VMEM((tile, D_MODEL), F32),
            pltpu.VMEM((tile, D_MODEL), BF16),
            pltpu.VMEM((tile, D_MODEL), F32),
            pltpu.VMEM((tile, D_FF), BF16),
            pltpu.VMEM((tile, D_MODEL), F32),
        ],
        compiler_params=pltpu.CompilerParams(
            dimension_semantics=("arbitrary",), vmem_limit_bytes=BIG_VMEM_LIMIT_BYTES),
        name="prompt_layer",
    )(x, *consts)


def _mixer_sample_kernel(x_ref, mod_ref, s_in_ref, cbuf_in_ref, lb_ref, w_in_ref, b_in_ref, hgw_ref,
                         cw_ref, cb_ref, gng_ref, gnb_ref, gmat_ref, w_out_ref, b_out_ref,
                         lng_ref, lnb_ref,
                         x1_ref, s_out_ref, cbuf_out_ref,
                         qe_s, ke_s, kd_s, v_s, o_s, *, seqs):
    rows_n = seqs * DEC_SEQ
    x = jnp.stack([x_ref[:, t, :] for t in range(DEC_SEQ)], axis=0)
    sh1 = mod_ref[:, 0:D_MODEL]
    sc1 = mod_ref[:, D_MODEL:2 * D_MODEL]
    g1 = mod_ref[:, 2 * D_MODEL:3 * D_MODEL]
    hb = (x * (1.0 + sc1) + sh1).astype(BF16).reshape(rows_n, D_MODEL)
    q, k, lf, v, gate, u = _in_projection(hb, _lower_bound(lb_ref), w_in_ref, b_in_ref)

    slab = lambda a, t: a[t * seqs:(t + 1) * seqs, :]
    b = [slab(lf, 0)]
    for t in range(1, DEC_SEQ):
        b.append(b[-1] + slab(lf, t))
    b_last = b[-1]
    decay = jnp.exp(b_last)
    pad = jnp.zeros((LANES - seqs, HG_DK), F32)
    decay_t = [jnp.concatenate([decay[:, h * HG_DK:(h + 1) * HG_DK], pad], axis=0).T
               for h in range(HG_HEADS)]

    def put(ref, t, val):
        for h in range(HG_HEADS):
            ref[h, t * seqs:(t + 1) * seqs, :] = val[:, h * HG_DK:(h + 1) * HG_DK]

    for t in range(DEC_SEQ):
        put(qe_s, t, slab(q, t) * jnp.exp(b[t]))
        put(ke_s, t, slab(k, t) * jnp.exp(-b[t]))
        put(kd_s, t, slab(k, t) * jnp.exp(b_last - b[t]))
        put(v_s, t, slab(v, t))

    stride = seqs // SEQ_GROUP
    n_rows = SEQ_GROUP * DEC_SEQ
    member = lax.broadcasted_iota(jnp.int32, (n_rows, HG_DK), 0) % SEQ_GROUP
    lanes = lambda g: slice(g * HG_DK, (g + 1) * HG_DK)
    per_member = lambda a: jnp.concatenate(
        [jnp.where(member == g, a, 0.0) for g in range(SEQ_GROUP)], axis=1)
    ri = lax.broadcasted_iota(jnp.int32, (n_rows, n_rows), 0)
    ci = lax.broadcasted_iota(jnp.int32, (n_rows, n_rows), 1)
    causal = (ri // SEQ_GROUP >= ci // SEQ_GROUP) & (ri % SEQ_GROUP == ci % SEQ_GROUP)
    for s in range(stride):
        rows = pl.ds(s, n_rows, stride=stride)
        for h in range(HG_HEADS):
            qe_g = qe_s[h, rows, :].astype(BF16)
            ke_g = ke_s[h, rows, :].astype(BF16)
            kd_g = kd_s[h, rows, :].astype(BF16)
            v_g = v_s[h, rows, :]
            s_g = jnp.concatenate([s_in_ref[s + g * stride, h] for g in range(SEQ_GROUP)], axis=1)
            att = jnp.where(causal, _dot_nt(qe_g, ke_g), 0.0).astype(BF16)
            inter_all = _dot(qe_g, s_g.astype(BF16))
            inter = inter_all[:, lanes(0)]
            for g in range(1, SEQ_GROUP):
                inter = jnp.where(member == g, inter_all[:, lanes(g)], inter)
            o_s[h, rows, :] = _dot(att, v_g.astype(BF16)) + inter
            col = lambda g: decay_t[h][:, s + g * stride:s + g * stride + 1]
            decay_g = jnp.concatenate([jnp.broadcast_to(col(g), (HG_DK, HG_DK))
                                       for g in range(SEQ_GROUP)], axis=1)
            new = decay_g * s_g + _dot_tn(kd_g, per_member(v_g).astype(BF16))
            for g in range(SEQ_GROUP):
                s_out_ref[s + g * stride, h] = new[:, lanes(g)]
    hgw = hgw_ref[...]
    o_a = jnp.concatenate(
        [_head_norm(o_s[h], hgw, gate[:, h * HG_DK:(h + 1) * HG_DK]) for h in range(HG_HEADS)],
        axis=-1)

    full = lambda i: cbuf_in_ref[i] if i < CV_BUF else slab(u, i - CV_BUF)
    acc = [jnp.broadcast_to(cb_ref[...], (seqs, CV_WIDTH)) for _ in range(DEC_SEQ)]
    for i in range(CV_BUF + DEC_SEQ):
        f_i = full(i)
        for t in range(DEC_SEQ):
            if 0 <= i - t < CV_KERNEL:
                acc[t] = acc[t] + cw_ref[i - t:i - t + 1, :] * f_i
    for i in range(CV_BUF):
        cbuf_out_ref[i] = full(i + DEC_SEQ)
    o_b = _group_norm_swish(jnp.concatenate(acc, axis=0), gmat_ref, gng_ref[...], gnb_ref[...])

    mix = _out_projection(jnp.concatenate([o_a, o_b], axis=-1).astype(BF16), w_out_ref, b_out_ref)
    y = ALPHA * x + (1.0 + g1) * mix.reshape(DEC_SEQ, seqs, D_MODEL)
    x1_ref[...] = _layer_norm(y, lng_ref[...], lnb_ref[...])


def _mixer_sample(x, mod, s_in, cbuf_in, lb_logits, w_in, b_in, hgw, conv_w, conv_b, gn_g, gn_b,
                  gmat, w_out, b_out, ln_g, ln_b, *, seqs):
    n_seq = s_in.shape[0]
    rows = seqs * DEC_SEQ
    kernel = functools.partial(_mixer_sample_kernel, seqs=seqs)
    return pl.pallas_call(
        kernel,
        grid=(n_seq // seqs,),
        in_specs=[
            pl.BlockSpec((seqs, DEC_SEQ, D_MODEL), lambda i: (i, 0, 0)),
            pl.BlockSpec((seqs, 3 * D_MODEL), lambda i: (i, 0)),
            pl.BlockSpec((seqs, HG_HEADS, HG_DK, HG_DK), lambda i: (i, 0, 0, 0)),
            pl.BlockSpec((CV_BUF, seqs, CV_WIDTH), lambda i: (0, i, 0)),
            _const_spec(lb_logits.shape),
            _const_spec(w_in.shape),
            _const_spec(b_in.shape),
            _const_spec(hgw.shape),
            _const_spec(conv_w.shape),
            _const_spec(conv_b.shape),
            _const_spec(gn_g.shape),
            _const_spec(gn_b.shape),
            _const_spec(gmat.shape),
            _const_spec(w_out.shape),
            _const_spec(b_out.shape),
            _const_spec(ln_g.shape),
            _const_spec(ln_b.shape),
        ],
        out_specs=[
            pl.BlockSpec((DEC_SEQ, seqs, D_MODEL), lambda i: (0, i, 0)),
            pl.BlockSpec((seqs, HG_HEADS, HG_DK, HG_DK), lambda i: (i, 0, 0, 0)),
            pl.BlockSpec((CV_BUF, seqs, CV_WIDTH), lambda i: (0, i, 0)),
        ],
        out_shape=[
            jax.ShapeDtypeStruct((DEC_SEQ, n_seq, D_MODEL), F32),
            jax.ShapeDtypeStruct(s_in.shape, F32),
            jax.ShapeDtypeStruct(cbuf_in.shape, F32),
        ],
        scratch_shapes=[pltpu.VMEM((HG_HEADS, rows, HG_DK), F32)] * 5,
        compiler_params=pltpu.CompilerParams(
            dimension_semantics=("arbitrary",), vmem_limit_bytes=BIG_VMEM_LIMIT_BYTES),
        name="mixer_sample",
    )(x, mod, s_in, cbuf_in, lb_logits, w_in, b_in, hgw, conv_w, conv_b, gn_g, gn_b, gmat,
      w_out, b_out, ln_g, ln_b)


def _mlp_kernel(x_ref, sh_ref, sc_ref, g_ref, w_up_ref, b_up_ref, w_down_ref, b_down_ref,
                lng_ref, lnb_ref, o_ref):
    x = x_ref[...]
    hb = (x * (1.0 + sc_ref[...]) + sh_ref[...]).astype(BF16).reshape(-1, D_MODEL)
    ff = jnp.broadcast_to(b_down_ref[...], hb.shape)
    for c0 in range(0, D_FF, FF_CHUNK):
        cs = slice(c0, c0 + FF_CHUNK)
        a = jnp.maximum(_dot(hb, w_up_ref[:, cs]) + b_up_ref[:, cs], 0.0)
        ff = ff + _dot((a * a).astype(BF16), w_down_ref[cs, :])
    y = ALPHA * x + (1.0 + g_ref[...]) * ff.reshape(x.shape)
    out = _layer_norm(y, lng_ref[...], lnb_ref[...])
    for t in range(DEC_SEQ):
        o_ref[:, t, :] = out[t]


def _mlp_sample(x, mod, w_up, b_up, w_down, b_down, ln_g, ln_b):
    x_spec = pl.BlockSpec(x.shape, lambda i: (0, 0, 0))
    out_shape = (x.shape[1], x.shape[0], x.shape[2])
    mod_specs = [pl.BlockSpec((x.shape[1], D_MODEL), functools.partial(lambda i, r: (0, r), r=r))
                 for r in (3, 4, 5)]
    return pl.pallas_call(
        _mlp_kernel,
        grid=(1,),
        in_specs=[x_spec] + mod_specs + [
            _const_spec(w_up.shape),
            _const_spec(b_up.shape),
            _const_spec(w_down.shape),
            _const_spec(b_down.shape),
            _const_spec(ln_g.shape),
            _const_spec(ln_b.shape),
        ],
        out_specs=pl.BlockSpec(out_shape, lambda i: (0, 0, 0)),
        out_shape=jax.ShapeDtypeStruct(out_shape, F32),
        compiler_params=pltpu.CompilerParams(
            dimension_semantics=("arbitrary",), vmem_limit_bytes=VMEM_LIMIT_BYTES),
        name="mlp_sample",
    )(x, mod, mod, mod, w_up, b_up, w_down, b_down, ln_g, ln_b)


PROMPT_TILE = 512
SAMPLE_SEQS = 32


def kernel(x_prompt, x_sample, c_prompt, c_sample, state_hgrn, state_conv, lb_logits, w_in, b_in,
           hg_norm_w, conv_w, conv_b, gn_g, gn_b, w_out, b_out, ln1_g, ln1_b, w_up, b_up, w_down,
           b_down, ln2_g, ln2_b, w_ada, b_ada):
    assert w_in.shape[0] == DEPTH
    batch, seq, _ = x_prompt.shape
    dec_batch, dec_seq, _ = x_sample.shape
    assert dec_seq == DEC_SEQ

    group = jnp.arange(MXU_WIDTH, dtype=jnp.int32) // (CV_WIDTH // CV_GROUPS)
    first = jnp.arange(MXU_WIDTH, dtype=jnp.int32)[:, None] == group[None, :] * (CV_WIDTH // CV_GROUPS)
    gmat = jnp.stack([jnp.where(first, 1.0, 0.0),
                      jnp.where(group[:, None] == group[None, :], CV_GROUPS / CV_WIDTH, 0.0)]
                     ).astype(BF16)

    xp = x_prompt
    xs = x_sample
    hp, cp, hs, cs = [], [], [], []
    for l in range(DEPTH):
        row = lambda a: a[l][None, :]
        w_in_b, w_out_b = _to_bf16(w_in[l]), _to_bf16(w_out[l])
        w_up_b, w_down_b = _to_bf16(w_up[l]), _to_bf16(w_down[l])

        mod_p, mod_s = _adaln(c_prompt, c_sample, w_ada[l], row(b_ada))
        mod_p = mod_p.reshape(batch, 6, D_MODEL)

        mixer_params = (lb_logits, w_in_b, row(b_in), row(hg_norm_w), conv_w[l], row(conv_b),
                        row(gn_g), row(gn_b), gmat, w_out_b, row(b_out), row(ln1_g), row(ln1_b))
        mlp_params = (w_up_b, row(b_up), w_down_b, row(b_down), row(ln2_g), row(ln2_b))
        xp, sp, bp = _prompt_layer(xp, mod_p, *mixer_params, *mlp_params, tile=PROMPT_TILE)
        xs, ss, bs = _mixer_sample(xs, mod_s, state_hgrn[l], jnp.transpose(state_conv[l], (1, 0, 2)),
                                   *mixer_params, seqs=SAMPLE_SEQS)
        xs = _mlp_sample(xs, mod_s, *mlp_params)

        hp.append(sp); cp.append(bp); hs.append(ss); cs.append(jnp.transpose(bs, (1, 0, 2)))

    return (xp, xs, jnp.stack(hp), jnp.stack(cp), jnp.stack(hs),
            jnp.stack(cs))
```

```python
import functools

import jax
import jax.numpy as jnp
from jax import lax
from jax.experimental import pallas as pl
from jax.experimental.pallas import tpu as pltpu

F32 = jnp.float32
BF16 = jnp.bfloat16

D_MODEL = 1024
HG_WIDTH = 512
CV_WIDTH = 512
HG_HEADS = 4
HG_DK = 128
CV_GROUPS = 8
CV_KERNEL = 31
CV_BUF = CV_KERNEL - 1
D_FF = 4 * D_MODEL
N_IN = 4 * HG_WIDTH + 2 * CV_WIDTH
DEPTH = 1
ALPHA = (2.0 * DEPTH) ** 0.25
EPS = 1e-5

CHUNK = 64
SUBLANES = 8
LANES = 128
MXU_WIDTH = 256
CONV_PAD = 32
CONV_ROWS = 128
ROW_CHUNK = 64
CONV_CHAINS = 2
UP_CHUNK = 512
DOWN_CHUNK = 256
FF_CHUNK = 1024
DEC_SEQ = 4
SEQ_GROUP = 16

VMEM_LIMIT_BYTES = 52 * 1024 * 1024
BIG_VMEM_LIMIT_BYTES = 60 * 1024 * 1024
CAST_BLOCK_BYTES = 8 * 1024 * 1024
ADALN_COLS = 2048

_NT = (((1,), (1,)), ((), ()))
_TN = (((0,), (0,)), ((), ()))


def _dot(a, b):
    return jnp.dot(a, b, preferred_element_type=F32)


def _dot_nt(a, b):
    return lax.dot_general(a, b, _NT, preferred_element_type=F32)


def _dot_tn(a, b):
    return lax.dot_general(a, b, _TN, preferred_element_type=F32)


def _silu(x):
    return x * jax.nn.sigmoid(x)


def _layer_norm(y, g, b):
    mu = jnp.mean(y, axis=-1, keepdims=True)
    yc = y - mu
    var = jnp.mean(yc * yc, axis=-1, keepdims=True)
    return yc * lax.rsqrt(var + EPS) * g + b


def _adaln_kernel(cp_ref, cs_ref, w_ref, b_ref, op_ref, os_ref):
    a = _silu(jnp.concatenate([cp_ref[...], cs_ref[...]], axis=0))
    a_hi = a.astype(BF16)
    a_lo = (a - a_hi.astype(F32)).astype(BF16)
    w = w_ref[...].astype(BF16)
    mod = _dot(a_hi, w) + _dot(a_lo, w) + b_ref[...]
    n_prompt = cp_ref.shape[0]
    op_ref[...] = mod[:n_prompt]
    os_ref[...] = mod[n_prompt:]


def _adaln(c_prompt, c_sample, w_ada, b_ada):
    n = w_ada.shape[1]
    tn = ADALN_COLS
    rows = lambda c: pl.BlockSpec((c.shape[0], D_MODEL), lambda j: (0, 0))
    cols = lambda c: pl.BlockSpec((c.shape[0], tn), lambda j: (0, j))
    return pl.pallas_call(
        _adaln_kernel,
        grid=(n // tn,),
        in_specs=[
            rows(c_prompt),
            rows(c_sample),
            pl.BlockSpec((D_MODEL, tn), lambda j: (0, j)),
            pl.BlockSpec((1, tn), lambda j: (0, j)),
        ],
        out_specs=[cols(c_prompt), cols(c_sample)],
        out_shape=[jax.ShapeDtypeStruct((c.shape[0], n), F32) for c in (c_prompt, c_sample)],
        compiler_params=pltpu.CompilerParams(
            dimension_semantics=("arbitrary",), vmem_limit_bytes=VMEM_LIMIT_BYTES),
        name="adaln",
    )(c_prompt, c_sample, w_ada, b_ada)


def _cast_kernel(w_ref, o_ref):
    o_ref[...] = w_ref[...].astype(o_ref.dtype)


def _to_bf16(w):
    rows, cols = w.shape
    block_rows = rows
    while block_rows * cols * w.dtype.itemsize > CAST_BLOCK_BYTES and block_rows % 16 == 0:
        block_rows //= 2
    spec = pl.BlockSpec((block_rows, cols), lambda i: (i, 0))
    return pl.pallas_call(
        _cast_kernel,
        grid=(rows // block_rows,),
        in_specs=[spec],
        out_specs=spec,
        out_shape=jax.ShapeDtypeStruct(w.shape, BF16),
        compiler_params=pltpu.CompilerParams(
            dimension_semantics=("arbitrary",), vmem_limit_bytes=VMEM_LIMIT_BYTES),
        name="to_bf16",
    )(w)


def _lower_bound(lb_ref):
    lg = lb_ref[...]
    e = jnp.exp(lg - jnp.max(lg, axis=0, keepdims=True))
    p = e / jnp.sum(e, axis=0, keepdims=True)
    return jnp.sum(p[0:DEPTH, :], axis=0, keepdims=True)


def _section(hb, w_in_ref, b_in_ref, i, width):
    return _dot(hb, w_in_ref[:, i:i + width]) + b_in_ref[:, i:i + width]


def _in_projection(hb, lb, w_in_ref, b_in_ref):
    sec = functools.partial(_section, hb, w_in_ref, b_in_ref)

    q = _silu(sec(0, HG_WIDTH))
    f = lb + (1.0 - lb) * jax.nn.sigmoid(sec(HG_WIDTH, HG_WIDTH))
    v = sec(2 * HG_WIDTH, HG_WIDTH)
    gate = _silu(sec(3 * HG_WIDTH, HG_WIDTH))
    za = sec(4 * HG_WIDTH, CV_WIDTH)
    zb = sec(4 * HG_WIDTH + CV_WIDTH, CV_WIDTH)
    return q, 1.0 - f, jnp.log(f), v, gate, za * jax.nn.sigmoid(zb)


def _head_norm(o, hgw, gate):
    ms = jnp.mean(o * o, axis=-1, keepdims=True)
    return o * lax.rsqrt(ms + EPS) * hgw * gate


def _group_norm_swish(uc, gmat_ref, gn_g, gn_b):
    width = gmat_ref.shape[1]

    def per_group(a, which):
        ab = a.astype(BF16)
        return jnp.concatenate([_dot(ab[:, c0:c0 + width], gmat_ref[which])
                                for c0 in range(0, ab.shape[1], width)], axis=-1)

    d0 = uc - per_group(uc, 0)
    d = d0 - per_group(d0, 1)
    var = per_group(d * d, 1)
    return _silu(d * lax.rsqrt(var + EPS) * gn_g + gn_b)


def _out_projection(mix_bf16, w_out_ref, b_out_ref):
    return _dot(mix_bf16, w_out_ref[...]) + b_out_ref[...]


def _prompt_kernel(x_ref, mod_ref, lb_ref, w_in_ref, b_in_ref, hgw_ref, cw_ref, cb_ref,
                   gng_ref, gnb_ref, gmat_ref, w_out_ref, b_out_ref, ln1g_ref, ln1b_ref,
                   w_up_ref, b_up_ref, w_down_ref, b_down_ref, ln2g_ref, ln2b_ref,
                   y_ref, s_out_ref, cbuf_out_ref,
                   q_s, k_s, lf_s, v_s, g_s, ubuf, ush, uc_s, mix_s, st_s,
                   x1_s, hb2_s, res2_s, act_s, ff_s, *, tile, tiles_per_seq, n_tiles):
    i = pl.program_id(0)
    mixer_tile = jnp.minimum(i, n_tiles - 1)
    b_mix = mixer_tile // tiles_per_seq
    t = mixer_tile % tiles_per_seq
    b_mlp = jnp.maximum(i - 1, 0) // tiles_per_seq

    row_chunks = [slice(r0, r0 + ROW_CHUNK) for r0 in range(0, tile, ROW_CHUNK)]

    def mlp_prologue():
        sh2 = mod_ref[b_mlp, 3:4, :]
        sc2 = mod_ref[b_mlp, 4:5, :]
        for rows in row_chunks:
            x1_prev = x1_s[rows, :]
            hb2_s[rows, :] = (x1_prev * (1.0 + sc2) + sh2).astype(BF16)
            res2_s[rows, :] = ALPHA * x1_prev

    def mlp_epilogue():
        g2 = mod_ref[b_mlp, 5:6, :]
        for rows in row_chunks:
            y_ref[0, rows, :] = _layer_norm(res2_s[rows, :] + (1.0 + g2) * ff_s[rows, :],
                                            ln2g_ref[...], ln2b_ref[...])

    def mlp_up(c0):
        cs = slice(c0, c0 + UP_CHUNK)
        a = jnp.maximum(_dot(hb2_s[...], w_up_ref[:, cs]) + b_up_ref[:, cs], 0.0)
        act_s[:, cs] = (a * a).astype(BF16)

    def mlp_down(c0):
        cs = slice(c0, c0 + DOWN_CHUNK)
        ff_s[:, cs] = _dot(act_s[...], w_down_ref[:, cs]) + b_down_ref[:, cs]

    up_chunks = list(range(0, D_FF, UP_CHUNK))
    down_chunks = list(range(0, D_MODEL, DOWN_CHUNK))

    def next_up():
        if up_chunks:
            mlp_up(up_chunks.pop(0))

    def next_down():
        if down_chunks:
            mlp_down(down_chunks.pop(0))

    @pl.when(i == 0)
    def _():
        x1_s[...] = jnp.zeros_like(x1_s)

    @pl.when(t == 0)
    def _():
        st_s[...] = jnp.zeros_like(st_s)
        ubuf[0:CONV_PAD, :] = jnp.zeros((CONV_PAD, CV_WIDTH), F32)

    mlp_prologue()

    x = x_ref[0]
    sh1 = mod_ref[b_mix, 0:1, :]
    sc1 = mod_ref[b_mix, 1:2, :]
    g1 = mod_ref[b_mix, 2:3, :]
    hb = (x * (1.0 + sc1) + sh1).astype(BF16)
    sec = functools.partial(_section, hb, w_in_ref, b_in_ref)
    lb = _lower_bound(lb_ref)

    ubuf[CONV_PAD:CONV_PAD + tile, :] = (sec(4 * HG_WIDTH, CV_WIDTH)
                                         * jax.nn.sigmoid(sec(4 * HG_WIDTH + CV_WIDTH, CV_WIDTH)))
    first_off = CONV_PAD - CV_BUF
    shifted_rows = ush.shape[1]

    def conv_piece(r0, cs):
        accs = [jnp.broadcast_to(cb_ref[:, cs], (CONV_ROWS, LANES))] + [None] * (CONV_CHAINS - 1)
        for phase in range(SUBLANES):
            taps = [j for j in range(CV_KERNEL) if (first_off + j) % SUBLANES == phase]
            span = (first_off + taps[-1]) // SUBLANES * SUBLANES
            rows = slice(r0, r0 + span + CONV_ROWS)
            win = ubuf[rows, cs] if phase == 0 else ush[phase - 1, rows, :]
            c = phase % CONV_CHAINS
            for j in taps:
                a0 = first_off + j - phase
                term = cw_ref[j:j + 1, cs] * win[a0:a0 + CONV_ROWS, :]
                accs[c] = term if accs[c] is None else accs[c] + term
        uc_s[r0:r0 + CONV_ROWS, cs] = functools.reduce(lambda a, b: a + b, accs)

    def conv_pieces(n):
        cs = slice(n * LANES, (n + 1) * LANES)
        for phase in range(1, SUBLANES):
            ush[phase - 1] = ubuf[phase:phase + shifted_rows, cs]
        for r0 in range(0, tile, CONV_ROWS):
            conv_piece(r0, cs)

    next_up()
    q_s[...] = _silu(sec(0, HG_WIDTH))
    conv_pieces(0)
    next_up()
    f = lb + (1.0 - lb) * jax.nn.sigmoid(sec(HG_WIDTH, HG_WIDTH))
    k_s[...] = 1.0 - f
    lf_s[...] = jnp.log(f)
    conv_pieces(1)
    next_up()
    v_s[...] = sec(2 * HG_WIDTH, HG_WIDTH).astype(BF16)
    conv_pieces(2)
    next_up()
    g_s[...] = _silu(sec(3 * HG_WIDTH, HG_WIDTH))
    conv_pieces(3)

    row = lax.broadcasted_iota(jnp.int32, (CHUNK, HG_WIDTH), 0)
    causal = (lax.broadcasted_iota(jnp.int32, (CHUNK, CHUNK), 0)
              >= lax.broadcasted_iota(jnp.int32, (CHUNK, CHUNK), 1))
    hgw = hgw_ref[...]
    for c in range(tile // CHUNK):
        next_up()
        rows = slice(c * CHUNK, (c + 1) * CHUNK)
        b = lf_s[rows, :]
        shift = 1
        while shift < CHUNK:
            b = b + jnp.where(row >= shift, pltpu.roll(b, shift, axis=0), 0.0)
            shift *= 2
        b_last = b[CHUNK - 1:CHUNK, :]
        kc = k_s[rows, :]
        qe = (q_s[rows, :] * jnp.exp(b)).astype(BF16)
        ke = (kc * jnp.exp(-b)).astype(BF16)
        kd = (kc * jnp.exp(b_last - b)).astype(BF16)
        decay = jnp.exp(b_last)
        vb = v_s[rows, :]
        for h in range(HG_HEADS):
            hs = slice(h * HG_DK, (h + 1) * HG_DK)
            att = jnp.where(causal, _dot_nt(qe[:, hs], ke[:, hs]), 0.0).astype(BF16)
            st = st_s[h]
            o = _dot(att, vb[:, hs]) + _dot_nt(qe[:, hs], st.astype(BF16))
            st_s[h] = st * decay[:, hs] + _dot_tn(vb[:, hs], kd[:, hs])
            mix_s[rows, hs] = _head_norm(o, hgw, g_s[rows, hs]).astype(BF16)
    while up_chunks:
        next_up()

    next_down()
    mix_s[:, HG_WIDTH:] = _group_norm_swish(
        uc_s[...], gmat_ref, gng_ref[...], gnb_ref[...]).astype(BF16)
    ubuf[0:CONV_PAD, :] = ubuf[tile:tile + CONV_PAD, :]
    next_down()
    mix = _out_projection(mix_s[...], w_out_ref, b_out_ref)
    for rows in row_chunks:
        x1_s[rows, :] = _layer_norm(ALPHA * x_ref[0, rows, :] + (1.0 + g1) * mix[rows, :],
                                    ln1g_ref[...], ln1b_ref[...])
    while down_chunks:
        next_down()
    mlp_epilogue()

    @pl.when((t == tiles_per_seq - 1) & (i < n_tiles))
    def _():
        for h in range(HG_HEADS):
            s_out_ref[0, h] = st_s[h].T
        cbuf_out_ref[0] = ubuf[CONV_PAD - CV_BUF:CONV_PAD, :]


def _const_spec(shape):
    zeros = (0,) * len(shape)
    return pl.BlockSpec(shape, lambda *_: zeros, pipeline_mode=pl.Buffered(1))


def _prompt_layer(x, mod, lb_logits, w_in, b_in, hgw, conv_w, conv_b, gn_g, gn_b, gmat,
                  w_out, b_out, ln1_g, ln1_b, w_up, b_up, w_down, b_down, ln2_g, ln2_b, *, tile):
    batch, seq, _ = x.shape
    tiles_per_seq = seq // tile
    n_tiles = batch * tiles_per_seq
    kernel = functools.partial(_prompt_kernel, tile=tile, tiles_per_seq=tiles_per_seq,
                               n_tiles=n_tiles)

    def mixer_block(i):
        m = jnp.minimum(i, n_tiles - 1)
        return m // tiles_per_seq, m % tiles_per_seq

    def mlp_block(i):
        m = jnp.maximum(i - 1, 0)
        return m // tiles_per_seq, m % tiles_per_seq

    consts = (mod, lb_logits, w_in, b_in, hgw, conv_w, conv_b, gn_g, gn_b, gmat, w_out, b_out,
              ln1_g, ln1_b, w_up, b_up, w_down, b_down, ln2_g, ln2_b)
    return pl.pallas_call(
        kernel,
        grid=(n_tiles + 1,),
        in_specs=[pl.BlockSpec((1, tile, D_MODEL), lambda i: (*mixer_block(i), 0))]
        + [_const_spec(a.shape) for a in consts],
        out_specs=[
            pl.BlockSpec((1, tile, D_MODEL), lambda i: (*mlp_block(i), 0)),
            pl.BlockSpec((1, HG_HEADS, HG_DK, HG_DK), lambda i: (mixer_block(i)[0], 0, 0, 0)),
            pl.BlockSpec((1, CV_BUF, CV_WIDTH), lambda i: (mixer_block(i)[0], 0, 0)),
        ],
        out_shape=[
            jax.ShapeDtypeStruct((batch, seq, D_MODEL), F32),
            jax.ShapeDtypeStruct((batch, HG_HEADS, HG_DK, HG_DK), F32),
            jax.ShapeDtypeStruct((batch, CV_BUF, CV_WIDTH), F32),
        ],
        scratch_shapes=[
            pltpu.VMEM((tile, HG_WIDTH), F32),
            pltpu.VMEM((tile, HG_WIDTH), F32),
            pltpu.VMEM((tile, HG_WIDTH), F32),
            pltpu.VMEM((tile, HG_WIDTH), BF16),
            pltpu.VMEM((tile, HG_WIDTH), F32),
            pltpu.VMEM((CONV_PAD + tile, CV_WIDTH), F32),
            pltpu.VMEM((SUBLANES - 1, CONV_PAD + tile - SUBLANES, LANES), F32),
            pltpu.VMEM((tile, CV_WIDTH), F32),
            pltpu.VMEM((tile, D_MODEL), BF16),
            pltpu.VMEM((HG_HEADS, HG_DK, HG_DK), F32),
            pltpu.VMEM((tile, D_MODEL), F32),
            pltpu.VMEM((tile, D_MODEL), BF16),
            pltpu.VMEM((tile, D_MODEL), F32),
            pltpu.VMEM((tile, D_FF), BF16),
            pltpu.VMEM((tile, D_MODEL), F32),
        ],
        compiler_params=pltpu.CompilerParams(
            dimension_semantics=("arbitrary",), vmem_limit_bytes=BIG_VMEM_LIMIT_BYTES),
        name="prompt_layer",
    )(x, *consts)


def _mixer_sample_kernel(x_ref, mod_ref, s_in_ref, cbuf_in_ref, lb_ref, w_in_ref, b_in_ref, hgw_ref,
                         cw_ref, cb_ref, gng_ref, gnb_ref, gmat_ref, w_out_ref, b_out_ref,
                         lng_ref, lnb_ref,
                         x1_ref, s_out_ref, cbuf_out_ref,
                         qe_s, ke_s, kd_s, v_s, o_s, *, seqs):
    rows_n = seqs * DEC_SEQ
    x = jnp.stack([x_ref[:, t, :] for t in range(DEC_SEQ)], axis=0)
    sh1 = mod_ref[:, 0:D_MODEL]
    sc1 = mod_ref[:, D_MODEL:2 * D_MODEL]
    g1 = mod_ref[:, 2 * D_MODEL:3 * D_MODEL]
    hb = (x * (1.0 + sc1) + sh1).astype(BF16).reshape(rows_n, D_MODEL)
    q, k, lf, v, gate, u = _in_projection(hb, _lower_bound(lb_ref), w_in_ref, b_in_ref)

    slab = lambda a, t: a[t * seqs:(t + 1) * seqs, :]
    b = [slab(lf, 0)]
    for t in range(1, DEC_SEQ):
        b.append(b[-1] + slab(lf, t))
    b_last = b[-1]
    decay = jnp.exp(b_last)
    pad = jnp.zeros((LANES - seqs, HG_DK), F32)
    decay_t = [jnp.concatenate([decay[:, h * HG_DK:(h + 1) * HG_DK], pad], axis=0).T
               for h in range(HG_HEADS)]

    def put(ref, t, val):
        for h in range(HG_HEADS):
            ref[h, t * seqs:(t + 1) * seqs, :] = val[:, h * HG_DK:(h + 1) * HG_DK]

    for t in range(DEC_SEQ):
        put(qe_s, t, slab(q, t) * jnp.exp(b[t]))
        put(ke_s, t, slab(k, t) * jnp.exp(-b[t]))
        put(kd_s, t, slab(k, t) * jnp.exp(b_last - b[t]))
        put(v_s, t, slab(v, t))

    stride = seqs // SEQ_GROUP
    n_rows = SEQ_GROUP * DEC_SEQ
    member = lax.broadcasted_iota(jnp.int32, (n_rows, HG_DK), 0) % SEQ_GROUP
    lanes = lambda g: slice(g * HG_DK, (g + 1) * HG_DK)
    per_member = lambda a: jnp.concatenate(
        [jnp.where(member == g, a, 0.0) for g in range(SEQ_GROUP)], axis=1)
    ri = lax.broadcasted_iota(jnp.int32, (n_rows, n_rows), 0)
    ci = lax.broadcasted_iota(jnp.int32, (n_rows, n_rows), 1)
    causal = (ri // SEQ_GROUP >= ci // SEQ_GROUP) & (ri % SEQ_GROUP == ci % SEQ_GROUP)
    for s in range(stride):
        rows = pl.ds(s, n_rows, stride=stride)
        for h in range(HG_HEADS):
            qe_g = qe_s[h, rows, :].astype(BF16)
            ke_g = ke_s[h, rows, :].astype(BF16)
            kd_g = kd_s[h, rows, :].astype(BF16)
            v_g = v_s[h, rows, :]
            s_g = jnp.concatenate([s_in_ref[s + g * stride, h] for g in range(SEQ_GROUP)], axis=1)
            att = jnp.where(causal, _dot_nt(qe_g, ke_g), 0.0).astype(BF16)
            inter_all = _dot(qe_g, s_g.astype(BF16))
            inter = inter_all[:, lanes(0)]
            for g in range(1, SEQ_GROUP):
                inter = jnp.where(member == g, inter_all[:, lanes(g)], inter)
            o_s[h, rows, :] = _dot(att, v_g.astype(BF16)) + inter
            col = lambda g: decay_t[h][:, s + g * stride:s + g * stride + 1]
            decay_g = jnp.concatenate([jnp.broadcast_to(col(g), (HG_DK, HG_DK))
                                       for g in range(SEQ_GROUP)], axis=1)
            new = decay_g * s_g + _dot_tn(kd_g, per_member(v_g).astype(BF16))
            for g in range(SEQ_GROUP):
                s_out_ref[s + g * stride, h] = new[:, lanes(g)]
    hgw = hgw_ref[...]
    o_a = jnp.concatenate(
        [_head_norm(o_s[h], hgw, gate[:, h * HG_DK:(h + 1) * HG_DK]) for h in range(HG_HEADS)],
        axis=-1)

    full = lambda i: cbuf_in_ref[i] if i < CV_BUF else slab(u, i - CV_BUF)
    acc = [jnp.broadcast_to(cb_ref[...], (seqs, CV_WIDTH)) for _ in range(DEC_SEQ)]
    for i in range(CV_BUF + DEC_SEQ):
        f_i = full(i)
        for t in range(DEC_SEQ):
            if 0 <= i - t < CV_KERNEL:
                acc[t] = acc[t] + cw_ref[i - t:i - t + 1, :] * f_i
    for i in range(CV_BUF):
        cbuf_out_ref[i] = full(i + DEC_SEQ)
    o_b = _group_norm_swish(jnp.concatenate(acc, axis=0), gmat_ref, gng_ref[...], gnb_ref[...])

    mix = _out_projection(jnp.concatenate([o_a, o_b], axis=-1).astype(BF16), w_out_ref, b_out_ref)
    y = ALPHA * x + (1.0 + g1) * mix.reshape(DEC_SEQ, seqs, D_MODEL)
    x1_ref[...] = _layer_norm(y, lng_ref[...], lnb_ref[...])


def _mixer_sample(x, mod, s_in, cbuf_in, lb_logits, w_in, b_in, hgw, conv_w, conv_b, gn_g, gn_b,
                  gmat, w_out, b_out, ln_g, ln_b, *, seqs):
    n_seq = s_in.shape[0]
    rows = seqs * DEC_SEQ
    kernel = functools.partial(_mixer_sample_kernel, seqs=seqs)
    return pl.pallas_call(
        kernel,
        grid=(n_seq // seqs,),
        in_specs=[
            pl.BlockSpec((seqs, DEC_SEQ, D_MODEL), lambda i: (i, 0, 0)),
            pl.BlockSpec((seqs, 3 * D_MODEL), lambda i: (i, 0)),
            pl.BlockSpec((seqs, HG_HEADS, HG_DK, HG_DK), lambda i: (i, 0, 0, 0)),
            pl.BlockSpec((CV_BUF, seqs, CV_WIDTH), lambda i: (0, i, 0)),
            _const_spec(lb_logits.shape),
            _const_spec(w_in.shape),
            _const_spec(b_in.shape),
            _const_spec(hgw.shape),
            _const_spec(conv_w.shape),
            _const_spec(conv_b.shape),
            _const_spec(gn_g.shape),
            _const_spec(gn_b.shape),
            _const_spec(gmat.shape),
            _const_spec(w_out.shape),
            _const_spec(b_out.shape),
            _const_spec(ln_g.shape),
            _const_spec(ln_b.shape),
        ],
        out_specs=[
            pl.BlockSpec((DEC_SEQ, seqs, D_MODEL), lambda i: (0, i, 0)),
            pl.BlockSpec((seqs, HG_HEADS, HG_DK, HG_DK), lambda i: (i, 0, 0, 0)),
            pl.BlockSpec((CV_BUF, seqs, CV_WIDTH), lambda i: (0, i, 0)),
        ],
        out_shape=[
            jax.ShapeDtypeStruct((DEC_SEQ, n_seq, D_MODEL), F32),
            jax.ShapeDtypeStruct(s_in.shape, F32),
            jax.ShapeDtypeStruct(cbuf_in.shape, F32),
        ],
        scratch_shapes=[pltpu.VMEM((HG_HEADS, rows, HG_DK), F32)] * 5,
        compiler_params=pltpu.CompilerParams(
            dimension_semantics=("arbitrary",), vmem_limit_bytes=BIG_VMEM_LIMIT_BYTES),
        name="mixer_sample",
    )(x, mod, s_in, cbuf_in, lb_logits, w_in, b_in, hgw, conv_w, conv_b, gn_g, gn_b, gmat,
      w_out, b_out, ln_g, ln_b)


def _mlp_kernel(x_ref, sh_ref, sc_ref, g_ref, w_up_ref, b_up_ref, w_down_ref, b_down_ref,
                lng_ref, lnb_ref, o_ref):
    x = x_ref[...]
    hb = (x * (1.0 + sc_ref[...]) + sh_ref[...]).astype(BF16).reshape(-1, D_MODEL)
    ff = jnp.broadcast_to(b_down_ref[...], hb.shape)
    for c0 in range(0, D_FF, FF_CHUNK):
        cs = slice(c0, c0 + FF_CHUNK)
        a = jnp.maximum(_dot(hb, w_up_ref[:, cs]) + b_up_ref[:, cs], 0.0)
        ff = ff + _dot((a * a).astype(BF16), w_down_ref[cs, :])
    y = ALPHA * x + (1.0 + g_ref[...]) * ff.reshape(x.shape)
    out = _layer_norm(y, lng_ref[...], lnb_ref[...])
    for t in range(DEC_SEQ):
        o_ref[:, t, :] = out[t]


def _mlp_sample(x, mod, w_up, b_up, w_down, b_down, ln_g, ln_b):
    x_spec = pl.BlockSpec(x.shape, lambda i: (0, 0, 0))
    out_shape = (x.shape[1], x.shape[0], x.shape[2])
    mod_specs = [pl.BlockSpec((x.shape[1], D_MODEL), functools.partial(lambda i, r: (0, r), r=r))
                 for r in (3, 4, 5)]
    return pl.pallas_call(
        _mlp_kernel,
        grid=(1,),
        in_specs=[x_spec] + mod_specs + [
            _const_spec(w_up.shape),
            _const_spec(b_up.shape),
            _const_spec(w_down.shape),
            _const_spec(b_down.shape),
            _const_spec(ln_g.shape),
            _const_spec(ln_b.shape),
        ],
        out_specs=pl.BlockSpec(out_shape, lambda i: (0, 0, 0)),
        out_shape=jax.ShapeDtypeStruct(out_shape, F32),
        compiler_params=pltpu.CompilerParams(
            dimension_semantics=("arbitrary",), vmem_limit_bytes=VMEM_LIMIT_BYTES),
        name="mlp_sample",
    )(x, mod, mod, mod, w_up, b_up, w_down, b_down, ln_g, ln_b)


PROMPT_TILE = 512
SAMPLE_SEQS = 32


def kernel(x_prompt, x_sample, c_prompt, c_sample, state_hgrn, state_conv, lb_logits, w_in, b_in,
           hg_norm_w, conv_w, conv_b, gn_g, gn_b, w_out, b_out, ln1_g, ln1_b, w_up, b_up, w_down,
           b_down, ln2_g, ln2_b, w_ada, b_ada):
    assert w_in.shape[0] == DEPTH
    batch, seq, _ = x_prompt.shape
    dec_batch, dec_seq, _ = x_sample.shape
    assert dec_seq == DEC_SEQ

    group = jnp.arange(MXU_WIDTH, dtype=jnp.int32) // (CV_WIDTH // CV_GROUPS)
    first = jnp.arange(MXU_WIDTH, dtype=jnp.int32)[:, None] == group[None, :] * (CV_WIDTH // CV_GROUPS)
    gmat = jnp.stack([jnp.where(first, 1.0, 0.0),
                      jnp.where(group[:, None] == group[None, :], CV_GROUPS / CV_WIDTH, 0.0)]
                     ).astype(BF16)

    xp = x_prompt
    xs = x_sample
    hp, cp, hs, cs = [], [], [], []
    for l in range(DEPTH):
        row = lambda a: a[l][None, :]
        w_in_b, w_out_b = _to_bf16(w_in[l]), _to_bf16(w_out[l])
        w_up_b, w_down_b = _to_bf16(w_up[l]), _to_bf16(w_down[l])

        mod_p, mod_s = _adaln(c_prompt, c_sample, w_ada[l], row(b_ada))
        mod_p = mod_p.reshape(batch, 6, D_MODEL)

        mixer_params = (lb_logits, w_in_b, row(b_in), row(hg_norm_w), conv_w[l], row(conv_b),
                        row(gn_g), row(gn_b), gmat, w_out_b, row(b_out), row(ln1_g), row(ln1_b))
        mlp_params = (w_up_b, row(b_up), w_down_b, row(b_down), row(ln2_g), row(ln2_b))
        xp, sp, bp = _prompt_layer(xp, mod_p, *mixer_params, *mlp_params, tile=PROMPT_TILE)
        xs, ss, bs = _mixer_sample(xs, mod_s, state_hgrn[l], jnp.transpose(state_conv[l], (1, 0, 2)),
                                   *mixer_params, seqs=SAMPLE_SEQS)
        xs = _mlp_sample(xs, mod_s, *mlp_params)

        hp.append(sp); cp.append(bp); hs.append(ss); cs.append(jnp.transpose(bs, (1, 0, 2)))

    return (xp, xs, jnp.stack(hp), jnp.stack(cp), jnp.stack(hs),
            jnp.stack(cs))
```
